```python
import math
import jax, jax.numpy as jnp
from jax import lax
import numpy as np

D_MODEL = 2048
BATCH = 2
SEQ = 4096
DEPTH = 4
DEC_BATCH = 8
DEC_SEQ = 1
PAST_LEN = 16384
PAGE_SIZE = 128

N_MIXERS = 3
ROPE_THETA = 500000.0
EPS = 1e-6
NEG_INF = -1e30
FORCE = 1e9
Q_BLOCK = 128
SEL_Q_BLOCK = 32
D_FF = 256 * math.ceil(8 * D_MODEL / 3 / 256)

A_HEADS = D_MODEL // 128
A_HALF = 64
A_VDIM = 2 * A_HALF
A_KV_HEADS = 4
A_GROUP = A_HEADS // A_KV_HEADS
A_Q = A_HEADS * 2 * A_HALF
A_K = A_KV_HEADS * 2 * A_HALF
A_IN = A_Q + A_K + A_KV_HEADS * A_VDIM

B_HEADS = D_MODEL // 128
B_HD = 128
B_KV_HEADS = 4
B_GROUP = B_HEADS // B_KV_HEADS
B_KV = B_KV_HEADS * B_HD
B_Q = B_HEADS * B_HD
B_IN = B_Q + 6 * B_KV + 3 * B_HEADS
CMP_STRIDE = 16
CMP_LEN = 2 * CMP_STRIDE
SEL_BLOCK = 64
SEL_N = 16
WINDOW = 512

C_HEADS = D_MODEL // 128
C_HD = 128
C_KV_HEADS = 4
C_GROUP = C_HEADS // C_KV_HEADS
C_Q = C_HEADS * C_HD
C_KV = C_KV_HEADS * C_HD
IDX_HEADS = 16
IDX_DIM = 64
IDX_TOPK_MAX = 256
C_IN = C_Q + 2 * C_KV + IDX_HEADS * IDX_DIM + IDX_DIM + IDX_HEADS

MEM_LEN = 256
X_HEADS = 4
X_HD = 128
X_W = X_HEADS * X_HD

N_A = len(range(0, DEPTH, N_MIXERS))
N_B = len(range(1, DEPTH, N_MIXERS))
N_C = len(range(2, DEPTH, N_MIXERS))

STATE_NAMES = ('a_k', 'a_v', 'b_cmp_k', 'b_cmp_v', 'b_sel_k', 'b_sel_v', 'b_win_k', 'b_win_v', 'c_k', 'c_v', 'c_idx_k')

kernel_name = 'hybrid_diff_nsa_dsa_decoder_step'


def rms_norm(x, g):
    xf = x.astype(jnp.float32)
    y = xf * lax.rsqrt(jnp.mean(xf * xf, axis=-1, keepdims=True) + EPS)
    return (y * g.astype(jnp.float32)).astype(x.dtype)


def swiglu(h, w_in, w_out):
    gate, up = jnp.split(h @ w_in, 2, axis=-1)
    return (jax.nn.silu(gate) * up) @ w_out


def rope(x, pos):
    dh = x.shape[-1]
    rot = dh // 4
    half = rot // 2
    inv = ROPE_THETA ** (-jnp.arange(half, dtype=jnp.float32) / half)
    ang = pos.astype(jnp.float32)[:, None] * inv[None, :]
    shp = (1, pos.shape[0]) + (1,) * (x.ndim - 3) + (half,)
    cos = jnp.cos(ang).reshape(shp)
    sin = jnp.sin(ang).reshape(shp)
    x1 = x[..., :half].astype(jnp.float32)
    x2 = x[..., half:rot].astype(jnp.float32)
    r = jnp.concatenate([x1 * cos - x2 * sin, x2 * cos + x1 * sin], axis=-1).astype(x.dtype)
    return jnp.concatenate([r, x[..., rot:]], axis=-1)


def masked_softmax(s, mask):
    s = jnp.where(mask, s.astype(jnp.float32), NEG_INF)
    p = jax.nn.softmax(s, axis=-1)
    return jnp.where(jnp.any(mask, axis=-1, keepdims=True), p, 0.0)


def over_query_blocks(fn, blk, pos, *qs):
    t = pos.shape[0]
    if t <= blk:
        return fn(pos, *qs)
    n = t // blk

    def split(a):
        return jnp.moveaxis(a.reshape((a.shape[0], n, blk) + a.shape[2:]), 1, 0)

    out = lax.map(lambda args: fn(*args), (pos.reshape(n, blk),) + tuple(split(a) for a in qs))
    out = jnp.moveaxis(out, 0, 1)
    return out.reshape((out.shape[0], t) + out.shape[3:])


def take_rows(src, idx, head=None):
    b = idx.shape[0]
    fi = jnp.clip(idx, 0, src.shape[1] - 1).reshape(b, -1)
    if head is None:
        rows = jax.vmap(lambda s, i: s[i])(src, fi)
        return rows.reshape(idx.shape + src.shape[2:])
    rows = jax.vmap(lambda s, i, h: s[i, h])(src, fi, head.reshape(b, -1))
    return rows.reshape(idx.shape + src.shape[3:])


def paged_rows(pool, page_table, idx, head=None):
    b = idx.shape[0]
    past = page_table.shape[1] * PAGE_SIZE
    fi = jnp.clip(idx, 0, past - 1).reshape(b, -1)
    phys = jnp.take_along_axis(page_table, fi // PAGE_SIZE, axis=1)
    off = fi % PAGE_SIZE
    if head is None:
        return pool[phys, off].reshape(idx.shape + pool.shape[2:])
    return pool[phys, off, head.reshape(b, -1)].reshape(idx.shape + pool.shape[3:])


def gather_past(pool, page_table):
    g = pool[page_table]
    return g.reshape((g.shape[0], -1) + g.shape[3:])


def make_fetch(new_rows, pool=None, page_table=None):
    if pool is None:
        return lambda idx, head=None: take_rows(new_rows, idx, head)
    past = page_table.shape[1] * PAGE_SIZE

    def fetch(idx, head=None):
        rp = paged_rows(pool, page_table, idx, head)
        rn = take_rows(new_rows, idx - past, head)
        in_past = (idx < past).reshape(idx.shape + (1,) * (rp.ndim - idx.ndim))
        return jnp.where(in_past, rp, rn)
    return fetch


def lambda_init(layer):
    return 0.8 - 0.6 * math.exp(-0.3 * layer)


def diff_attention(h, pos, w_in, w_out, lam, subln, lam_init, past):
    b, t, _ = h.shape
    q, k, v = jnp.split(h @ w_in, [A_Q, A_Q + A_K], axis=-1)
    q = rope(q.reshape(b, t, A_HEADS * 2, A_HALF), pos).reshape(b, t, A_KV_HEADS, A_GROUP, 2, A_HALF)
    k = rope(k.reshape(b, t, A_KV_HEADS * 2, A_HALF), pos).reshape(b, t, A_KV_HEADS, 2 * A_HALF)
    v = v.reshape(b, t, A_KV_HEADS, A_VDIM)
    if past is None:
        k_all, v_all = k, v
    else:
        pk, pv, pt = past
        k_all = jnp.concatenate([gather_past(pk, pt), k], axis=1)
        v_all = jnp.concatenate([gather_past(pv, pt), v], axis=1)
    L = k_all.shape[1]
    kpos = jnp.arange(L)
    k2 = k_all.reshape(b, L, A_KV_HEADS, 2, A_HALF)
    lam_f = (jnp.exp(jnp.sum(lam[0] * lam[1])) - jnp.exp(jnp.sum(lam[2] * lam[3]))).astype(jnp.float32) + lam_init
    scale = A_HALF ** -0.5

    def block(qpos, qb):
        s = jnp.einsum('btgrcd,bsgcd->bcgrts', qb, k2) * scale
        p = masked_softmax(s, kpos[None, :] <= qpos[:, None])
        a = p[:, 0] - lam_f * p[:, 1]
        return jnp.einsum('bgrts,bsgd->btgrd', a.astype(v_all.dtype), v_all)

    o = over_query_blocks(block, Q_BLOCK, pos, q)
    o = rms_norm(o, subln) * (1.0 - lam_init)
    return o.reshape(b, t, A_HEADS * A_VDIM) @ w_out, (k, v)


def compress(rows, w, bias):
    b, L = rows.shape[:2]
    nch = L // CMP_STRIDE
    ch = rows[:, :nch * CMP_STRIDE].reshape((b, nch, CMP_STRIDE) + rows.shape[2:])
    first = jnp.einsum('bcjgd,jde->bcge', ch, w[:CMP_STRIDE])
    second = jnp.einsum('bcjgd,jde->bcge', ch, w[CMP_STRIDE:])
    return first[:, :-1] + second[:, 1:] + bias


def nsa_attention(h, pos, w_in, w_out, cmp_w, cmp_b, past):
    b, t, _ = h.shape
    q, kc, vc, ks, vs, kw, vw, gates = jnp.split(h @ w_in, [B_Q + i * B_KV for i in range(7)], axis=-1)
    shp = (b, t, B_KV_HEADS, B_HD)
    q = q.reshape(b, t, B_HEADS, B_HD)
    q_raw = q.reshape(b, t, B_KV_HEADS, B_GROUP, B_HD)
    q_rot = rope(q, pos).reshape(b, t, B_KV_HEADS, B_GROUP, B_HD)
    kc, vc, vs, vw = (a.reshape(shp) for a in (kc, vc, vs, vw))
    ks = rope(ks.reshape(shp), pos)
    kw = rope(kw.reshape(shp), pos)
    g = jax.nn.sigmoid(gates.astype(jnp.float32)).reshape(b, t, B_KV_HEADS, B_GROUP, 3)
    if past is None:
        kc_all, vc_all = kc, vc
        fetch_k, fetch_v = make_fetch(ks), make_fetch(vs)
    else:
        pkc, pvc, pks, pvs, bkw, bvw, pt = past
        kc_all = jnp.concatenate([gather_past(pkc, pt), kc], axis=1)
        vc_all = jnp.concatenate([gather_past(pvc, pt), vc], axis=1)
        fetch_k, fetch_v = make_fetch(ks, pks, pt), make_fetch(vs, pvs, pt)
    L = kc_all.shape[1]
    scale = B_HD ** -0.5

    kcb = compress(kc_all, cmp_w[0], cmp_b[0])
    vcb = compress(vc_all, cmp_w[1], cmp_b[1])
    nblk = kcb.shape[1]
    blk_end = jnp.arange(nblk) * CMP_STRIDE + CMP_LEN - 1
    s = jnp.einsum('btgrd,bngd->btgrn', q_raw, kcb) * scale
    p_cmp = masked_softmax(s, (blk_end[None, :] <= pos[:, None])[None, :, None, None, :])
    o_cmp = jnp.einsum('btgrn,bngd->btgrd', p_cmp.astype(vcb.dtype), vcb)

    grp = jnp.sum(p_cmp, axis=3)
    nch = nblk + 1
    pp = jnp.pad(grp, ((0, 0), (0, 0), (0, 0), (1, 1)))
    chunk = pp[..., 1:] + pp[..., :-1]
    n_sel = -(-L // SEL_BLOCK)
    cps = SEL_BLOCK // CMP_STRIDE
    chunk = jnp.pad(chunk, ((0, 0), (0, 0), (0, 0), (0, n_sel * cps - nch)))
    sel_score = chunk.reshape(b, t, B_KV_HEADS, n_sel, cps).sum(-1)
    j = jnp.arange(n_sel)[None, :]
    tq = pos[:, None]
    cur = tq // SEL_BLOCK
    valid = (j * SEL_BLOCK <= tq)[None, :, None, :]
    forced = ((j == 0) | (j == cur) | (j == cur - 1))[None, :, None, :]
    sel_score = jnp.where(forced, FORCE, jnp.where(valid, sel_score, NEG_INF))
    n_top = min(SEL_N, n_sel)
    _, sel_idx = lax.top_k(sel_score, n_top)
    offs = jnp.arange(SEL_BLOCK)
    gidx = jnp.arange(B_KV_HEADS)[None, None, :, None]

    def sel_block(qpos, qb, ib):
        kp = (ib[..., None] * SEL_BLOCK + offs).reshape(ib.shape[:3] + (n_top * SEL_BLOCK,))
        head = jnp.broadcast_to(gidx, kp.shape)
        kg = fetch_k(kp, head)
        vg = fetch_v(kp, head)
        sc = jnp.einsum('btgrd,btgnd->btgrn', qb, kg) * scale
        p = masked_softmax(sc, (kp <= qpos[None, :, None, None])[:, :, :, None, :])
        return jnp.einsum('btgrn,btgnd->btgrd', p.astype(vg.dtype), vg)

    o_sel = over_query_blocks(sel_block, SEL_Q_BLOCK, pos, q_rot, sel_idx)

    def win_attend(qpos, qb, kb, vb, kpos):
        sc = jnp.einsum('btgrd,bsgd->btgrs', qb, kb) * scale
        dt = qpos[:, None] - kpos[None, :]
        mask = ((dt >= 0) & (dt <= WINDOW) & (kpos >= 0)[None, :])[None, :, None, None, :]
        p = masked_softmax(sc, mask)
        return jnp.einsum('btgrs,bsgd->btgrd', p.astype(vb.dtype), vb)

    if past is None:
        kpad = jnp.pad(kw, ((0, 0), (WINDOW, 0), (0, 0), (0, 0)))
        vpad = jnp.pad(vw, ((0, 0), (WINDOW, 0), (0, 0), (0, 0)))

        def win_block(qpos, qb):
            n = qpos.shape[0] + WINDOW
            s0 = qpos[0]
            kb = lax.dynamic_slice_in_dim(kpad, s0, n, axis=1)
            vb = lax.dynamic_slice_in_dim(vpad, s0, n, axis=1)
            return win_attend(qpos, qb, kb, vb, s0 - WINDOW + jnp.arange(n))

        o_win = over_query_blocks(win_block, Q_BLOCK, pos, q_rot)
        keep = min(WINDOW, t)
        win_state = (kw[:, t - keep:], vw[:, t - keep:])
    else:
        kb = jnp.concatenate([bkw, kw], axis=1)
        vb = jnp.concatenate([bvw, vw], axis=1)
        wb = bkw.shape[1]
        kpos = pos[0] - wb + jnp.arange(wb + t)
        o_win = over_query_blocks(lambda qp, qb: win_attend(qp, qb, kb, vb, kpos), Q_BLOCK, pos, q_rot)
        win_state = (kb[:, t:], vb[:, t:])

    o = o_cmp * g[..., 0:1] + o_sel * g[..., 1:2] + o_win * g[..., 2:3]
    y = o.reshape(b, t, B_Q) @ w_out
    return y, (kc, vc, ks, vs, win_state[0], win_state[1])


def dsa_attention(h, pos, w_in, w_out, idx_knorm, past):
    b, t, _ = h.shape
    o1 = C_Q + 2 * C_KV + IDX_HEADS * IDX_DIM
    q, k, v, iq, ik, iw = jnp.split(h @ w_in, [C_Q, C_Q + C_KV, C_Q + 2 * C_KV, o1, o1 + IDX_DIM], axis=-1)
    q = rope(q.reshape(b, t, C_HEADS, C_HD), pos).reshape(b, t, C_KV_HEADS, C_GROUP, C_HD)
    k = rope(k.reshape(b, t, C_KV_HEADS, C_HD), pos)
    v = v.reshape(b, t, C_KV_HEADS, C_HD)
    iq = rope(iq.reshape(b, t, IDX_HEADS, IDX_DIM), pos)
    ik = rope(rms_norm(ik, idx_knorm)[:, :, None, :], pos)[:, :, 0]
    iw = iw * (IDX_HEADS ** -0.5)
    if past is None:
        ik_all = ik
        fetch_k, fetch_v = make_fetch(k), make_fetch(v)
    else:
        pk, pv, pik, pt = past
        ik_all = jnp.concatenate([gather_past(pik, pt), ik], axis=1)
        fetch_k, fetch_v = make_fetch(k, pk, pt), make_fetch(v, pv, pt)
    L = ik_all.shape[1]
    top = min(IDX_TOPK_MAX, L // 4)
    kpos = jnp.arange(L)
    scale = C_HD ** -0.5

    def block(qpos, qb, iqb, iwb):
        sc = jnp.einsum('bthd,bsd->bths', iqb, ik_all).astype(jnp.float32) * (IDX_DIM ** -0.5)
        score = jnp.einsum('bths,bth->bts', jax.nn.relu(sc), iwb.astype(jnp.float32))
        score = jnp.where((kpos[None, :] <= qpos[:, None])[None], score, NEG_INF)
        _, sel = lax.top_k(score, top)
        kg = fetch_k(sel)
        vg = fetch_v(sel)
        s = jnp.einsum('btgrd,btngd->btgrn', qb, kg) * scale
        p = masked_softmax(s, (sel <= qpos[None, :, None])[:, :, None, None, :])
        return jnp.einsum('btgrn,btngd->btgrd', p.astype(vg.dtype), vg)

    o = over_query_blocks(block, Q_BLOCK, pos, q, iq, iw)
    return o.reshape(b, t, C_Q) @ w_out, (k, v, ik)


def mem_kv(mem, w_kv):
    kv = (mem @ w_kv).reshape(mem.shape[0], mem.shape[1], 2, X_HEADS, X_HD)
    return kv[:, :, 0], kv[:, :, 1]


def cross_attention(h, mk, mv, w_q, w_o):
    b, t, _ = h.shape
    q = (h @ w_q).reshape(b, t, X_HEADS, X_HD)
    s = jnp.einsum('bthd,bmhd->bhtm', q, mk).astype(jnp.float32) * (X_HD ** -0.5)
    p = jax.nn.softmax(s, axis=-1).astype(mv.dtype)
    o = jnp.einsum('bhtm,bmhd->bthd', p, mv)
    return o.reshape(b, t, X_W) @ w_o


def trunk(x, pos, mem_k, mem_v, W, past):
    new = {nm: [] for nm in STATE_NAMES}
    for l in range(DEPTH):
        x = x + 0.5 * swiglu(rms_norm(x, W['norm_ffn1'][l]), W['ffn1_w_in'][l], W['ffn1_w_out'][l])
        hn = rms_norm(x, W['norm_mix'][l])
        i = l // N_MIXERS
        kind = l % N_MIXERS
        if kind == 0:
            cache = None if past is None else (past['a_k'][i], past['a_v'][i], past['page_table'])
            y, rows = diff_attention(hn, pos, W['a_w_in'][i], W['a_w_out'][i], W['a_lambda'][i],
                                     W['a_subln'][i], lambda_init(l), cache)
            names = ('a_k', 'a_v')
        elif kind == 1:
            cache = None if past is None else (past['b_cmp_k'][i], past['b_cmp_v'][i], past['b_sel_k'][i],
                                               past['b_sel_v'][i], past['b_win_k'][i], past['b_win_v'][i],
                                               past['page_table'])
            y, rows = nsa_attention(hn, pos, W['b_w_in'][i], W['b_w_out'][i], W['b_cmp_w'][i], W['b_cmp_b'][i], cache)
            names = ('b_cmp_k', 'b_cmp_v', 'b_sel_k', 'b_sel_v', 'b_win_k', 'b_win_v')
        else:
            cache = None if past is None else (past['c_k'][i], past['c_v'][i], past['c_idx_k'][i], past['page_table'])
            y, rows = dsa_attention(hn, pos, W['c_w_in'][i], W['c_w_out'][i], W['c_idx_knorm'][i], cache)
            names = ('c_k', 'c_v', 'c_idx_k')
        for nm, r in zip(names, rows):
            new[nm].append(r)
        x = x + y
        x = x + cross_attention(rms_norm(x, W['norm_cross'][l]), mem_k[l], mem_v[l], W['x_w_q'][l], W['x_w_o'][l])
        x = x + 0.5 * swiglu(rms_norm(x, W['norm_ffn2'][l]), W['ffn2_w_in'][l], W['ffn2_w_out'][l])
    return rms_norm(x, W['final_norm']), {nm: jnp.stack(v) for nm, v in new.items()}


def setup_inputs(seed: int = 0) -> dict:
    key = jax.random.key(seed)
    keys = iter(jax.random.split(key, 64))
    n_pages = PAST_LEN // PAGE_SIZE
    n_pool = (DEC_BATCH * n_pages * 5) // 4
    win_buf = min(WINDOW, PAST_LEN)

    def nrm(shape, scale=1.0):
        return jax.random.normal(next(keys), shape, jnp.float32) * scale

    def gain(shape):
        return 1.0 + 0.02 * nrm(shape)

    perm = jax.random.permutation(next(keys), n_pool)
    page_table = perm[:DEC_BATCH * n_pages].reshape(DEC_BATCH, n_pages).astype(jnp.int32)
    return {
        'x_prompt': nrm((BATCH, SEQ, D_MODEL)),
        'x_sample': nrm((DEC_BATCH, DEC_SEQ, D_MODEL)),
        'cache_a_k': nrm((N_A, n_pool, PAGE_SIZE, A_KV_HEADS, 2 * A_HALF)),
        'cache_a_v': nrm((N_A, n_pool, PAGE_SIZE, A_KV_HEADS, A_VDIM)),
        'cache_b_cmp_k': nrm((N_B, n_pool, PAGE_SIZE, B_KV_HEADS, B_HD)),
        'cache_b_cmp_v': nrm((N_B, n_pool, PAGE_SIZE, B_KV_HEADS, B_HD)),
        'cache_b_sel_k': nrm((N_B, n_pool, PAGE_SIZE, B_KV_HEADS, B_HD)),
        'cache_b_sel_v': nrm((N_B, n_pool, PAGE_SIZE, B_KV_HEADS, B_HD)),
        'state_b_win_k': nrm((N_B, DEC_BATCH, win_buf, B_KV_HEADS, B_HD)),
        'state_b_win_v': nrm((N_B, DEC_BATCH, win_buf, B_KV_HEADS, B_HD)),
        'cache_c_k': nrm((N_C, n_pool, PAGE_SIZE, C_KV_HEADS, C_HD)),
        'cache_c_v': nrm((N_C, n_pool, PAGE_SIZE, C_KV_HEADS, C_HD)),
        'cache_c_idx_k': nrm((N_C, n_pool, PAGE_SIZE, IDX_DIM)),
        'cache_mem_k': nrm((DEPTH, DEC_BATCH, MEM_LEN, X_HEADS, X_HD)),
        'cache_mem_v': nrm((DEPTH, DEC_BATCH, MEM_LEN, X_HEADS, X_HD)),
        'page_table': page_table,
        'mem_prompt': nrm((BATCH, MEM_LEN, D_MODEL)),
        'norm_ffn1': gain((DEPTH, D_MODEL)),
        'norm_mix': gain((DEPTH, D_MODEL)),
        'norm_cross': gain((DEPTH, D_MODEL)),
        'norm_ffn2': gain((DEPTH, D_MODEL)),
        'final_norm': gain((D_MODEL,)),
        'ffn1_w_in': nrm((DEPTH, D_MODEL, 2 * D_FF), D_MODEL ** -0.5),
        'ffn1_w_out': nrm((DEPTH, D_FF, D_MODEL), D_FF ** -0.5),
        'ffn2_w_in': nrm((DEPTH, D_MODEL, 2 * D_FF), D_MODEL ** -0.5),
        'ffn2_w_out': nrm((DEPTH, D_FF, D_MODEL), D_FF ** -0.5),
        'x_w_q': nrm((DEPTH, D_MODEL, X_W), D_MODEL ** -0.5),
        'x_w_kv': nrm((DEPTH, D_MODEL, 2 * X_W), D_MODEL ** -0.5),
        'x_w_o': nrm((DEPTH, X_W, D_MODEL), X_W ** -0.5),
        'a_w_in': nrm((N_A, D_MODEL, A_IN), D_MODEL ** -0.5),
        'a_w_out': nrm((N_A, A_HEADS * A_VDIM, D_MODEL), (A_HEADS * A_VDIM) ** -0.5),
        'a_lambda': nrm((N_A, 4, A_HALF), 0.1),
        'a_subln': gain((N_A, A_VDIM)),
        'b_w_in': nrm((N_B, D_MODEL, B_IN), D_MODEL ** -0.5),
        'b_w_out': nrm((N_B, B_Q, D_MODEL), B_Q ** -0.5),
        'b_cmp_w': nrm((N_B, 2, CMP_LEN, B_HD, B_HD), (CMP_LEN * B_HD) ** -0.5),
        'b_cmp_b': nrm((N_B, 2, B_HD), 0.02),
        'c_w_in': nrm((N_C, D_MODEL, C_IN), D_MODEL ** -0.5),
        'c_w_out': nrm((N_C, C_Q, D_MODEL), C_Q ** -0.5),
        'c_idx_knorm': gain((N_C, IDX_DIM)),
    }


def reference(x_prompt, x_sample, cache_a_k, cache_a_v, cache_b_cmp_k, cache_b_cmp_v, cache_b_sel_k,
              cache_b_sel_v, state_b_win_k, state_b_win_v, cache_c_k, cache_c_v, cache_c_idx_k,
              cache_mem_k, cache_mem_v, page_table, mem_prompt, norm_ffn1, norm_mix, norm_cross, norm_ffn2,
              final_norm, ffn1_w_in, ffn1_w_out, ffn2_w_in, ffn2_w_out, x_w_q, x_w_kv, x_w_o, a_w_in, a_w_out,
              a_lambda, a_subln, b_w_in, b_w_out, b_cmp_w, b_cmp_b, c_w_in, c_w_out, c_idx_knorm):
    W = dict(norm_ffn1=norm_ffn1, norm_mix=norm_mix, norm_cross=norm_cross, norm_ffn2=norm_ffn2,
             final_norm=final_norm, ffn1_w_in=ffn1_w_in, ffn1_w_out=ffn1_w_out, ffn2_w_in=ffn2_w_in,
             ffn2_w_out=ffn2_w_out, x_w_q=x_w_q, x_w_o=x_w_o, a_w_in=a_w_in, a_w_out=a_w_out,
             a_lambda=a_lambda, a_subln=a_subln, b_w_in=b_w_in, b_w_out=b_w_out, b_cmp_w=b_cmp_w,
             b_cmp_b=b_cmp_b, c_w_in=c_w_in, c_w_out=c_w_out, c_idx_knorm=c_idx_knorm)

    pos_p = jnp.arange(x_prompt.shape[1], dtype=jnp.int32)
    mkv = [mem_kv(mem_prompt, x_w_kv[l]) for l in range(DEPTH)]
    p_mem_k = jnp.stack([m[0] for m in mkv])
    p_mem_v = jnp.stack([m[1] for m in mkv])
    y_prompt, ps = trunk(x_prompt, pos_p, p_mem_k, p_mem_v, W, None)

    past = dict(a_k=cache_a_k, a_v=cache_a_v, b_cmp_k=cache_b_cmp_k, b_cmp_v=cache_b_cmp_v,
                b_sel_k=cache_b_sel_k, b_sel_v=cache_b_sel_v, b_win_k=state_b_win_k, b_win_v=state_b_win_v,
                c_k=cache_c_k, c_v=cache_c_v, c_idx_k=cache_c_idx_k, page_table=page_table)
    pos_s = page_table.shape[1] * PAGE_SIZE + jnp.arange(x_sample.shape[1], dtype=jnp.int32)
    y_sample, ss = trunk(x_sample, pos_s, cache_mem_k, cache_mem_v, W, past)

    return (y_prompt, y_sample,
            ps['a_k'], ps['a_v'], ps['b_cmp_k'], ps['b_cmp_v'], ps['b_sel_k'], ps['b_sel_v'],
            ps['b_win_k'], ps['b_win_v'], ps['c_k'], ps['c_v'], ps['c_idx_k'], p_mem_k, p_mem_v,
            ss['a_k'], ss['a_v'], ss['b_cmp_k'], ss['b_cmp_v'], ss['b_sel_k'], ss['b_sel_v'],
            ss['b_win_k'], ss['b_win_v'], ss['c_k'], ss['c_v'], ss['c_idx_k'])
```

```python
import functools
import math

import jax
import jax.numpy as jnp
from jax import lax
from jax.experimental import pallas as pl
from jax.experimental.pallas import tpu as pltpu

F32 = jnp.float32
BF16 = jnp.bfloat16
I32 = jnp.int32

D_MODEL = 2048
DEPTH = 4
PAGE_SIZE = 128
N_MIXERS = 3
ROPE_THETA = 500000.0
EPS = 1e-6
NEG_INF = -1e30
FORCE = 1e9
D_FF = 256 * math.ceil(8 * D_MODEL / 3 / 256)

A_HEADS = D_MODEL // 128
A_HALF = 64
A_VDIM = 128
A_KV_HEADS = 4
A_Q = A_HEADS * 2 * A_HALF
A_K = A_KV_HEADS * 2 * A_HALF

B_HEADS = D_MODEL // 128
B_HD = 128
B_KV_HEADS = 4
B_KV = B_KV_HEADS * B_HD
B_Q = B_HEADS * B_HD
CMP_STRIDE = 16
CMP_LEN = 32
SEL_BLOCK = 64
SEL_N = 16
WINDOW = 512

C_HEADS = D_MODEL // 128
C_HD = 128
C_KV_HEADS = 4
C_Q = C_HEADS * C_HD
C_KV = C_KV_HEADS * C_HD
IDX_HEADS = 16
IDX_DIM = 64
IDX_TOPK_MAX = 256

MEM_LEN = 256
X_HEADS = 4
X_HD = 128
X_W = X_HEADS * X_HD

GROUPS = 4
HEADS_PER_GROUP = 4
HD = 128
GW = HEADS_PER_GROUP * HD

LANES = 128
SUBLANES = 8
VMEM_LIMIT_BYTES = 56 * 1024 * 1024
M_INIT = -1e29
INT_MIN = -2147483648


def _cparams(*sem):
    return pltpu.CompilerParams(dimension_semantics=sem, vmem_limit_bytes=VMEM_LIMIT_BYTES)


def _pick(n, pref):
    if n <= pref:
        return n
    t = pref
    while n % t:
        t //= 2
    return t


def _rms_kernel(x_ref, g_ref, o_ref):
    x = x_ref[...]
    ms = jnp.mean(x * x, axis=-1, keepdims=True)
    o_ref[...] = (x * lax.rsqrt(ms + EPS) * g_ref[...]).astype(o_ref.dtype)


def rmsnorm(x, g, out_dtype):
    t, d = x.shape
    tm = _pick(t, 512)
    return pl.pallas_call(
        _rms_kernel,
        grid=(t // tm,),
        in_specs=[pl.BlockSpec((tm, d), lambda i: (i, 0)), pl.BlockSpec((1, d), lambda i: (0, 0))],
        out_specs=pl.BlockSpec((tm, d), lambda i: (i, 0)),
        out_shape=jax.ShapeDtypeStruct((t, d), out_dtype),
        compiler_params=_cparams("parallel"),
        name="rmsnorm",
    )(x, g.reshape(1, d))


def _mm_kernel(a_ref, w_ref, o_ref, wb_ref):
    @pl.when(pl.program_id(1) == 0)
    def _():
        wb_ref[...] = w_ref[...].astype(BF16)

    o_ref[...] = jnp.dot(a_ref[...].astype(BF16), wb_ref[...], preferred_element_type=F32).astype(o_ref.dtype)


def _mm_res_kernel(a_ref, w_ref, r_ref, o_ref, wb_ref, *, scale):
    @pl.when(pl.program_id(1) == 0)
    def _():
        wb_ref[...] = w_ref[...].astype(BF16)

    acc = jnp.dot(a_ref[...].astype(BF16), wb_ref[...], preferred_element_type=F32)
    o_ref[...] = r_ref[...] + scale * acc


def _mm_swiglu_kernel(a_ref, wg_ref, wu_ref, o_ref, wgb_ref, wub_ref):
    @pl.when(pl.program_id(1) == 0)
    def _():
        wgb_ref[...] = wg_ref[...].astype(BF16)
        wub_ref[...] = wu_ref[...].astype(BF16)

    a = a_ref[...].astype(BF16)
    g = jnp.dot(a, wgb_ref[...], preferred_element_type=F32)
    u = jnp.dot(a, wub_ref[...], preferred_element_type=F32)
    o_ref[...] = (g * jax.nn.sigmoid(g) * u).astype(o_ref.dtype)


def _w_spec(w, layer, k, tn, c0):
    if w.ndim == 3:
        return pl.BlockSpec((None, k, tn), lambda j, i: (layer, 0, j + c0))
    return pl.BlockSpec((k, tn), lambda j, i: (0, j + c0))


def linear(a, w, layer=0, *, col0=0, ncols=None, out_dtype=F32, tm=1024, tn=512, res=None, scale=1.0):
    m, k = a.shape
    ncols = w.shape[-1] - col0 if ncols is None else ncols
    tm = _pick(m, tm)
    tn = _pick(ncols, tn)
    assert col0 % tn == 0 and ncols % tn == 0 and m % tm == 0
    c0 = col0 // tn
    grid = (ncols // tn, m // tm)
    a_spec = pl.BlockSpec((tm, k), lambda j, i: (i, 0))
    w_spec = _w_spec(w, layer, k, tn, c0)
    o_spec = pl.BlockSpec((tm, tn), lambda j, i: (i, j))
    scratch = [pltpu.VMEM((k, tn), BF16)]
    if res is None:
        return pl.pallas_call(
            _mm_kernel, grid=grid, in_specs=[a_spec, w_spec], out_specs=o_spec,
            out_shape=jax.ShapeDtypeStruct((m, ncols), out_dtype), scratch_shapes=scratch,
            compiler_params=_cparams("parallel", "arbitrary"), name="linear",
        )(a, w)
    return pl.pallas_call(
        functools.partial(_mm_res_kernel, scale=scale), grid=grid,
        in_specs=[a_spec, w_spec, o_spec], out_specs=o_spec,
        out_shape=jax.ShapeDtypeStruct((m, ncols), F32), scratch_shapes=scratch,
        compiler_params=_cparams("parallel", "arbitrary"), name="linear_res",
    )(a, w, res)


def swiglu_in(a, w_in, layer):
    m, k = a.shape
    tm = _pick(m, 1024)
    tn = 512
    nj = D_FF // tn
    return pl.pallas_call(
        _mm_swiglu_kernel, grid=(nj, m // tm),
        in_specs=[pl.BlockSpec((tm, k), lambda j, i: (i, 0)),
                  _w_spec(w_in, layer, k, tn, 0),
                  _w_spec(w_in, layer, k, tn, nj)],
        out_specs=pl.BlockSpec((tm, tn), lambda j, i: (i, j)),
        out_shape=jax.ShapeDtypeStruct((m, D_FF), BF16),
        scratch_shapes=[pltpu.VMEM((k, tn), BF16), pltpu.VMEM((k, tn), BF16)],
        compiler_params=_cparams("parallel", "arbitrary"), name="swiglu_in",
    )(a, w_in, w_in)


def ffn(x, g, w_in, w_out, layer):
    h = rmsnorm(x, g[layer], BF16)
    hid = swiglu_in(h, w_in, layer)
    return linear(hid, w_out, layer, tm=512, tn=512, res=x, scale=0.5)


def rope_tables(pos, dh):
    rot = dh // 4
    half = rot // 2
    inv = ROPE_THETA ** (-jnp.arange(half, dtype=F32) / half)
    ang = pos.astype(F32)[:, None] * inv[None, :]
    cos, sin = jnp.cos(ang), jnp.sin(ang)
    p = pos.shape[0]
    rest1 = jnp.ones((p, dh - rot), F32)
    rest0 = jnp.zeros((p, dh - rot), F32)
    z = jnp.zeros((p, half), F32)
    c = jnp.concatenate([cos, cos, rest1], axis=1)
    up = jnp.concatenate([-sin, z, rest0], axis=1)
    dn = jnp.concatenate([z, sin, rest0], axis=1)
    rep = LANES // dh
    return tuple(jnp.tile(t, (1, rep)) for t in (c, up, dn)), half


def _rope_chunk(xc, c, up, dn, half):
    return xc * c + pltpu.roll(xc, LANES - half, 1) * up + pltpu.roll(xc, half, 1) * dn


def _cols_kernel(*refs, half, width):
    if half:
        x_ref, c_ref, u_ref, d_ref = refs[:4]
        outs = refs[4:]
        c, up, dn = c_ref[...], u_ref[...], d_ref[...]
        x = x_ref[...]
        y = jnp.concatenate(
            [_rope_chunk(x[:, k * LANES:(k + 1) * LANES], c, up, dn, half) for k in range(width // LANES)], axis=1)
    else:
        x_ref = refs[0]
        outs = refs[1:]
        y = x_ref[...]
    for o in outs:
        o[...] = y.astype(o.dtype)


def take_cols(x, col0, width, out_dtypes, tabs=None, n_pos_blocks=1):
    t = x.shape[0]
    bw = 512 if width % 512 == 0 else width
    assert col0 % bw == 0
    tables, half = tabs if tabs is not None else ((), 0)
    tm = _pick(t, 512) if tabs is None else _pick(tables[0].shape[0], 512)
    assert t % tm == 0
    npb = (tables[0].shape[0] // tm) if tabs is not None else 1
    c0 = col0 // bw
    in_specs = [pl.BlockSpec((tm, bw), lambda i, j: (i, j + c0))]
    in_specs += [pl.BlockSpec((tm, LANES), lambda i, j: (i % npb, 0)) for _ in tables]
    outs = pl.pallas_call(
        functools.partial(_cols_kernel, half=half, width=bw),
        grid=(t // tm, width // bw),
        in_specs=in_specs,
        out_specs=[pl.BlockSpec((tm, bw), lambda i, j: (i, j)) for _ in out_dtypes],
        out_shape=[jax.ShapeDtypeStruct((t, width), dt) for dt in out_dtypes],
        compiler_params=_cparams("parallel", "parallel"), name="take_cols",
    )(x, *tables)
    return outs


def _stack_heads(q, sel=None):
    parts = []
    for r in range(HEADS_PER_GROUP):
        qr = q[:, r * HD:(r + 1) * HD]
        if sel is not None:
            qr = jnp.where(sel, qr, jnp.zeros((), q.dtype))
        parts.append(qr)
    return _concat_rows(parts)


def _concat_rows(parts):
    if parts[0].shape[0] % (2 * SUBLANES):
        return jnp.concatenate([p.astype(F32) for p in parts], axis=0).astype(parts[0].dtype)
    return jnp.concatenate(parts, axis=0)


def _unstack_heads(o, tq):
    return jnp.concatenate([o[r * tq:(r + 1) * tq] for r in range(HEADS_PER_GROUP)], axis=1)


def _qk(q, k):
    return lax.dot_general(q, k, (((1,), (1,)), ((), ())), preferred_element_type=F32)


def _mask_rows(mask, s, tq):
    tk = s.shape[-1]
    return jnp.where(mask[None], s.reshape(HEADS_PER_GROUP, tq, tk), NEG_INF).reshape(HEADS_PER_GROUP * tq, tk)


def _online_update(slot, s, vt, m_sc, l_sc, acc_sc):
    m_prev = m_sc[slot]
    m_new = jnp.maximum(m_prev, jnp.max(s, axis=-1, keepdims=True))
    alpha = jnp.exp(m_prev - m_new)
    p = jnp.exp(s - m_new)
    l_sc[slot] = alpha * l_sc[slot] + jnp.sum(p, axis=-1, keepdims=True)
    acc_sc[slot] = alpha * acc_sc[slot] + jnp.dot(p.astype(BF16), vt, preferred_element_type=F32)
    m_sc[slot] = m_new


def _init_state(m_sc, l_sc, acc_sc):
    m_sc[...] = jnp.full(m_sc.shape, M_INIT, F32)
    l_sc[...] = jnp.zeros(l_sc.shape, F32)
    acc_sc[...] = jnp.zeros(acc_sc.shape, F32)


def _order_key(x):
    b = lax.bitcast_convert_type(x, I32)
    return jnp.where(b < 0, b ^ jnp.int32(0x7FFFFFFF), b)


def _kth_largest_key(key, k):
    kf = jnp.float32(k)

    def count_ge(t):
        return jnp.sum(jnp.where(key >= t, 1.0, 0.0), axis=-1, keepdims=True)

    t0 = jnp.where(count_ge(jnp.int32(0)) >= kf, jnp.int32(0), jnp.int32(INT_MIN))

    def body(it, t):
        cand = t | (jnp.int32(1) << (jnp.int32(30) - it))
        return jnp.where(count_ge(cand) >= kf, cand, t)

    return lax.fori_loop(0, 31, body, t0)


def _n_kv_tiles(qpos0, i, tq, tk, lp):
    hi = qpos0 + (i + 1) * tq
    return jnp.minimum((hi + tk - 1) // tk, lp // tk)


def _diff_attn_kernel(lam_ref, q_ref, k_ref, v_ref, g_ref, o_ref, m_sc, l_sc, acc_sc, *, tq, tk, qpos0, lp, post):
    i = pl.program_id(2)
    q = q_ref[...]
    lane = lax.broadcasted_iota(I32, (1, HD), 1)
    lo = lane < A_HALF
    qs = (_stack_heads(q, lo), _stack_heads(q, jnp.logical_not(lo)))
    qpos = qpos0 + i * tq + lax.broadcasted_iota(I32, (tq, 1), 0)
    _init_state(m_sc, l_sc, acc_sc)
    scale = A_HALF ** -0.5

    def body(j, carry):
        off = pl.multiple_of(j * tk, tk)
        kt = k_ref[pl.ds(off, tk), :]
        vt = v_ref[pl.ds(off, tk), :]
        kpos = off + lax.broadcasted_iota(I32, (1, tk), 1)
        mask = kpos <= qpos
        for c in range(2):
            s = _mask_rows(mask, _qk(qs[c], kt) * scale, tq)
            _online_update(c, s, vt, m_sc, l_sc, acc_sc)
        return carry

    lax.fori_loop(0, _n_kv_tiles(qpos0, i, tq, tk, lp), body, 0)
    o = acc_sc[0] / l_sc[0] - lam_ref[0] * (acc_sc[1] / l_sc[1])
    ms = jnp.mean(o * o, axis=-1, keepdims=True)
    o = o * lax.rsqrt(ms + EPS) * g_ref[...] * post
    o_ref[...] = _unstack_heads(o, tq).astype(o_ref.dtype)


def diff_attention(q, k, v, lam_f, subln, lam_init, qpos0):
    b, tqn, _ = q.shape
    lp = k.shape[1]
    tq = _pick(tqn, 256)
    tk = _pick(lp, 512)
    rows = HEADS_PER_GROUP * tq
    kern = functools.partial(_diff_attn_kernel, tq=tq, tk=tk, qpos0=qpos0, lp=lp, post=1.0 - lam_init)
    return pl.pallas_call(
        kern, grid=(b, GROUPS, tqn // tq),
        in_specs=[pl.BlockSpec(memory_space=pltpu.SMEM),
                  pl.BlockSpec((None, tq, GW), lambda bi, g, i: (bi, i, g)),
                  pl.BlockSpec((None, lp, HD), lambda bi, g, i: (bi, 0, g)),
                  pl.BlockSpec((None, lp, HD), lambda bi, g, i: (bi, 0, g)),
                  pl.BlockSpec((1, HD), lambda bi, g, i: (0, 0))],
        out_specs=pl.BlockSpec((None, tq, GW), lambda bi, g, i: (bi, i, g)),
        out_shape=jax.ShapeDtypeStruct(q.shape, BF16),
        scratch_shapes=[pltpu.VMEM((2, rows, 1), F32), pltpu.VMEM((2, rows, 1), F32), pltpu.VMEM((2, rows, HD), F32)],
        compiler_params=_cparams("parallel", "parallel", "parallel"), name="diff_attention",
    )(lam_f.reshape(1).astype(F32), q, k, v, subln.reshape(1, HD))


def _nsa_cmp_kernel(q_ref, fsk_ref, fsv_ref, b_ref, o_ref, sel_ref, *, tq, w, nblk, nch, qpos0, klanes):
    i = pl.program_id(2)
    qst = _stack_heads(q_ref[...])
    fsk = fsk_ref[...]
    fsv = fsv_ref[...]
    kcb = fsk[:, :HD] + pltpu.roll(fsk[:, HD:], w - 1, 0) + b_ref[0:1, :]
    vcb = fsv[:, :HD] + pltpu.roll(fsv[:, HD:], w - 1, 0) + b_ref[1:2, :]
    qpos = qpos0 + i * tq + lax.broadcasted_iota(I32, (tq, 1), 0)
    lanei = lax.broadcasted_iota(I32, (1, w), 1)
    s = _qk(qst, kcb.astype(BF16)) * (B_HD ** -0.5)
    valid = (lanei * CMP_STRIDE + (CMP_LEN - 1) <= qpos) & (lanei < nblk)
    s3 = jnp.where(valid[None], s.reshape(HEADS_PER_GROUP, tq, w), NEG_INF)
    e = jnp.exp(s3 - jnp.max(s3, axis=-1, keepdims=True))
    p = e / jnp.sum(e, axis=-1, keepdims=True)
    any_valid = (qpos >= CMP_LEN - 1) & (nblk > 0)
    p = jnp.where(any_valid[None], p, 0.0)
    o = jnp.dot(p.reshape(HEADS_PER_GROUP * tq, w).astype(BF16), vcb.astype(BF16), preferred_element_type=F32)
    o_ref[...] = _unstack_heads(o, tq).astype(o_ref.dtype)

    grp = p[0] + p[1] + p[2] + p[3]
    chunk = grp + jnp.where(lanei == 0, 0.0, pltpu.roll(grp, 1, 1))
    pair = jnp.where((lanei & 1) == 0, chunk + pltpu.roll(chunk, w - 1, 1), chunk + pltpu.roll(chunk, 1, 1))
    quad = jnp.where((lanei & 2) == 0, pair + pltpu.roll(pair, w - 2, 1), pair + pltpu.roll(pair, 2, 1))
    jb = lanei >> 2
    cur = qpos >> 6
    forced = (jb == 0) | (jb == cur) | (jb == cur - 1)
    score = jnp.where(forced, FORCE, jnp.where(jb * SEL_BLOCK <= qpos, quad, NEG_INF))
    score = jnp.where(lanei < nch, score, NEG_INF)
    key = _order_key(score)
    thr = _kth_largest_key(key, klanes)
    sel_ref[...] = jnp.where((key >= thr) | (lanei >= nch), 1.0, 0.0).astype(sel_ref.dtype)


def nsa_compressed(q_raw, fsk, fsv, bias, *, nblk, nch, qpos0, n_top_rep):
    b, tqn, _ = q_raw.shape
    w = fsk.shape[2]
    tq = _pick(tqn, 256)
    kern = functools.partial(_nsa_cmp_kernel, tq=tq, w=w, nblk=nblk, nch=nch, qpos0=qpos0,
                             klanes=(SEL_BLOCK // CMP_STRIDE) * n_top_rep)
    return pl.pallas_call(
        kern, grid=(b, GROUPS, tqn // tq),
        in_specs=[pl.BlockSpec((None, tq, GW), lambda bi, g, i: (bi, i, g)),
                  pl.BlockSpec((None, None, w, 2 * HD), lambda bi, g, i: (bi, g, 0, 0)),
                  pl.BlockSpec((None, None, w, 2 * HD), lambda bi, g, i: (bi, g, 0, 0)),
                  pl.BlockSpec((2, HD), lambda bi, g, i: (0, 0))],
        out_specs=[pl.BlockSpec((None, tq, GW), lambda bi, g, i: (bi, i, g)),
                   pl.BlockSpec((None, None, tq, w), lambda bi, g, i: (bi, g, i, 0))],
        out_shape=[jax.ShapeDtypeStruct(q_raw.shape, F32), jax.ShapeDtypeStruct((b, GROUPS, tqn, w), BF16)],
        compiler_params=_cparams("parallel", "parallel", "parallel"), name="nsa_compressed",
    )(q_raw, fsk, fsv, bias)


def _nsa_sw_kernel(q_ref, ks_ref, vs_ref, kw_ref, vw_ref, sel_ref, oc_ref, gate_ref, o_ref, m_sc, l_sc, acc_sc,
                   *, tq, tk, tkw, qpos0, lp, lw, kwpos0, w):
    i = pl.program_id(2)
    qst = _stack_heads(q_ref[...])
    qpos = qpos0 + i * tq + lax.broadcasted_iota(I32, (tq, 1), 0)
    _init_state(m_sc, l_sc, acc_sc)
    scale = B_HD ** -0.5
    selm = sel_ref[...]
    chunk_row = lax.broadcasted_iota(I32, (w, 1), 0)

    def sel_body(j, carry):
        off = pl.multiple_of(j * tk, tk)
        kt = ks_ref[pl.ds(off, tk), :]
        vt = vs_ref[pl.ds(off, tk), :]
        kpos = off + lax.broadcasted_iota(I32, (1, tk), 1)
        expand = jnp.where(chunk_row == (kpos >> 4), 1.0, 0.0).astype(BF16)
        chosen = jnp.dot(selm, expand, preferred_element_type=F32) > 0.5
        mask = chosen & (kpos <= qpos)
        _online_update(0, _mask_rows(mask, _qk(qst, kt) * scale, tq), vt, m_sc, l_sc, acc_sc)
        return carry

    lax.fori_loop(0, _n_kv_tiles(qpos0, i, tq, tk, lp), sel_body, 0)

    def win_body(j, carry):
        off = pl.multiple_of(j * tkw, tkw)
        kt = kw_ref[pl.ds(off, tkw), :]
        vt = vw_ref[pl.ds(off, tkw), :]
        kpos = kwpos0 + off + lax.broadcasted_iota(I32, (1, tkw), 1)
        dt = qpos - kpos
        mask = (dt >= 0) & (dt <= WINDOW)
        _online_update(1, _mask_rows(mask, _qk(qst, kt) * scale, tq), vt, m_sc, l_sc, acc_sc)
        return carry

    first = jnp.maximum(qpos0 + i * tq - WINDOW - kwpos0, 0) // tkw
    last = jnp.minimum((qpos0 + (i + 1) * tq - kwpos0 + tkw - 1) // tkw, lw // tkw)
    lax.fori_loop(first, last, win_body, 0)

    o_sel = _unstack_heads(acc_sc[0] / l_sc[0], tq)
    o_win = _unstack_heads(acc_sc[1] / l_sc[1], tq)
    gsig = jax.nn.sigmoid(gate_ref[...])
    oc = oc_ref[...]
    outs = []
    for r in range(HEADS_PER_GROUP):
        cs = slice(r * HD, (r + 1) * HD)
        outs.append(oc[:, cs] * gsig[:, 3 * r:3 * r + 1] + o_sel[:, cs] * gsig[:, 3 * r + 1:3 * r + 2]
                    + o_win[:, cs] * gsig[:, 3 * r + 2:3 * r + 3])
    o_ref[...] = jnp.concatenate(outs, axis=1).astype(o_ref.dtype)


def nsa_select_window(q_rot, ks, vs, kw, vw, selmask, o_cmp, gates, *, qpos0, kwpos0):
    b, tqn, _ = q_rot.shape
    lp, lw, w = ks.shape[1], kw.shape[1], selmask.shape[3]
    tq = _pick(tqn, 256)
    tk = _pick(lp, 512)
    tkw = _pick(lw, 512)
    rows = HEADS_PER_GROUP * tq
    kern = functools.partial(_nsa_sw_kernel, tq=tq, tk=tk, tkw=tkw, qpos0=qpos0, lp=lp, lw=lw, kwpos0=kwpos0, w=w)
    qspec = pl.BlockSpec((None, tq, GW), lambda bi, g, i: (bi, i, g))
    kspec = pl.BlockSpec((None, lp, HD), lambda bi, g, i: (bi, 0, g))
    wspec = pl.BlockSpec((None, lw, HD), lambda bi, g, i: (bi, 0, g))
    return pl.pallas_call(
        kern, grid=(b, GROUPS, tqn // tq),
        in_specs=[qspec, kspec, kspec, wspec, wspec,
                  pl.BlockSpec((None, None, tq, w), lambda bi, g, i: (bi, g, i, 0)),
                  qspec,
                  pl.BlockSpec((None, None, tq, LANES), lambda bi, g, i: (bi, g, i, 0))],
        out_specs=qspec,
        out_shape=jax.ShapeDtypeStruct(q_rot.shape, BF16),
        scratch_shapes=[pltpu.VMEM((2, rows, 1), F32), pltpu.VMEM((2, rows, 1), F32), pltpu.VMEM((2, rows, HD), F32)],
        compiler_params=_cparams("parallel", "parallel", "parallel"), name="nsa_select_window",
    )(q_rot, ks, vs, kw, vw, selmask, o_cmp, gates)


def _ik_kernel(t_ref, g_ref, c_ref, u_ref, d_ref, ikf_ref, ik2_ref, *, half):
    x = t_ref[...]
    lane = lax.broadcasted_iota(I32, (1, LANES), 1)
    xm = jnp.where(lane < IDX_DIM, x, 0.0)
    ms = jnp.sum(xm * xm, axis=-1, keepdims=True) * (1.0 / IDX_DIM)
    y = xm * lax.rsqrt(ms + EPS) * g_ref[...]
    y = _rope_chunk(y, c_ref[...], u_ref[...], d_ref[...], half)
    ikf_ref[...] = y
    ik2_ref[...] = (y + pltpu.roll(y, IDX_DIM, 1)).astype(ik2_ref.dtype)


def index_keys(tail, knorm, tabs):
    t = tail.shape[0]
    tables, half = tabs
    tm = _pick(tables[0].shape[0], 512)
    npb = tables[0].shape[0] // tm
    g = jnp.concatenate([knorm.astype(F32), jnp.zeros((LANES - IDX_DIM,), F32)]).reshape(1, LANES)
    rspec = pl.BlockSpec((tm, LANES), lambda i: (i, 0))
    tspec = pl.BlockSpec((tm, LANES), lambda i: (i % npb, 0))
    return pl.pallas_call(
        functools.partial(_ik_kernel, half=half), grid=(t // tm,),
        in_specs=[rspec, pl.BlockSpec((1, LANES), lambda i: (0, 0)), tspec, tspec, tspec],
        out_specs=[rspec, rspec],
        out_shape=[jax.ShapeDtypeStruct((t, LANES), F32), jax.ShapeDtypeStruct((t, LANES), BF16)],
        compiler_params=_cparams("parallel"), name="index_keys",
    )(tail, g, *tables)


def _dsa_index_kernel(iq_ref, ik_ref, tail_ref, mask_ref, sc_sc, *, tq, tk, qpos0, lp, top):
    i = pl.program_id(1)
    iq = iq_ref[...]
    lane = lax.broadcasted_iota(I32, (1, LANES), 1)
    lo = lane < IDX_DIM
    hi = jnp.logical_not(lo)
    zero = jnp.zeros((), iq.dtype)
    parts = []
    for h in range(IDX_HEADS):
        pair = iq[:, (h // 2) * LANES:(h // 2 + 1) * LANES]
        parts.append(jnp.where(lo if h % 2 == 0 else hi, pair, zero))
    iqst = _concat_rows(parts)
    iw = tail_ref[:, IDX_DIM:IDX_DIM + IDX_HEADS] * ((IDX_DIM ** -0.5) * (IDX_HEADS ** -0.5))
    qpos = qpos0 + i * tq + lax.broadcasted_iota(I32, (tq, 1), 0)
    sc_sc[...] = jnp.full(sc_sc.shape, NEG_INF, F32)

    def body(j, carry):
        off = pl.multiple_of(j * tk, tk)
        sc = _qk(iqst, ik_ref[pl.ds(off, tk), :])
        acc = jnp.zeros((tq, tk), F32)
        for h in range(IDX_HEADS):
            acc = acc + jnp.maximum(sc[h * tq:(h + 1) * tq], 0.0) * iw[:, h:h + 1]
        kpos = off + lax.broadcasted_iota(I32, (1, tk), 1)
        sc_sc[:, pl.ds(off, tk)] = jnp.where(kpos <= qpos, acc, NEG_INF)
        return carry

    lax.fori_loop(0, _n_kv_tiles(qpos0, i, tq, tk, lp), body, 0)
    key = _order_key(sc_sc[...])
    thr = _kth_largest_key(key, top)
    kpos_all = lax.broadcasted_iota(I32, (1, lp), 1)
    mask_ref[...] = jnp.where((key >= thr) & (kpos_all <= qpos), 1.0, 0.0).astype(mask_ref.dtype)


def dsa_select(iq, ik2, tail, *, qpos0, top):
    b, tqn, _ = iq.shape
    lp = ik2.shape[1]
    tq = _pick(tqn, 128)
    tk = _pick(lp, 512)
    kern = functools.partial(_dsa_index_kernel, tq=tq, tk=tk, qpos0=qpos0, lp=lp, top=top)
    return pl.pallas_call(
        kern, grid=(b, tqn // tq),
        in_specs=[pl.BlockSpec((None, tq, IDX_HEADS * IDX_DIM), lambda bi, i: (bi, i, 0)),
                  pl.BlockSpec((None, lp, LANES), lambda bi, i: (bi, 0, 0)),
                  pl.BlockSpec((None, tq, LANES), lambda bi, i: (bi, i, 0))],
        out_specs=pl.BlockSpec((None, tq, lp), lambda bi, i: (bi, i, 0)),
        out_shape=jax.ShapeDtypeStruct((b, tqn, lp), BF16),
        scratch_shapes=[pltpu.VMEM((tq, lp), F32)],
        compiler_params=_cparams("parallel", "parallel"), name="dsa_select",
    )(iq, ik2, tail)


def _masked_attn_kernel(q_ref, k_ref, v_ref, mask_ref, o_ref, m_sc, l_sc, acc_sc, *, tq, tk, qpos0, lp):
    i = pl.program_id(1)
    qst = _stack_heads(q_ref[...])
    _init_state(m_sc, l_sc, acc_sc)
    scale = C_HD ** -0.5

    def body(j, carry):
        off = pl.multiple_of(j * tk, tk)
        kt = k_ref[pl.ds(off, tk), :]
        vt = v_ref[pl.ds(off, tk), :]
        mask = mask_ref[:, pl.ds(off, tk)] > 0.5
        _online_update(0, _mask_rows(mask, _qk(qst, kt) * scale, tq), vt, m_sc, l_sc, acc_sc)
        return carry

    lax.fori_loop(0, _n_kv_tiles(qpos0, i, tq, tk, lp), body, 0)
    o_ref[...] = _unstack_heads(acc_sc[0] / l_sc[0], tq).astype(o_ref.dtype)


def masked_attention(q, k, v, mask, *, qpos0):
    b, tqn, _ = q.shape
    lp = k.shape[1]
    tq = _pick(tqn, 128)
    tk = _pick(lp, 512)
    rows = HEADS_PER_GROUP * tq
    kern = functools.partial(_masked_attn_kernel, tq=tq, tk=tk, qpos0=qpos0, lp=lp)
    qspec = pl.BlockSpec((None, tq, GW), lambda bi, i, g: (bi, i, g))
    kspec = pl.BlockSpec((None, lp, HD), lambda bi, i, g: (bi, 0, g))
    return pl.pallas_call(
        kern, grid=(b, tqn // tq, GROUPS),
        in_specs=[qspec, kspec, kspec, pl.BlockSpec((None, tq, lp), lambda bi, i, g: (bi, i, 0))],
        out_specs=qspec,
        out_shape=jax.ShapeDtypeStruct(q.shape, BF16),
        scratch_shapes=[pltpu.VMEM((1, rows, 1), F32), pltpu.VMEM((1, rows, 1), F32), pltpu.VMEM((1, rows, HD), F32)],
        compiler_params=_cparams("parallel", "parallel", "parallel"), name="masked_attention",
    )(q, k, v, mask)


def _cross_kernel(q_ref, k_ref, v_ref, o_ref):
    q = q_ref[...]
    k = k_ref[...].astype(BF16)
    v = v_ref[...].astype(BF16)
    outs = []
    for h in range(X_HEADS):
        cs = slice(h * X_HD, (h + 1) * X_HD)
        s = _qk(q[:, cs], k[:, cs]) * (X_HD ** -0.5)
        e = jnp.exp(s - jnp.max(s, axis=-1, keepdims=True))
        p = e / jnp.sum(e, axis=-1, keepdims=True)
        outs.append(jnp.dot(p.astype(BF16), v[:, cs], preferred_element_type=F32))
    o_ref[...] = jnp.concatenate(outs, axis=1).astype(o_ref.dtype)


def cross_attention(q, mk, mv):
    b, tqn, _ = q.shape
    tq = _pick(tqn, 512)
    ml = mk.shape[1]
    qspec = pl.BlockSpec((None, tq, X_W), lambda bi, i: (bi, i, 0))
    mspec = pl.BlockSpec((None, ml, X_W), lambda bi, i: (bi, 0, 0))
    return pl.pallas_call(
        _cross_kernel, grid=(b, tqn // tq), in_specs=[qspec, mspec, mspec], out_specs=qspec,
        out_shape=jax.ShapeDtypeStruct(q.shape, BF16),
        compiler_params=_cparams("parallel", "parallel"), name="cross_attention",
    )(q, mk, mv)


def _gather_kernel(pt_ref, pool_ref, new_ref, o_ref, *, n_pages, dup):
    p = pl.program_id(1)

    def emit(x):
        x = x.astype(o_ref.dtype)
        o_ref[...] = jnp.concatenate([x, x], axis=1) if dup else x

    @pl.when(p < n_pages)
    def _():
        emit(pool_ref[...])

    @pl.when(p == n_pages)
    def _():
        emit(new_ref[...])

    @pl.when(p > n_pages)
    def _():
        o_ref[...] = jnp.zeros(o_ref.shape, o_ref.dtype)


def gather_pages(pool, layer, page_table, new_rows, lp, dup=False):
    b, n_pages = page_table.shape
    c = pool.shape[-1]
    co = 2 * c if dup else c
    kern = functools.partial(_gather_kernel, n_pages=n_pages, dup=dup)
    grid_spec = pltpu.PrefetchScalarGridSpec(
        num_scalar_prefetch=1, grid=(b, lp // PAGE_SIZE),
        in_specs=[pl.BlockSpec((None, None, PAGE_SIZE, c),
                               lambda bi, p, pt: (layer, pt[bi, jnp.minimum(p, n_pages - 1)], 0, 0)),
                  pl.BlockSpec((None, PAGE_SIZE, c), lambda bi, p, pt: (bi, 0, 0))],
        out_specs=pl.BlockSpec((None, PAGE_SIZE, co), lambda bi, p, pt: (bi, p, 0)))
    return pl.pallas_call(
        kern, grid_spec=grid_spec, out_shape=jax.ShapeDtypeStruct((b, lp, co), BF16),
        compiler_params=_cparams("parallel", "arbitrary"), name="gather_pages",
    )(page_table, pool, new_rows)


def lambda_init(layer):
    return 0.8 - 0.6 * math.exp(-0.3 * layer)


def _pad_rows(x, n):
    return jnp.pad(x, ((0, 0), (0, n - x.shape[1]), (0, 0)))


def trunk(x, nb, tq_real, qpos0, mem_k, mem_v, W, past):
    t = nb * tq_real
    tqp = max(tq_real, SUBLANES)
    pos = qpos0 + jnp.arange(tq_real, dtype=I32)
    pos_rows = jnp.tile(pos, nb) if tq_real < SUBLANES else pos
    tab64 = rope_tables(pos_rows, 64)
    tab128 = rope_tables(pos_rows, 128)
    new = {}

    def to_attn(a):
        return _pad_rows(a.reshape(nb, tq_real, a.shape[-1]), tqp)

    def from_attn(a):
        return a[:, :tq_real].reshape(t, a.shape[-1])

    if past is not None:
        pt = past['page_table']
        past_len = pt.shape[1] * PAGE_SIZE
        lp = past_len + 512

        def with_past(pool, layer, new_bf, dup=False):
            return gather_pages(pool, layer, pt, _pad_rows(new_bf.reshape(nb, tq_real, -1), PAGE_SIZE), lp, dup)
    else:
        past_len = 0

    def flat_pool(name):
        p = past[name]
        return p.reshape(p.shape[0], p.shape[1], PAGE_SIZE, -1)

    for l in range(DEPTH):
        x = ffn(x, W['norm_ffn1'], W['ffn1_w_in'], W['ffn1_w_out'], l)
        h = rmsnorm(x, W['norm_mix'][l], BF16)
        i = l // N_MIXERS
        kind = l % N_MIXERS
        if kind == 0:
            proj = linear(h, W['a_w_in'], i)
            (q_rot,) = take_cols(proj, 0, A_Q, (BF16,), tab64)
            k_f, k_b = take_cols(proj, A_Q, A_K, (F32, BF16), tab64)
            (v_b,) = take_cols(proj, A_Q + A_K, A_KV_HEADS * A_VDIM, (BF16,))
            v_f = proj[:, A_Q + A_K:]
            new.setdefault('a_k', []).append(k_f.reshape(nb, tq_real, A_KV_HEADS, 2 * A_HALF))
            new.setdefault('a_v', []).append(v_f.reshape(nb, tq_real, A_KV_HEADS, A_VDIM))
            if past is None:
                k_all, v_all = k_b.reshape(nb, tq_real, -1), v_b.reshape(nb, tq_real, -1)
            else:
                k_all = with_past(flat_pool('a_k'), i, k_b)
                v_all = with_past(flat_pool('a_v'), i, v_b)
            lam = W['a_lambda'][i]
            lam_f = (jnp.exp(jnp.sum(lam[0] * lam[1])) - jnp.exp(jnp.sum(lam[2] * lam[3]))).astype(F32) + lambda_init(l)
            o = diff_attention(to_attn(q_rot), k_all, v_all, lam_f, W['a_subln'][i], lambda_init(l), qpos0)
            y_in, w_out = from_attn(o), W['a_w_out']
        elif kind == 1:
            w_in = W['b_w_in']
            n_main = B_Q + 6 * B_KV
            proj = linear(h, w_in, i, ncols=n_main)
            w_tail = jnp.pad(w_in[i, :, n_main:], ((0, 0), (0, LANES - (w_in.shape[-1] - n_main))))
            gate_logits = linear(h, w_tail, tn=LANES)
            (q_raw,) = take_cols(proj, 0, B_Q, (BF16,))
            (q_rot,) = take_cols(proj, 0, B_Q, (BF16,), tab128)
            kc_f = proj[:, B_Q:B_Q + B_KV]
            vc_f = proj[:, B_Q + B_KV:B_Q + 2 * B_KV]
            ks_f, ks_b = take_cols(proj, B_Q + 2 * B_KV, B_KV, (F32, BF16), tab128)
            vs_f = proj[:, B_Q + 3 * B_KV:B_Q + 4 * B_KV]
            (vs_b,) = take_cols(proj, B_Q + 3 * B_KV, B_KV, (BF16,))
            kw_f, kw_b = take_cols(proj, B_Q + 4 * B_KV, B_KV, (F32, BF16), tab128)
            vw_f = proj[:, B_Q + 5 * B_KV:B_Q + 6 * B_KV]
            (vw_b,) = take_cols(proj, B_Q + 5 * B_KV, B_KV, (BF16,))
            shp = (nb, tq_real, B_KV_HEADS, B_HD)
            for nm, a in (('b_cmp_k', kc_f), ('b_cmp_v', vc_f), ('b_sel_k', ks_f), ('b_sel_v', vs_f)):
                new.setdefault(nm, []).append(a.reshape(shp))

            cw = W['b_cmp_w'][i]
            wfs = [jnp.concatenate([cw[s, :CMP_STRIDE].reshape(CMP_STRIDE * B_HD, B_HD),
                                    cw[s, CMP_STRIDE:].reshape(CMP_STRIDE * B_HD, B_HD)], axis=1) for s in range(2)]
            if past is None:
                l_all = tq_real
                rows_k, rows_v = kc_f.reshape(nb, l_all, B_KV), vc_f.reshape(nb, l_all, B_KV)
                ks_all, vs_all = ks_b.reshape(nb, tq_real, -1), vs_b.reshape(nb, tq_real, -1)
                kw_all, vw_all = kw_b.reshape(nb, tq_real, -1), vw_b.reshape(nb, tq_real, -1)
                kwpos0 = 0
                keep = min(WINDOW, tq_real)
                new.setdefault('b_win_k', []).append(kw_f.reshape(shp)[:, tq_real - keep:])
                new.setdefault('b_win_v', []).append(vw_f.reshape(shp)[:, tq_real - keep:])
                lp_sel = tq_real
            else:
                l_all = past_len + tq_real
                zeros_new = jnp.zeros((nb * tq_real, B_KV), BF16)
                rows_k = with_past(flat_pool('b_cmp_k'), i, zeros_new)[:, :past_len]
                rows_v = with_past(flat_pool('b_cmp_v'), i, zeros_new)[:, :past_len]
                ks_all = with_past(flat_pool('b_sel_k'), i, ks_b)
                vs_all = with_past(flat_pool('b_sel_v'), i, vs_b)
                bkw = past['b_win_k'][i].reshape(nb, -1, B_KV)
                bvw = past['b_win_v'][i].reshape(nb, -1, B_KV)
                wb = bkw.shape[1]
                kb = jnp.concatenate([bkw, kw_f.reshape(nb, tq_real, B_KV)], axis=1)
                vb = jnp.concatenate([bvw, vw_f.reshape(nb, tq_real, B_KV)], axis=1)
                new.setdefault('b_win_k', []).append(kb[:, tq_real:].reshape(nb, wb, B_KV_HEADS, B_HD))
                new.setdefault('b_win_v', []).append(vb[:, tq_real:].reshape(nb, wb, B_KV_HEADS, B_HD))
                lw = -(-(wb + tq_real) // 512) * 512
                kw_all = _pad_rows(kb, lw).astype(BF16)
                vw_all = _pad_rows(vb, lw).astype(BF16)
                kwpos0 = qpos0 - wb
                lp_sel = lp
            nch = (l_all // CMP_STRIDE)
            nblk = nch - 1
            n_sel = -(-l_all // SEL_BLOCK)
            n_rep = nch // (SEL_BLOCK // CMP_STRIDE)
            assert nch % (SEL_BLOCK // CMP_STRIDE) == 0 and n_sel - n_rep in (0, 1)
            n_top_rep = min(SEL_N, n_sel) - (n_sel - n_rep)
            wch = -(-(lp_sel // CMP_STRIDE) // LANES) * LANES

            def partials(rows, wf):
                xg = rows[:, :nch * CMP_STRIDE].reshape(nb, nch, CMP_STRIDE, B_KV_HEADS, B_HD)
                xg = xg.transpose(0, 3, 1, 2, 4).reshape(nb * B_KV_HEADS * nch, CMP_STRIDE * B_HD)
                fs = linear(xg, wf, tn=2 * B_HD).reshape(nb, B_KV_HEADS, nch, 2 * B_HD)
                return jnp.pad(fs, ((0, 0), (0, 0), (0, wch - nch), (0, 0)))

            fsk, fsv = partials(rows_k, wfs[0]), partials(rows_v, wfs[1])
            o_cmp, selmask = nsa_compressed(to_attn(q_raw), fsk, fsv, W['b_cmp_b'][i].astype(F32),
                                            nblk=nblk, nch=nch, qpos0=qpos0, n_top_rep=n_top_rep)
            gates = gate_logits[:, :3 * B_HEADS].reshape(nb, tq_real, B_KV_HEADS, 3 * HEADS_PER_GROUP)
            gates = jnp.pad(gates.transpose(0, 2, 1, 3),
                            ((0, 0), (0, 0), (0, tqp - tq_real), (0, LANES - 3 * HEADS_PER_GROUP)))
            o = nsa_select_window(to_attn(q_rot), ks_all, vs_all, kw_all, vw_all, selmask, o_cmp, gates,
                                  qpos0=qpos0, kwpos0=kwpos0)
            y_in, w_out = from_attn(o), W['b_w_out']
        else:
            w_in = W['c_w_in']
            n_main = C_Q + 2 * C_KV + IDX_HEADS * IDX_DIM
            proj = linear(h, w_in, i, ncols=n_main)
            w_tail = jnp.pad(w_in[i, :, n_main:], ((0, 0), (0, LANES - (w_in.shape[-1] - n_main))))
            tail = linear(h, w_tail, tn=LANES)
            (q_rot,) = take_cols(proj, 0, C_Q, (BF16,), tab128)
            k_f, k_b = take_cols(proj, C_Q, C_KV, (F32, BF16), tab128)
            v_f = proj[:, C_Q + C_KV:C_Q + 2 * C_KV]
            (v_b,) = take_cols(proj, C_Q + C_KV, C_KV, (BF16,))
            (iq_rot,) = take_cols(proj, C_Q + 2 * C_KV, IDX_HEADS * IDX_DIM, (BF16,), tab64)
            ik_f, ik2 = index_keys(tail, W['c_idx_knorm'][i], tab64)
            shp = (nb, tq_real, C_KV_HEADS, C_HD)
            new.setdefault('c_k', []).append(k_f.reshape(shp))
            new.setdefault('c_v', []).append(v_f.reshape(shp))
            new.setdefault('c_idx_k', []).append(ik_f[:, :IDX_DIM].reshape(nb, tq_real, IDX_DIM))
            if past is None:
                l_all = tq_real
                k_all, v_all = k_b.reshape(nb, tq_real, -1), v_b.reshape(nb, tq_real, -1)
                ik_all = ik2.reshape(nb, tq_real, LANES)
            else:
                l_all = past_len + tq_real
                k_all = with_past(flat_pool('c_k'), i, k_b)
                v_all = with_past(flat_pool('c_v'), i, v_b)
                ik_all = with_past(flat_pool('c_idx_k'), i, ik2[:, :IDX_DIM], dup=True)
            top = min(IDX_TOPK_MAX, l_all // 4)
            mask = dsa_select(to_attn(iq_rot), ik_all, to_attn(tail), qpos0=qpos0, top=top)
            o = masked_attention(to_attn(q_rot), k_all, v_all, mask, qpos0=qpos0)
            y_in, w_out = from_attn(o), W['c_w_out']
        x = linear(y_in, w_out, i, res=x, scale=1.0)

        h = rmsnorm(x, W['norm_cross'][l], BF16)
        qx = linear(h, W['x_w_q'], l, out_dtype=BF16)
        ox = cross_attention(to_attn(qx), mem_k[l], mem_v[l])
        x = linear(from_attn(ox), W['x_w_o'], l, res=x, scale=1.0)
        x = ffn(x, W['norm_ffn2'], W['ffn2_w_in'], W['ffn2_w_out'], l)
    y = rmsnorm(x, W['final_norm'], F32)
    return y, {nm: jnp.stack(v) for nm, v in new.items()}


def kernel(x_prompt, x_sample, cache_a_k, cache_a_v, cache_b_cmp_k, cache_b_cmp_v, cache_b_sel_k, cache_b_sel_v, state_b_win_k, state_b_win_v, cache_c_k, cache_c_v, cache_c_idx_k, cache_mem_k, cache_mem_v, page_table, mem_prompt, norm_ffn1, norm_mix, norm_cross, norm_ffn2, final_norm, ffn1_w_in, ffn1_w_out, ffn2_w_in, ffn2_w_out, x_w_q, x_w_kv, x_w_o, a_w_in, a_w_out, a_lambda, a_subln, b_w_in, b_w_out, b_cmp_w, b_cmp_b, c_w_in, c_w_out, c_idx_knorm):
    W = dict(norm_ffn1=norm_ffn1, norm_mix=norm_mix, norm_cross=norm_cross, norm_ffn2=norm_ffn2,
             final_norm=final_norm, ffn1_w_in=ffn1_w_in, ffn1_w_out=ffn1_w_out, ffn2_w_in=ffn2_w_in,
             ffn2_w_out=ffn2_w_out, x_w_q=x_w_q, x_w_o=x_w_o, a_w_in=a_w_in, a_w_out=a_w_out,
             a_lambda=a_lambda, a_subln=a_subln, b_w_in=b_w_in, b_w_out=b_w_out, b_cmp_w=b_cmp_w,
             b_cmp_b=b_cmp_b, c_w_in=c_w_in, c_w_out=c_w_out, c_idx_knorm=c_idx_knorm)
    nbp, seq, d = x_prompt.shape
    nbs, dseq, _ = x_sample.shape
    ml = mem_prompt.shape[1]

    mem2d = mem_prompt.reshape(nbp * ml, d)
    mkv = [linear(mem2d, x_w_kv, l) for l in range(DEPTH)]
    p_mem_k = jnp.stack([m[:, :X_W].reshape(nbp, ml, X_HEADS, X_HD) for m in mkv])
    p_mem_v = jnp.stack([m[:, X_W:].reshape(nbp, ml, X_HEADS, X_HD) for m in mkv])
    y_p, ps = trunk(x_prompt.reshape(nbp * seq, d), nbp, seq, 0,
                    p_mem_k.reshape(DEPTH, nbp, ml, X_W), p_mem_v.reshape(DEPTH, nbp, ml, X_W), W, None)

    past = dict(a_k=cache_a_k, a_v=cache_a_v, b_cmp_k=cache_b_cmp_k, b_cmp_v=cache_b_cmp_v,
                b_sel_k=cache_b_sel_k, b_sel_v=cache_b_sel_v, b_win_k=state_b_win_k, b_win_v=state_b_win_v,
                c_k=cache_c_k, c_v=cache_c_v, c_idx_k=cache_c_idx_k, page_table=page_table)
    past_len = page_table.shape[1] * PAGE_SIZE
    sml = cache_mem_k.shape[2]
    y_s, ss = trunk(x_sample.reshape(nbs * dseq, d), nbs, dseq, past_len,
                    cache_mem_k.reshape(DEPTH, nbs, sml, X_W), cache_mem_v.reshape(DEPTH, nbs, sml, X_W), W, past)

    return (y_p.reshape(nbp, seq, d), y_s.reshape(nbs, dseq, d),
            ps['a_k'], ps['a_v'], ps['b_cmp_k'], ps['b_cmp_v'], ps['b_sel_k'], ps['b_sel_v'],
            ps['b_win_k'], ps['b_win_v'], ps['c_k'], ps['c_v'], ps['c_idx_k'], p_mem_k, p_mem_v,
            ss['a_k'], ss['a_v'], ss['b_cmp_k'], ss['b_cmp_v'], ss['b_sel_k'], ss['b_sel_v'],
            ss['b_win_k'], ss['b_win_v'], ss['c_k'], ss['c_v'], ss['c_idx_k'])
```

```python
import functools
import math

import jax
import jax.numpy as jnp
from jax import lax
from jax.experimental import pallas as pl
from jax.experimental.pallas import tpu as pltpu

F32 = jnp.float32
BF16 = jnp.bfloat16
I32 = jnp.int32

D_MODEL = 2048
DEPTH = 4
PAGE_SIZE = 128
N_MIXERS = 3
ROPE_THETA = 500000.0
EPS = 1e-6
NEG_INF = -1e30
FORCE = 1e9
D_FF = 256 * math.ceil(8 * D_MODEL / 3 / 256)

A_HEADS = D_MODEL // 128
A_HALF = 64
A_VDIM = 128
A_KV_HEADS = 4
A_Q = A_HEADS * 2 * A_HALF
A_K = A_KV_HEADS * 2 * A_HALF

B_HEADS = D_MODEL // 128
B_HD = 128
B_KV_HEADS = 4
B_KV = B_KV_HEADS * B_HD
B_Q = B_HEADS * B_HD
CMP_STRIDE = 16
CMP_LEN = 32
SEL_BLOCK = 64
SEL_N = 16
WINDOW = 512

C_HEADS = D_MODEL // 128
C_HD = 128
C_KV_HEADS = 4
C_Q = C_HEADS * C_HD
C_KV = C_KV_HEADS * C_HD
IDX_HEADS = 16
IDX_DIM = 64
IDX_TOPK_MAX = 256

MEM_LEN = 256
X_HEADS = 4
X_HD = 128
X_W = X_HEADS * X_HD

GROUPS = 4
HEADS_PER_GROUP = 4
HD = 128
GW = HEADS_PER_GROUP * HD

LANES = 128
SUBLANES = 8
VMEM_LIMIT_BYTES = 56 * 1024 * 1024
M_INIT = -1e29
LOG2E = 1.4426950408889634
INT_MIN = -2147483648


def _cparams(*sem):
    return pltpu.CompilerParams(dimension_semantics=sem, vmem_limit_bytes=VMEM_LIMIT_BYTES)


def _pick(n, pref):
    if n <= pref:
        return n
    t = pref
    while n % t:
        t //= 2
    return t


def _rms_kernel(x_ref, g_ref, o_ref):
    x = x_ref[...]
    ms = jnp.mean(x * x, axis=-1, keepdims=True)
    o_ref[...] = (x * lax.rsqrt(ms + EPS) * g_ref[...]).astype(o_ref.dtype)


def rmsnorm(x, g, out_dtype):
    t, d = x.shape
    tm = _pick(t, 512)
    return pl.pallas_call(
        _rms_kernel,
        grid=(t // tm,),
        in_specs=[pl.BlockSpec((tm, d), lambda i: (i, 0)), pl.BlockSpec((1, d), lambda i: (0, 0))],
        out_specs=pl.BlockSpec((tm, d), lambda i: (i, 0)),
        out_shape=jax.ShapeDtypeStruct((t, d), out_dtype),
        compiler_params=_cparams("parallel"),
        name="rmsnorm",
    )(x, g.reshape(1, d))


def _mm_kernel(a_ref, w_ref, o_ref, wb_ref):
    @pl.when(pl.program_id(1) == 0)
    def _():
        wb_ref[...] = w_ref[...].astype(BF16)

    o_ref[...] = jnp.dot(a_ref[...].astype(BF16), wb_ref[...], preferred_element_type=F32).astype(o_ref.dtype)


def _mm_res_kernel(a_ref, w_ref, r_ref, o_ref, wb_ref, *, scale):
    @pl.when(pl.program_id(1) == 0)
    def _():
        wb_ref[...] = w_ref[...].astype(BF16)

    acc = jnp.dot(a_ref[...].astype(BF16), wb_ref[...], preferred_element_type=F32)
    o_ref[...] = r_ref[...] + scale * acc


def _mm_swiglu_kernel(a_ref, wg_ref, wu_ref, o_ref, wgb_ref, wub_ref):
    @pl.when(pl.program_id(1) == 0)
    def _():
        wgb_ref[...] = wg_ref[...].astype(BF16)
        wub_ref[...] = wu_ref[...].astype(BF16)

    a = a_ref[...].astype(BF16)
    g = jnp.dot(a, wgb_ref[...], preferred_element_type=F32)
    u = jnp.dot(a, wub_ref[...], preferred_element_type=F32)
    o_ref[...] = (g * jax.nn.sigmoid(g) * u).astype(o_ref.dtype)


def _w_spec(w, layer, k, tn, c0):
    if w.ndim == 3:
        return pl.BlockSpec((None, k, tn), lambda j, i: (layer, 0, j + c0))
    return pl.BlockSpec((k, tn), lambda j, i: (0, j + c0))


def linear(a, w, layer=0, *, col0=0, ncols=None, out_dtype=F32, tm=1024, tn=512, res=None, scale=1.0):
    m, k = a.shape
    ncols = w.shape[-1] - col0 if ncols is None else ncols
    tm = _pick(m, tm)
    tn = _pick(ncols, tn)
    assert col0 % tn == 0 and ncols % tn == 0 and m % tm == 0
    c0 = col0 // tn
    grid = (ncols // tn, m // tm)
    a_spec = pl.BlockSpec((tm, k), lambda j, i: (i, 0))
    w_spec = _w_spec(w, layer, k, tn, c0)
    o_spec = pl.BlockSpec((tm, tn), lambda j, i: (i, j))
    scratch = [pltpu.VMEM((k, tn), BF16)]
    if res is None:
        return pl.pallas_call(
            _mm_kernel, grid=grid, in_specs=[a_spec, w_spec], out_specs=o_spec,
            out_shape=jax.ShapeDtypeStruct((m, ncols), out_dtype), scratch_shapes=scratch,
            compiler_params=_cparams("parallel", "arbitrary"), name="linear",
        )(a, w)
    return pl.pallas_call(
        functools.partial(_mm_res_kernel, scale=scale), grid=grid,
        in_specs=[a_spec, w_spec, o_spec], out_specs=o_spec,
        out_shape=jax.ShapeDtypeStruct((m, ncols), F32), scratch_shapes=scratch,
        compiler_params=_cparams("parallel", "arbitrary"), name="linear_res",
    )(a, w, res)


def swiglu_in(a, w_in, layer):
    m, k = a.shape
    tm = _pick(m, 1024)
    tn = 512
    nj = D_FF // tn
    return pl.pallas_call(
        _mm_swiglu_kernel, grid=(nj, m // tm),
        in_specs=[pl.BlockSpec((tm, k), lambda j, i: (i, 0)),
                  _w_spec(w_in, layer, k, tn, 0),
                  _w_spec(w_in, layer, k, tn, nj)],
        out_specs=pl.BlockSpec((tm, tn), lambda j, i: (i, j)),
        out_shape=jax.ShapeDtypeStruct((m, D_FF), BF16),
        scratch_shapes=[pltpu.VMEM((k, tn), BF16), pltpu.VMEM((k, tn), BF16)],
        compiler_params=_cparams("parallel", "arbitrary"), name="swiglu_in",
    )(a, w_in, w_in)


def ffn(x, g, w_in, w_out, layer):
    h = rmsnorm(x, g[layer], BF16)
    hid = swiglu_in(h, w_in, layer)
    return linear(hid, w_out, layer, tm=512, tn=512, res=x, scale=0.5)


def rope_tables(pos, dh):
    rot = dh // 4
    half = rot // 2
    inv = ROPE_THETA ** (-jnp.arange(half, dtype=F32) / half)
    ang = pos.astype(F32)[:, None] * inv[None, :]
    cos, sin = jnp.cos(ang), jnp.sin(ang)
    p = pos.shape[0]
    rest1 = jnp.ones((p, dh - rot), F32)
    rest0 = jnp.zeros((p, dh - rot), F32)
    z = jnp.zeros((p, half), F32)
    c = jnp.concatenate([cos, cos, rest1], axis=1)
    up = jnp.concatenate([-sin, z, rest0], axis=1)
    dn = jnp.concatenate([z, sin, rest0], axis=1)
    rep = LANES // dh
    return tuple(jnp.tile(t, (1, rep)) for t in (c, up, dn)), half


def _rope_chunk(xc, c, up, dn, half):
    return xc * c + pltpu.roll(xc, LANES - half, 1) * up + pltpu.roll(xc, half, 1) * dn


def _cols_kernel(*refs, half, width):
    if half:
        x_ref, c_ref, u_ref, d_ref = refs[:4]
        outs = refs[4:]
        c, up, dn = c_ref[...], u_ref[...], d_ref[...]
        x = x_ref[...]
        y = jnp.concatenate(
            [_rope_chunk(x[:, k * LANES:(k + 1) * LANES], c, up, dn, half) for k in range(width // LANES)], axis=1)
    else:
        x_ref = refs[0]
        outs = refs[1:]
        y = x_ref[...]
    for o in outs:
        o[...] = y.astype(o.dtype)


def take_cols(x, col0, width, out_dtypes, tabs=None, n_pos_blocks=1):
    t = x.shape[0]
    bw = 512 if width % 512 == 0 else width
    assert col0 % bw == 0
    tables, half = tabs if tabs is not None else ((), 0)
    tm = _pick(t, 512) if tabs is None else _pick(tables[0].shape[0], 512)
    assert t % tm == 0
    npb = (tables[0].shape[0] // tm) if tabs is not None else 1
    c0 = col0 // bw
    in_specs = [pl.BlockSpec((tm, bw), lambda i, j: (i, j + c0))]
    in_specs += [pl.BlockSpec((tm, LANES), lambda i, j: (i % npb, 0)) for _ in tables]
    outs = pl.pallas_call(
        functools.partial(_cols_kernel, half=half, width=bw),
        grid=(t // tm, width // bw),
        in_specs=in_specs,
        out_specs=[pl.BlockSpec((tm, bw), lambda i, j: (i, j)) for _ in out_dtypes],
        out_shape=[jax.ShapeDtypeStruct((t, width), dt) for dt in out_dtypes],
        compiler_params=_cparams("parallel", "parallel"), name="take_cols",
    )(x, *tables)
    return outs


def _stack_heads(q, sel=None):
    parts = []
    for r in range(HEADS_PER_GROUP):
        qr = q[:, r * HD:(r + 1) * HD]
        if sel is not None:
            qr = jnp.where(sel, qr, jnp.zeros((), q.dtype))
        parts.append(qr)
    return _concat_rows(parts)


def _concat_rows(parts):
    if parts[0].shape[0] % (2 * SUBLANES):
        return jnp.concatenate([p.astype(F32) for p in parts], axis=0).astype(parts[0].dtype)
    return jnp.concatenate(parts, axis=0)


def _unstack_heads(o, tq):
    return jnp.concatenate([o[r * tq:(r + 1) * tq] for r in range(HEADS_PER_GROUP)], axis=1)


def _qk(q, k):
    return lax.dot_general(q, k, (((1,), (1,)), ((), ())), preferred_element_type=F32)


def _mask_rows(mask, s, tq):
    tk = s.shape[-1]
    return jnp.where(mask[None], s.reshape(HEADS_PER_GROUP, tq, tk), NEG_INF).reshape(HEADS_PER_GROUP * tq, tk)


def _online_update(slot, s, vt, m_sc, l_sc, acc_sc):
    m_prev = m_sc[slot]
    m_new = jnp.maximum(m_prev, jnp.max(s, axis=-1, keepdims=True))
    alpha = jnp.exp2(m_prev - m_new)
    p = jnp.exp2(s - pltpu.repeat(m_new, s.shape[-1] // LANES, axis=1))
    l_sc[slot] = alpha * l_sc[slot] + jnp.sum(p, axis=-1, keepdims=True)
    acc_sc[slot] = alpha * acc_sc[slot] + jnp.dot(p.astype(BF16), vt, preferred_element_type=F32)
    m_sc[slot] = m_new


def _init_state(m_sc, l_sc, acc_sc):
    m_sc[...] = jnp.full(m_sc.shape, M_INIT, F32)
    l_sc[...] = jnp.zeros(l_sc.shape, F32)
    acc_sc[...] = jnp.zeros(acc_sc.shape, F32)


def _order_key(x):
    b = lax.bitcast_convert_type(x, I32)
    return jnp.where(b < 0, b ^ jnp.int32(0x7FFFFFFF), b)


def _kth_largest_key(key, k):
    kf = jnp.float32(k)

    def count_ge(t):
        return jnp.sum(jnp.where(key >= t, 1.0, 0.0), axis=-1, keepdims=True)

    t0 = jnp.where(count_ge(jnp.int32(0)) >= kf, jnp.int32(0), jnp.int32(INT_MIN))

    def body(it, t):
        cand = t | (jnp.int32(1) << (jnp.int32(30) - it))
        return jnp.where(count_ge(cand) >= kf, cand, t)

    return lax.fori_loop(0, 31, body, t0)


def _n_kv_tiles(qpos0, i, tq, tk, lp):
    hi = qpos0 + (i + 1) * tq
    return jnp.minimum((hi + tk - 1) // tk, lp // tk)


def _kv_tile(lp, tq):
    return _pick(lp, 512 if tq >= 64 else 2048)


def _kv_len(k):
    return k.shape[1] if k.ndim == 3 else k.shape[2]


def _kv_spec(k, bg):
    lp = _kv_len(k)
    if k.ndim == 3:
        return pl.BlockSpec((None, lp, HD), lambda *idx: (bg(*idx)[0], 0, bg(*idx)[1]))
    return pl.BlockSpec((None, None, lp, HD), lambda *idx: (bg(*idx)[0], bg(*idx)[1], 0, 0))


def _softmax_scratch(slots, rows):
    return [pltpu.VMEM((slots, rows, LANES), F32), pltpu.VMEM((slots, rows, LANES), F32),
            pltpu.VMEM((slots, rows, HD), F32)]


def _diff_attn_kernel(lam_ref, q_ref, k_ref, v_ref, g_ref, o_ref, m_sc, l_sc, acc_sc, *, tq, tk, qpos0, lp, post):
    i = pl.program_id(2)
    q = q_ref[...]
    lane = lax.broadcasted_iota(I32, (1, HD), 1)
    lo = lane < A_HALF
    qs = (_stack_heads(q, lo), _stack_heads(q, jnp.logical_not(lo)))
    qpos = qpos0 + i * tq + lax.broadcasted_iota(I32, (tq, 1), 0)
    _init_state(m_sc, l_sc, acc_sc)
    scale = (A_HALF ** -0.5) * LOG2E

    def step(j, masked):
        off = pl.multiple_of(j * tk, tk)
        kt = k_ref[pl.ds(off, tk), :]
        vt = v_ref[pl.ds(off, tk), :]
        if masked:
            mask = off + lax.broadcasted_iota(I32, (1, tk), 1) <= qpos
        for c in range(2):
            s = _qk(qs[c], kt) * scale
            if masked:
                s = _mask_rows(mask, s, tq)
            _online_update(c, s, vt, m_sc, l_sc, acc_sc)

    n_full = jnp.minimum((qpos0 + i * tq + 1) // tk, lp // tk)
    lax.fori_loop(0, n_full, lambda j, c: (step(j, False), c)[1], 0)
    lax.fori_loop(n_full, _n_kv_tiles(qpos0, i, tq, tk, lp), lambda j, c: (step(j, True), c)[1], 0)
    o = acc_sc[0] / l_sc[0] - lam_ref[0] * (acc_sc[1] / l_sc[1])
    ms = jnp.mean(o * o, axis=-1, keepdims=True)
    o = o * lax.rsqrt(ms + EPS) * g_ref[...] * post
    o_ref[...] = _unstack_heads(o, tq).astype(o_ref.dtype)


def diff_attention(q, k, v, lam_f, subln, lam_init, qpos0):
    b, tqn, _ = q.shape
    lp = _kv_len(k)
    tq = _pick(tqn, 256)
    tk = _kv_tile(lp, tq)
    kern = functools.partial(_diff_attn_kernel, tq=tq, tk=tk, qpos0=qpos0, lp=lp, post=1.0 - lam_init)
    return pl.pallas_call(
        kern, grid=(b, GROUPS, tqn // tq),
        in_specs=[pl.BlockSpec(memory_space=pltpu.SMEM),
                  pl.BlockSpec((None, tq, GW), lambda bi, g, i: (bi, i, g)),
                  _kv_spec(k, lambda bi, g, i: (bi, g)),
                  _kv_spec(v, lambda bi, g, i: (bi, g)),
                  pl.BlockSpec((1, HD), lambda bi, g, i: (0, 0))],
        out_specs=pl.BlockSpec((None, tq, GW), lambda bi, g, i: (bi, i, g)),
        out_shape=jax.ShapeDtypeStruct(q.shape, BF16),
        scratch_shapes=_softmax_scratch(2, HEADS_PER_GROUP * tq),
        compiler_params=_cparams("parallel", "parallel", "parallel"), name="diff_attention",
    )(lam_f.reshape(1).astype(F32), q, k, v, subln.reshape(1, HD))


def _nsa_cmp_kernel(q_ref, fsk_ref, fsv_ref, b_ref, o_ref, sel_ref, *, tq, w, nblk, nch, qpos0, klanes):
    i = pl.program_id(2)
    qst = _stack_heads(q_ref[...])
    fsk = fsk_ref[...]
    fsv = fsv_ref[...]
    kcb = fsk[:, :HD] + pltpu.roll(fsk[:, HD:], w - 1, 0) + b_ref[0:1, :]
    vcb = fsv[:, :HD] + pltpu.roll(fsv[:, HD:], w - 1, 0) + b_ref[1:2, :]
    qpos = qpos0 + i * tq + lax.broadcasted_iota(I32, (tq, 1), 0)
    lanei = lax.broadcasted_iota(I32, (1, w), 1)
    s = _qk(qst, kcb.astype(BF16)) * (B_HD ** -0.5)
    valid = (lanei * CMP_STRIDE + (CMP_LEN - 1) <= qpos) & (lanei < nblk)
    s3 = jnp.where(valid[None], s.reshape(HEADS_PER_GROUP, tq, w), NEG_INF)
    e = jnp.exp(s3 - jnp.max(s3, axis=-1, keepdims=True))
    p = e / jnp.sum(e, axis=-1, keepdims=True)
    any_valid = (qpos >= CMP_LEN - 1) & (nblk > 0)
    p = jnp.where(any_valid[None], p, 0.0)
    o = jnp.dot(p.reshape(HEADS_PER_GROUP * tq, w).astype(BF16), vcb.astype(BF16), preferred_element_type=F32)
    o_ref[...] = _unstack_heads(o, tq).astype(o_ref.dtype)

    grp = p[0] + p[1] + p[2] + p[3]
    chunk = grp + jnp.where(lanei == 0, 0.0, pltpu.roll(grp, 1, 1))
    pair = jnp.where((lanei & 1) == 0, chunk + pltpu.roll(chunk, w - 1, 1), chunk + pltpu.roll(chunk, 1, 1))
    quad = jnp.where((lanei & 2) == 0, pair + pltpu.roll(pair, w - 2, 1), pair + pltpu.roll(pair, 2, 1))
    jb = lanei >> 2
    cur = qpos >> 6
    forced = (jb == 0) | (jb == cur) | (jb == cur - 1)
    score = jnp.where(forced, FORCE, jnp.where(jb * SEL_BLOCK <= qpos, quad, NEG_INF))
    score = jnp.where(lanei < nch, score, NEG_INF)
    key = _order_key(score)
    thr = _kth_largest_key(key, klanes)
    sel_ref[...] = jnp.where((key >= thr) | (lanei >= nch), 1.0, 0.0).astype(sel_ref.dtype)


def nsa_compressed(q_raw, fsk, fsv, bias, *, nblk, nch, qpos0, n_top_rep):
    b, tqn, _ = q_raw.shape
    w = fsk.shape[2]
    tq = _pick(tqn, 256)
    kern = functools.partial(_nsa_cmp_kernel, tq=tq, w=w, nblk=nblk, nch=nch, qpos0=qpos0,
                             klanes=(SEL_BLOCK // CMP_STRIDE) * n_top_rep)
    return pl.pallas_call(
        kern, grid=(b, GROUPS, tqn // tq),
        in_specs=[pl.BlockSpec((None, tq, GW), lambda bi, g, i: (bi, i, g)),
                  pl.BlockSpec((None, None, w, 2 * HD), lambda bi, g, i: (bi, g, 0, 0)),
                  pl.BlockSpec((None, None, w, 2 * HD), lambda bi, g, i: (bi, g, 0, 0)),
                  pl.BlockSpec((2, HD), lambda bi, g, i: (0, 0))],
        out_specs=[pl.BlockSpec((None, tq, GW), lambda bi, g, i: (bi, i, g)),
                   pl.BlockSpec((None, None, tq, w), lambda bi, g, i: (bi, g, i, 0))],
        out_shape=[jax.ShapeDtypeStruct(q_raw.shape, F32), jax.ShapeDtypeStruct((b, GROUPS, tqn, w), BF16)],
        compiler_params=_cparams("parallel", "parallel", "parallel"), name="nsa_compressed",
    )(q_raw, fsk, fsv, bias)


def _nsa_sw_kernel(q_ref, ks_ref, vs_ref, kw_ref, vw_ref, sel_ref, oc_ref, gate_ref, o_ref, m_sc, l_sc, acc_sc,
                   *, tq, tk, tkw, qpos0, lp, lw, kwpos0, w):
    i = pl.program_id(2)
    qst = _stack_heads(q_ref[...])
    qpos = qpos0 + i * tq + lax.broadcasted_iota(I32, (tq, 1), 0)
    _init_state(m_sc, l_sc, acc_sc)
    scale = (B_HD ** -0.5) * LOG2E
    chunks_per_tile = tk // CMP_STRIDE
    assert LANES % chunks_per_tile == 0 and w % LANES == 0

    def sel_body(j, carry):
        off = pl.multiple_of(j * tk, tk)
        kt = ks_ref[pl.ds(off, tk), :]
        vt = vs_ref[pl.ds(off, tk), :]
        kpos = off + lax.broadcasted_iota(I32, (1, tk), 1)
        cbase = pl.multiple_of((j * chunks_per_tile) // LANES * LANES, LANES)
        chunk_row = cbase + lax.broadcasted_iota(I32, (LANES, 1), 0)
        expand = jnp.where(chunk_row == (kpos >> 4), 1.0, 0.0).astype(BF16)
        chosen = jnp.dot(sel_ref[:, pl.ds(cbase, LANES)], expand, preferred_element_type=F32) > 0.5
        mask = chosen & (kpos <= qpos)
        _online_update(0, _mask_rows(mask, _qk(qst, kt) * scale, tq), vt, m_sc, l_sc, acc_sc)
        return carry

    lax.fori_loop(0, _n_kv_tiles(qpos0, i, tq, tk, lp), sel_body, 0)

    def win_body(j, carry):
        off = pl.multiple_of(j * tkw, tkw)
        kt = kw_ref[pl.ds(off, tkw), :]
        vt = vw_ref[pl.ds(off, tkw), :]
        kpos = kwpos0 + off + lax.broadcasted_iota(I32, (1, tkw), 1)
        dt = qpos - kpos
        mask = (dt >= 0) & (dt <= WINDOW)
        _online_update(1, _mask_rows(mask, _qk(qst, kt) * scale, tq), vt, m_sc, l_sc, acc_sc)
        return carry

    first = jnp.maximum(qpos0 + i * tq - WINDOW - kwpos0, 0) // tkw
    last = jnp.minimum((qpos0 + (i + 1) * tq - kwpos0 + tkw - 1) // tkw, lw // tkw)
    lax.fori_loop(first, last, win_body, 0)

    o_sel = _unstack_heads(acc_sc[0] / l_sc[0], tq)
    o_win = _unstack_heads(acc_sc[1] / l_sc[1], tq)
    gsig = jax.nn.sigmoid(gate_ref[...])
    oc = oc_ref[...]
    outs = []
    for r in range(HEADS_PER_GROUP):
        cs = slice(r * HD, (r + 1) * HD)
        outs.append(oc[:, cs] * gsig[:, 3 * r:3 * r + 1] + o_sel[:, cs] * gsig[:, 3 * r + 1:3 * r + 2]
                    + o_win[:, cs] * gsig[:, 3 * r + 2:3 * r + 3])
    o_ref[...] = jnp.concatenate(outs, axis=1).astype(o_ref.dtype)


def nsa_select_window(q_rot, ks, vs, kw, vw, selmask, o_cmp, gates, *, qpos0, kwpos0):
    b, tqn, _ = q_rot.shape
    lp, lw, w = _kv_len(ks), _kv_len(kw), selmask.shape[3]
    tq = _pick(tqn, 256)
    tk = _kv_tile(lp, tq)
    tkw = _kv_tile(lw, tq)
    kern = functools.partial(_nsa_sw_kernel, tq=tq, tk=tk, tkw=tkw, qpos0=qpos0, lp=lp, lw=lw, kwpos0=kwpos0, w=w)
    qspec = pl.BlockSpec((None, tq, GW), lambda bi, g, i: (bi, i, g))
    bg = lambda bi, g, i: (bi, g)
    return pl.pallas_call(
        kern, grid=(b, GROUPS, tqn // tq),
        in_specs=[qspec, _kv_spec(ks, bg), _kv_spec(vs, bg), _kv_spec(kw, bg), _kv_spec(vw, bg),
                  pl.BlockSpec((None, None, tq, w), lambda bi, g, i: (bi, g, i, 0)),
                  qspec,
                  pl.BlockSpec((None, None, tq, LANES), lambda bi, g, i: (bi, g, i, 0))],
        out_specs=qspec,
        out_shape=jax.ShapeDtypeStruct(q_rot.shape, BF16),
        scratch_shapes=_softmax_scratch(2, HEADS_PER_GROUP * tq),
        compiler_params=_cparams("parallel", "parallel", "parallel"), name="nsa_select_window",
    )(q_rot, ks, vs, kw, vw, selmask, o_cmp, gates)


def _ik_kernel(t_ref, g_ref, c_ref, u_ref, d_ref, ikf_ref, ik2_ref, *, half):
    x = t_ref[...]
    lane = lax.broadcasted_iota(I32, (1, LANES), 1)
    xm = jnp.where(lane < IDX_DIM, x, 0.0)
    ms = jnp.sum(xm * xm, axis=-1, keepdims=True) * (1.0 / IDX_DIM)
    y = xm * lax.rsqrt(ms + EPS) * g_ref[...]
    y = _rope_chunk(y, c_ref[...], u_ref[...], d_ref[...], half)
    ikf_ref[...] = y
    ik2_ref[...] = (y + pltpu.roll(y, IDX_DIM, 1)).astype(ik2_ref.dtype)


def index_keys(tail, knorm, tabs):
    t = tail.shape[0]
    tables, half = tabs
    tm = _pick(tables[0].shape[0], 512)
    npb = tables[0].shape[0] // tm
    g = jnp.concatenate([knorm.astype(F32), jnp.zeros((LANES - IDX_DIM,), F32)]).reshape(1, LANES)
    rspec = pl.BlockSpec((tm, LANES), lambda i: (i, 0))
    tspec = pl.BlockSpec((tm, LANES), lambda i: (i % npb, 0))
    return pl.pallas_call(
        functools.partial(_ik_kernel, half=half), grid=(t // tm,),
        in_specs=[rspec, pl.BlockSpec((1, LANES), lambda i: (0, 0)), tspec, tspec, tspec],
        out_specs=[rspec, rspec],
        out_shape=[jax.ShapeDtypeStruct((t, LANES), F32), jax.ShapeDtypeStruct((t, LANES), BF16)],
        compiler_params=_cparams("parallel"), name="index_keys",
    )(tail, g, *tables)


def _dsa_index_kernel(iq_ref, ik_ref, tail_ref, mask_ref, sc_sc, *, tq, tk, qpos0, lp, top):
    i = pl.program_id(1)
    iq = iq_ref[...]
    lane = lax.broadcasted_iota(I32, (1, LANES), 1)
    lo = lane < IDX_DIM
    hi = jnp.logical_not(lo)
    zero = jnp.zeros((), iq.dtype)
    parts = []
    for h in range(IDX_HEADS):
        pair = iq[:, (h // 2) * LANES:(h // 2 + 1) * LANES]
        parts.append(jnp.where(lo if h % 2 == 0 else hi, pair, zero))
    iqst = _concat_rows(parts)
    iw = tail_ref[:, IDX_DIM:IDX_DIM + IDX_HEADS] * ((IDX_DIM ** -0.5) * (IDX_HEADS ** -0.5))
    qpos = qpos0 + i * tq + lax.broadcasted_iota(I32, (tq, 1), 0)
    sc_sc[...] = jnp.full(sc_sc.shape, NEG_INF, F32)

    def body(j, carry):
        off = pl.multiple_of(j * tk, tk)
        sc = _qk(iqst, ik_ref[pl.ds(off, tk), :])
        acc = jnp.zeros((tq, tk), F32)
        for h in range(IDX_HEADS):
            acc = acc + jnp.maximum(sc[h * tq:(h + 1) * tq], 0.0) * iw[:, h:h + 1]
        kpos = off + lax.broadcasted_iota(I32, (1, tk), 1)
        sc_sc[:, pl.ds(off, tk)] = jnp.where(kpos <= qpos, acc, NEG_INF)
        return carry

    lax.fori_loop(0, _n_kv_tiles(qpos0, i, tq, tk, lp), body, 0)
    key = _order_key(sc_sc[...])
    thr = _kth_largest_key(key, top)
    kpos_all = lax.broadcasted_iota(I32, (1, lp), 1)
    mask_ref[...] = jnp.where((key >= thr) & (kpos_all <= qpos), 1.0, 0.0).astype(mask_ref.dtype)


def dsa_select(iq, ik2, tail, *, qpos0, top):
    b, tqn, _ = iq.shape
    lp = ik2.shape[1]
    tq = _pick(tqn, 128)
    tk = _kv_tile(lp, tq)
    kern = functools.partial(_dsa_index_kernel, tq=tq, tk=tk, qpos0=qpos0, lp=lp, top=top)
    return pl.pallas_call(
        kern, grid=(b, tqn // tq),
        in_specs=[pl.BlockSpec((None, tq, IDX_HEADS * IDX_DIM), lambda bi, i: (bi, i, 0)),
                  pl.BlockSpec((None, lp, LANES), lambda bi, i: (bi, 0, 0)),
                  pl.BlockSpec((None, tq, LANES), lambda bi, i: (bi, i, 0))],
        out_specs=pl.BlockSpec((None, tq, lp), lambda bi, i: (bi, i, 0)),
        out_shape=jax.ShapeDtypeStruct((b, tqn, lp), BF16),
        scratch_shapes=[pltpu.VMEM((tq, lp), F32)],
        compiler_params=_cparams("parallel", "parallel"), name="dsa_select",
    )(iq, ik2, tail)


def _masked_attn_kernel(q_ref, k_ref, v_ref, mask_ref, o_ref, m_sc, l_sc, acc_sc, *, tq, tk, qpos0, lp):
    i = pl.program_id(1)
    qst = _stack_heads(q_ref[...])
    _init_state(m_sc, l_sc, acc_sc)
    scale = (C_HD ** -0.5) * LOG2E

    def body(j, carry):
        off = pl.multiple_of(j * tk, tk)
        kt = k_ref[pl.ds(off, tk), :]
        vt = v_ref[pl.ds(off, tk), :]
        mask = mask_ref[:, pl.ds(off, tk)] > 0.5
        _online_update(0, _mask_rows(mask, _qk(qst, kt) * scale, tq), vt, m_sc, l_sc, acc_sc)
        return carry

    lax.fori_loop(0, _n_kv_tiles(qpos0, i, tq, tk, lp), body, 0)
    o_ref[...] = _unstack_heads(acc_sc[0] / l_sc[0], tq).astype(o_ref.dtype)


def masked_attention(q, k, v, mask, *, qpos0):
    b, tqn, _ = q.shape
    lp = _kv_len(k)
    tq = _pick(tqn, 128)
    tk = _kv_tile(lp, tq)
    kern = functools.partial(_masked_attn_kernel, tq=tq, tk=tk, qpos0=qpos0, lp=lp)
    qspec = pl.BlockSpec((None, tq, GW), lambda bi, i, g: (bi, i, g))
    bg = lambda bi, i, g: (bi, g)
    return pl.pallas_call(
        kern, grid=(b, tqn // tq, GROUPS),
        in_specs=[qspec, _kv_spec(k, bg), _kv_spec(v, bg),
                  pl.BlockSpec((None, tq, lp), lambda bi, i, g: (bi, i, 0))],
        out_specs=qspec,
        out_shape=jax.ShapeDtypeStruct(q.shape, BF16),
        scratch_shapes=_softmax_scratch(1, HEADS_PER_GROUP * tq),
        compiler_params=_cparams("parallel", "parallel", "parallel"), name="masked_attention",
    )(q, k, v, mask)


def _cross_kernel(q_ref, k_ref, v_ref, o_ref):
    q = q_ref[...]
    k = k_ref[...].astype(BF16)
    v = v_ref[...].astype(BF16)
    outs = []
    for h in range(X_HEADS):
        cs = slice(h * X_HD, (h + 1) * X_HD)
        s = _qk(q[:, cs], k[:, cs]) * (X_HD ** -0.5)
        e = jnp.exp(s - jnp.max(s, axis=-1, keepdims=True))
        p = e / jnp.sum(e, axis=-1, keepdims=True)
        outs.append(jnp.dot(p.astype(BF16), v[:, cs], preferred_element_type=F32))
    o_ref[...] = jnp.concatenate(outs, axis=1).astype(o_ref.dtype)


def cross_attention(q, mk, mv):
    b, tqn, _ = q.shape
    tq = _pick(tqn, 512)
    ml = mk.shape[1]
    qspec = pl.BlockSpec((None, tq, X_W), lambda bi, i: (bi, i, 0))
    mspec = pl.BlockSpec((None, ml, X_W), lambda bi, i: (bi, 0, 0))
    return pl.pallas_call(
        _cross_kernel, grid=(b, tqn // tq), in_specs=[qspec, mspec, mspec], out_specs=qspec,
        out_shape=jax.ShapeDtypeStruct(q.shape, BF16),
        compiler_params=_cparams("parallel", "parallel"), name="cross_attention",
    )(q, mk, mv)


PAGES_PER_STEP = 16


def _gather_kernel(pt_ref, *refs, pps, grouped):
    pools, new_ref, o_ref = refs[:pps], refs[pps], refs[pps + 1]
    s = pl.program_id(1)
    n_steps = pl.num_programs(1) - 1

    @pl.when(s < n_steps)
    def _():
        for k in range(pps):
            rows = slice(k * PAGE_SIZE, (k + 1) * PAGE_SIZE)
            if grouped:
                for g in range(GROUPS):
                    o_ref[g, rows, :] = pools[k][pl.ds(g, PAGE_SIZE, stride=GROUPS), :].astype(o_ref.dtype)
            else:
                x = pools[k][...].astype(o_ref.dtype)
                o_ref[rows, :] = jnp.concatenate([x, x], axis=1)

    @pl.when(s == n_steps)
    def _():
        o_ref[...] = jnp.zeros(o_ref.shape, o_ref.dtype)
        if grouped:
            o_ref[:, 0:PAGE_SIZE, :] = new_ref[...]
        else:
            o_ref[0:PAGE_SIZE, :] = new_ref[...]


def gather_pages(pool, layer, page_table, new_rows):
    b, n_pages = page_table.shape
    grouped = pool.ndim == 5
    if grouped:
        pool = pool.reshape(pool.shape[0], pool.shape[1], PAGE_SIZE * GROUPS, HD)
    pps = math.gcd(PAGES_PER_STEP, n_pages)
    n_steps = n_pages // pps
    rows = pps * PAGE_SIZE
    lp = (n_steps + 1) * rows

    def pool_spec(k):
        def page(bi, s, pt):
            return pt[bi, jnp.minimum(s * pps + k, n_pages - 1)]
        return pl.BlockSpec((None, None) + pool.shape[2:], lambda bi, s, pt: (layer, page(bi, s, pt), 0, 0))

    if grouped:
        new_spec = pl.BlockSpec((None, GROUPS, PAGE_SIZE, HD), lambda bi, s, pt: (bi, 0, 0, 0))
        out_spec = pl.BlockSpec((None, GROUPS, rows, HD), lambda bi, s, pt: (bi, 0, s, 0))
        out_shape = jax.ShapeDtypeStruct((b, GROUPS, lp, HD), BF16)
    else:
        new_spec = pl.BlockSpec((None, PAGE_SIZE, LANES), lambda bi, s, pt: (bi, 0, 0))
        out_spec = pl.BlockSpec((None, rows, LANES), lambda bi, s, pt: (bi, s, 0))
        out_shape = jax.ShapeDtypeStruct((b, lp, LANES), BF16)
    grid_spec = pltpu.PrefetchScalarGridSpec(
        num_scalar_prefetch=1, grid=(b, n_steps + 1),
        in_specs=[pool_spec(k) for k in range(pps)] + [new_spec], out_specs=out_spec)
    return pl.pallas_call(
        functools.partial(_gather_kernel, pps=pps, grouped=grouped), grid_spec=grid_spec, out_shape=out_shape,
        compiler_params=_cparams("parallel", "arbitrary"), name="gather_pages",
    )(page_table, *([pool] * pps), new_rows)


def lambda_init(layer):
    return 0.8 - 0.6 * math.exp(-0.3 * layer)


def _pad_rows(x, n):
    return jnp.pad(x, ((0, 0), (0, n - x.shape[1]), (0, 0)))


def trunk(x, nb, tq_real, qpos0, mem_k, mem_v, W, past):
    t = nb * tq_real
    tqp = max(tq_real, SUBLANES)
    pos = qpos0 + jnp.arange(tq_real, dtype=I32)
    pos_rows = jnp.tile(pos, nb) if tq_real < SUBLANES else pos
    tab64 = rope_tables(pos_rows, 64)
    tab128 = rope_tables(pos_rows, 128)
    new = {}

    def to_attn(a):
        return _pad_rows(a.reshape(nb, tq_real, a.shape[-1]), tqp)

    def from_attn(a):
        return a[:, :tq_real].reshape(t, a.shape[-1])

    if past is not None:
        pt = past['page_table']
        past_len = pt.shape[1] * PAGE_SIZE

        def with_past(name, layer, new_bf):
            nr = new_bf.reshape(nb, tq_real, GROUPS, HD).transpose(0, 2, 1, 3)
            nr = jnp.pad(nr, ((0, 0), (0, 0), (0, PAGE_SIZE - tq_real), (0, 0)))
            return gather_pages(past[name], layer, pt, nr)
    else:
        past_len = 0

    for l in range(DEPTH):
        x = ffn(x, W['norm_ffn1'], W['ffn1_w_in'], W['ffn1_w_out'], l)
        h = rmsnorm(x, W['norm_mix'][l], BF16)
        i = l // N_MIXERS
        kind = l % N_MIXERS
        if kind == 0:
            proj = linear(h, W['a_w_in'], i)
            (q_rot,) = take_cols(proj, 0, A_Q, (BF16,), tab64)
            k_f, k_b = take_cols(proj, A_Q, A_K, (F32, BF16), tab64)
            (v_b,) = take_cols(proj, A_Q + A_K, A_KV_HEADS * A_VDIM, (BF16,))
            v_f = proj[:, A_Q + A_K:]
            new.setdefault('a_k', []).append(k_f.reshape(nb, tq_real, A_KV_HEADS, 2 * A_HALF))
            new.setdefault('a_v', []).append(v_f.reshape(nb, tq_real, A_KV_HEADS, A_VDIM))
            if past is None:
                k_all, v_all = k_b.reshape(nb, tq_real, -1), v_b.reshape(nb, tq_real, -1)
            else:
                k_all = with_past('a_k', i, k_b)
                v_all = with_past('a_v', i, v_b)
            lam = W['a_lambda'][i]
            lam_f = (jnp.exp(jnp.sum(lam[0] * lam[1])) - jnp.exp(jnp.sum(lam[2] * lam[3]))).astype(F32) + lambda_init(l)
            o = diff_attention(to_attn(q_rot), k_all, v_all, lam_f, W['a_subln'][i], lambda_init(l), qpos0)
            y_in, w_out = from_attn(o), W['a_w_out']
        elif kind == 1:
            w_in = W['b_w_in']
            n_main = B_Q + 6 * B_KV
            proj = linear(h, w_in, i, ncols=n_main)
            w_tail = jnp.pad(w_in[i, :, n_main:], ((0, 0), (0, LANES - (w_in.shape[-1] - n_main))))
            gate_logits = linear(h, w_tail, tn=LANES)
            (q_raw,) = take_cols(proj, 0, B_Q, (BF16,))
            (q_rot,) = take_cols(proj, 0, B_Q, (BF16,), tab128)
            kc_f = proj[:, B_Q:B_Q + B_KV]
            vc_f = proj[:, B_Q + B_KV:B_Q + 2 * B_KV]
            ks_f, ks_b = take_cols(proj, B_Q + 2 * B_KV, B_KV, (F32, BF16), tab128)
            vs_f = proj[:, B_Q + 3 * B_KV:B_Q + 4 * B_KV]
            (vs_b,) = take_cols(proj, B_Q + 3 * B_KV, B_KV, (BF16,))
            kw_f, kw_b = take_cols(proj, B_Q + 4 * B_KV, B_KV, (F32, BF16), tab128)
            vw_f = proj[:, B_Q + 5 * B_KV:B_Q + 6 * B_KV]
            (vw_b,) = take_cols(proj, B_Q + 5 * B_KV, B_KV, (BF16,))
            shp = (nb, tq_real, B_KV_HEADS, B_HD)
            for nm, a in (('b_cmp_k', kc_f), ('b_cmp_v', vc_f), ('b_sel_k', ks_f), ('b_sel_v', vs_f)):
                new.setdefault(nm, []).append(a.reshape(shp))

            cw = W['b_cmp_w'][i]
            wfs = [jnp.concatenate([cw[s, :CMP_STRIDE].reshape(CMP_STRIDE * B_HD, B_HD),
                                    cw[s, CMP_STRIDE:].reshape(CMP_STRIDE * B_HD, B_HD)], axis=1) for s in range(2)]
            if past is None:
                l_all = tq_real
                rows_k, rows_v = kc_f.reshape(nb, l_all, B_KV), vc_f.reshape(nb, l_all, B_KV)
                ks_all, vs_all = ks_b.reshape(nb, tq_real, -1), vs_b.reshape(nb, tq_real, -1)
                kw_all, vw_all = kw_b.reshape(nb, tq_real, -1), vw_b.reshape(nb, tq_real, -1)
                kwpos0 = 0
                keep = min(WINDOW, tq_real)
                new.setdefault('b_win_k', []).append(kw_f.reshape(shp)[:, tq_real - keep:])
                new.setdefault('b_win_v', []).append(vw_f.reshape(shp)[:, tq_real - keep:])
                lp_sel = tq_real
            else:
                l_all = past_len + tq_real
                zeros_new = jnp.zeros((nb * tq_real, B_KV), BF16)
                rows_k = with_past('b_cmp_k', i, zeros_new)
                rows_v = with_past('b_cmp_v', i, zeros_new)
                ks_all = with_past('b_sel_k', i, ks_b)
                vs_all = with_past('b_sel_v', i, vs_b)
                bkw = past['b_win_k'][i].reshape(nb, -1, B_KV)
                bvw = past['b_win_v'][i].reshape(nb, -1, B_KV)
                wb = bkw.shape[1]
                kb = jnp.concatenate([bkw, kw_f.reshape(nb, tq_real, B_KV)], axis=1)
                vb = jnp.concatenate([bvw, vw_f.reshape(nb, tq_real, B_KV)], axis=1)
                new.setdefault('b_win_k', []).append(kb[:, tq_real:].reshape(nb, wb, B_KV_HEADS, B_HD))
                new.setdefault('b_win_v', []).append(vb[:, tq_real:].reshape(nb, wb, B_KV_HEADS, B_HD))
                lw = -(-(wb + tq_real) // 512) * 512
                kw_all = _pad_rows(kb, lw).astype(BF16)
                vw_all = _pad_rows(vb, lw).astype(BF16)
                kwpos0 = qpos0 - wb
                lp_sel = _kv_len(ks_all)
            nch = (l_all // CMP_STRIDE)
            nblk = nch - 1
            n_sel = -(-l_all // SEL_BLOCK)
            n_rep = nch // (SEL_BLOCK // CMP_STRIDE)
            assert nch % (SEL_BLOCK // CMP_STRIDE) == 0 and n_sel - n_rep in (0, 1)
            n_top_rep = min(SEL_N, n_sel) - (n_sel - n_rep)
            wch = -(-(lp_sel // CMP_STRIDE) // LANES) * LANES

            def partials(rows, wf):
                if rows.ndim == 4:
                    xg = rows[:, :, :nch * CMP_STRIDE].reshape(nb * B_KV_HEADS * nch, CMP_STRIDE * B_HD)
                else:
                    xg = rows[:, :nch * CMP_STRIDE].reshape(nb, nch, CMP_STRIDE, B_KV_HEADS, B_HD)
                    xg = xg.transpose(0, 3, 1, 2, 4).reshape(nb * B_KV_HEADS * nch, CMP_STRIDE * B_HD)
                fs = linear(xg, wf, tn=2 * B_HD).reshape(nb, B_KV_HEADS, nch, 2 * B_HD)
                return jnp.pad(fs, ((0, 0), (0, 0), (0, wch - nch), (0, 0)))

            fsk, fsv = partials(rows_k, wfs[0]), partials(rows_v, wfs[1])
            o_cmp, selmask = nsa_compressed(to_attn(q_raw), fsk, fsv, W['b_cmp_b'][i].astype(F32),
                                            nblk=nblk, nch=nch, qpos0=qpos0, n_top_rep=n_top_rep)
            gates = gate_logits[:, :3 * B_HEADS].reshape(nb, tq_real, B_KV_HEADS, 3 * HEADS_PER_GROUP)
            gates = jnp.pad(gates.transpose(0, 2, 1, 3),
                            ((0, 0), (0, 0), (0, tqp - tq_real), (0, LANES - 3 * HEADS_PER_GROUP)))
            o = nsa_select_window(to_attn(q_rot), ks_all, vs_all, kw_all, vw_all, selmask, o_cmp, gates,
                                  qpos0=qpos0, kwpos0=kwpos0)
            y_in, w_out = from_attn(o), W['b_w_out']
        else:
            w_in = W['c_w_in']
            n_main = C_Q + 2 * C_KV + IDX_HEADS * IDX_DIM
            proj = linear(h, w_in, i, ncols=n_main)
            w_tail = jnp.pad(w_in[i, :, n_main:], ((0, 0), (0, LANES - (w_in.shape[-1] - n_main))))
            tail = linear(h, w_tail, tn=LANES)
            (q_rot,) = take_cols(proj, 0, C_Q, (BF16,), tab128)
            k_f, k_b = take_cols(proj, C_Q, C_KV, (F32, BF16), tab128)
            v_f = proj[:, C_Q + C_KV:C_Q + 2 * C_KV]
            (v_b,) = take_cols(proj, C_Q + C_KV, C_KV, (BF16,))
            (iq_rot,) = take_cols(proj, C_Q + 2 * C_KV, IDX_HEADS * IDX_DIM, (BF16,), tab64)
            ik_f, ik2 = index_keys(tail, W['c_idx_knorm'][i], tab64)
            shp = (nb, tq_real, C_KV_HEADS, C_HD)
            new.setdefault('c_k', []).append(k_f.reshape(shp))
            new.setdefault('c_v', []).append(v_f.reshape(shp))
            new.setdefault('c_idx_k', []).append(ik_f[:, :IDX_DIM].reshape(nb, tq_real, IDX_DIM))
            if past is None:
                l_all = tq_real
                k_all, v_all = k_b.reshape(nb, tq_real, -1), v_b.reshape(nb, tq_real, -1)
                ik_all = ik2.reshape(nb, tq_real, LANES)
            else:
                l_all = past_len + tq_real
                k_all = with_past('c_k', i, k_b)
                v_all = with_past('c_v', i, v_b)
                ik_all = gather_pages(past['c_idx_k'], i, pt, _pad_rows(ik2.reshape(nb, tq_real, LANES), PAGE_SIZE))
            top = min(IDX_TOPK_MAX, l_all // 4)
            mask = dsa_select(to_attn(iq_rot), ik_all, to_attn(tail), qpos0=qpos0, top=top)
            o = masked_attention(to_attn(q_rot), k_all, v_all, mask, qpos0=qpos0)
            y_in, w_out = from_attn(o), W['c_w_out']
        x = linear(y_in, w_out, i, res=x, scale=1.0)

        h = rmsnorm(x, W['norm_cross'][l], BF16)
        qx = linear(h, W['x_w_q'], l, out_dtype=BF16)
        ox = cross_attention(to_attn(qx), mem_k[l], mem_v[l])
        x = linear(from_attn(ox), W['x_w_o'], l, res=x, scale=1.0)
        x = ffn(x, W['norm_ffn2'], W['ffn2_w_in'], W['ffn2_w_out'], l)
    y = rmsnorm(x, W['final_norm'], F32)
    return y, {nm: jnp.stack(v) for nm, v in new.items()}


def kernel(x_prompt, x_sample, cache_a_k, cache_a_v, cache_b_cmp_k, cache_b_cmp_v, cache_b_sel_k, cache_b_sel_v, state_b_win_k, state_b_win_v, cache_c_k, cache_c_v, cache_c_idx_k, cache_mem_k, cache_mem_v, page_table, mem_prompt, norm_ffn1, norm_mix, norm_cross, norm_ffn2, final_norm, ffn1_w_in, ffn1_w_out, ffn2_w_in, ffn2_w_out, x_w_q, x_w_kv, x_w_o, a_w_in, a_w_out, a_lambda, a_subln, b_w_in, b_w_out, b_cmp_w, b_cmp_b, c_w_in, c_w_out, c_idx_knorm):
    W = dict(norm_ffn1=norm_ffn1, norm_mix=norm_mix, norm_cross=norm_cross, norm_ffn2=norm_ffn2,
             final_norm=final_norm, ffn1_w_in=ffn1_w_in, ffn1_w_out=ffn1_w_out, ffn2_w_in=ffn2_w_in,
             ffn2_w_out=ffn2_w_out, x_w_q=x_w_q, x_w_o=x_w_o, a_w_in=a_w_in, a_w_out=a_w_out,
             a_lambda=a_lambda, a_subln=a_subln, b_w_in=b_w_in, b_w_out=b_w_out, b_cmp_w=b_cmp_w,
             b_cmp_b=b_cmp_b, c_w_in=c_w_in, c_w_out=c_w_out, c_idx_knorm=c_idx_knorm)
    nbp, seq, d = x_prompt.shape
    nbs, dseq, _ = x_sample.shape
    ml = mem_prompt.shape[1]

    mem2d = mem_prompt.reshape(nbp * ml, d)
    mkv = [linear(mem2d, x_w_kv, l) for l in range(DEPTH)]
    p_mem_k = jnp.stack([m[:, :X_W].reshape(nbp, ml, X_HEADS, X_HD) for m in mkv])
    p_mem_v = jnp.stack([m[:, X_W:].reshape(nbp, ml, X_HEADS, X_HD) for m in mkv])
    y_p, ps = trunk(x_prompt.reshape(nbp * seq, d), nbp, seq, 0,
                    p_mem_k.reshape(DEPTH, nbp, ml, X_W), p_mem_v.reshape(DEPTH, nbp, ml, X_W), W, None)

    past = dict(a_k=cache_a_k, a_v=cache_a_v, b_cmp_k=cache_b_cmp_k, b_cmp_v=cache_b_cmp_v,
                b_sel_k=cache_b_sel_k, b_sel_v=cache_b_sel_v, b_win_k=state_b_win_k, b_win_v=state_b_win_v,
                c_k=cache_c_k, c_v=cache_c_v, c_idx_k=cache_c_idx_k, page_table=page_table)
    past_len = page_table.shape[1] * PAGE_SIZE
    sml = cache_mem_k.shape[2]
    y_s, ss = trunk(x_sample.reshape(nbs * dseq, d), nbs, dseq, past_len,
                    cache_mem_k.reshape(DEPTH, nbs, sml, X_W), cache_mem_v.reshape(DEPTH, nbs, sml, X_W), W, past)

    return (y_p.reshape(nbp, seq, d), y_s.reshape(nbs, dseq, d),
            ps['a_k'], ps['a_v'], ps['b_cmp_k'], ps['b_cmp_v'], ps['b_sel_k'], ps['b_sel_v'],
            ps['b_win_k'], ps['b_win_v'], ps['c_k'], ps['c_v'], ps['c_idx_k'], p_mem_k, p_mem_v,
            ss['a_k'], ss['a_v'], ss['b_cmp_k'], ss['b_cmp_v'], ss['b_sel_k'], ss['b_sel_v'],
            ss['b_win_k'], ss['b_win_v'], ss['c_k'], ss['c_v'], ss['c_idx_k'])
```

```python
import functools
import math

import jax
import jax.numpy as jnp
from jax import lax
from jax.experimental import pallas as pl
from jax.experimental.pallas import tpu as pltpu

F32 = jnp.float32
BF16 = jnp.bfloat16
I32 = jnp.int32

D_MODEL = 2048
DEPTH = 4
PAGE_SIZE = 128
N_MIXERS = 3
ROPE_THETA = 500000.0
EPS = 1e-6
NEG_INF = -1e30
FORCE = 1e9
D_FF = 256 * math.ceil(8 * D_MODEL / 3 / 256)

A_HEADS = D_MODEL // 128
A_HALF = 64
A_VDIM = 128
A_KV_HEADS = 4
A_Q = A_HEADS * 2 * A_HALF
A_K = A_KV_HEADS * 2 * A_HALF

B_HEADS = D_MODEL // 128
B_HD = 128
B_KV_HEADS = 4
B_KV = B_KV_HEADS * B_HD
B_Q = B_HEADS * B_HD
CMP_STRIDE = 16
CMP_LEN = 32
SEL_BLOCK = 64
SEL_N = 16
WINDOW = 512

C_HEADS = D_MODEL // 128
C_HD = 128
C_KV_HEADS = 4
C_Q = C_HEADS * C_HD
C_KV = C_KV_HEADS * C_HD
IDX_HEADS = 16
IDX_DIM = 64
IDX_TOPK_MAX = 256

MEM_LEN = 256
X_HEADS = 4
X_HD = 128
X_W = X_HEADS * X_HD

GROUPS = 4
HEADS_PER_GROUP = 4
HD = 128
GW = HEADS_PER_GROUP * HD

LANES = 128
SUBLANES = 8
VMEM_LIMIT_BYTES = 56 * 1024 * 1024
M_INIT = -1e29
LOG2E = 1.4426950408889634
INT_MIN = -2147483648


def _cparams(*sem):
    return pltpu.CompilerParams(dimension_semantics=sem, vmem_limit_bytes=VMEM_LIMIT_BYTES)


def _pick(n, pref):
    if n <= pref:
        return n
    t = pref
    while n % t:
        t //= 2
    return t


def _rms_kernel(x_ref, g_ref, o_ref):
    x = x_ref[...]
    ms = jnp.mean(x * x, axis=-1, keepdims=True)
    o_ref[...] = (x * lax.rsqrt(ms + EPS) * g_ref[...]).astype(o_ref.dtype)


def rmsnorm(x, g, out_dtype):
    t, d = x.shape
    tm = _pick(t, 512)
    return pl.pallas_call(
        _rms_kernel,
        grid=(t // tm,),
        in_specs=[pl.BlockSpec((tm, d), lambda i: (i, 0)), pl.BlockSpec((1, d), lambda i: (0, 0))],
        out_specs=pl.BlockSpec((tm, d), lambda i: (i, 0)),
        out_shape=jax.ShapeDtypeStruct((t, d), out_dtype),
        compiler_params=_cparams("parallel"),
        name="rmsnorm",
    )(x, g.reshape(1, d))


def _mm_kernel(a_ref, w_ref, o_ref, wb_ref):
    @pl.when(pl.program_id(1) == 0)
    def _():
        wb_ref[...] = w_ref[...].astype(BF16)

    o_ref[...] = jnp.dot(a_ref[...].astype(BF16), wb_ref[...], preferred_element_type=F32).astype(o_ref.dtype)


def _mm_res_kernel(a_ref, w_ref, r_ref, o_ref, wb_ref, *, scale):
    @pl.when(pl.program_id(1) == 0)
    def _():
        wb_ref[...] = w_ref[...].astype(BF16)

    acc = jnp.dot(a_ref[...].astype(BF16), wb_ref[...], preferred_element_type=F32)
    o_ref[...] = r_ref[...] + scale * acc


def _mm_swiglu_kernel(a_ref, wg_ref, wu_ref, o_ref, wgb_ref, wub_ref):
    @pl.when(pl.program_id(1) == 0)
    def _():
        wgb_ref[...] = wg_ref[...].astype(BF16)
        wub_ref[...] = wu_ref[...].astype(BF16)

    a = a_ref[...].astype(BF16)
    g = jnp.dot(a, wgb_ref[...], preferred_element_type=F32)
    u = jnp.dot(a, wub_ref[...], preferred_element_type=F32)
    o_ref[...] = (g * jax.nn.sigmoid(g) * u).astype(o_ref.dtype)


def _w_spec(w, layer, k, tn, c0):
    if w.ndim == 3:
        return pl.BlockSpec((None, k, tn), lambda j, i: (layer, 0, j + c0))
    return pl.BlockSpec((k, tn), lambda j, i: (0, j + c0))


def linear(a, w, layer=0, *, col0=0, ncols=None, out_dtype=F32, tm=1024, tn=512, res=None, scale=1.0):
    m, k = a.shape
    ncols = w.shape[-1] - col0 if ncols is None else ncols
    tm = _pick(m, tm)
    tn = _pick(ncols, tn)
    assert col0 % tn == 0 and ncols % tn == 0 and m % tm == 0
    c0 = col0 // tn
    grid = (ncols // tn, m // tm)
    a_spec = pl.BlockSpec((tm, k), lambda j, i: (i, 0))
    w_spec = _w_spec(w, layer, k, tn, c0)
    o_spec = pl.BlockSpec((tm, tn), lambda j, i: (i, j))
    scratch = [pltpu.VMEM((k, tn), BF16)]
    if res is None:
        return pl.pallas_call(
            _mm_kernel, grid=grid, in_specs=[a_spec, w_spec], out_specs=o_spec,
            out_shape=jax.ShapeDtypeStruct((m, ncols), out_dtype), scratch_shapes=scratch,
            compiler_params=_cparams("parallel", "arbitrary"), name="linear",
        )(a, w)
    return pl.pallas_call(
        functools.partial(_mm_res_kernel, scale=scale), grid=grid,
        in_specs=[a_spec, w_spec, o_spec], out_specs=o_spec,
        out_shape=jax.ShapeDtypeStruct((m, ncols), F32), scratch_shapes=scratch,
        compiler_params=_cparams("parallel", "arbitrary"), name="linear_res",
    )(a, w, res)


def swiglu_in(a, w_in, layer):
    m, k = a.shape
    tm = _pick(m, 1024)
    tn = 512
    nj = D_FF // tn
    return pl.pallas_call(
        _mm_swiglu_kernel, grid=(nj, m // tm),
        in_specs=[pl.BlockSpec((tm, k), lambda j, i: (i, 0)),
                  _w_spec(w_in, layer, k, tn, 0),
                  _w_spec(w_in, layer, k, tn, nj)],
        out_specs=pl.BlockSpec((tm, tn), lambda j, i: (i, j)),
        out_shape=jax.ShapeDtypeStruct((m, D_FF), BF16),
        scratch_shapes=[pltpu.VMEM((k, tn), BF16), pltpu.VMEM((k, tn), BF16)],
        compiler_params=_cparams("parallel", "arbitrary"), name="swiglu_in",
    )(a, w_in, w_in)


def ffn(x, g, w_in, w_out, layer):
    h = rmsnorm(x, g[layer], BF16)
    hid = swiglu_in(h, w_in, layer)
    return linear(hid, w_out, layer, tm=512, tn=512, res=x, scale=0.5)


def rope_tables(pos, dh):
    rot = dh // 4
    half = rot // 2
    inv = ROPE_THETA ** (-jnp.arange(half, dtype=F32) / half)
    ang = pos.astype(F32)[:, None] * inv[None, :]
    cos, sin = jnp.cos(ang), jnp.sin(ang)
    p = pos.shape[0]
    rest1 = jnp.ones((p, dh - rot), F32)
    rest0 = jnp.zeros((p, dh - rot), F32)
    z = jnp.zeros((p, half), F32)
    c = jnp.concatenate([cos, cos, rest1], axis=1)
    up = jnp.concatenate([-sin, z, rest0], axis=1)
    dn = jnp.concatenate([z, sin, rest0], axis=1)
    rep = LANES // dh
    return tuple(jnp.tile(t, (1, rep)) for t in (c, up, dn)), half


def _rope_chunk(xc, c, up, dn, half):
    return xc * c + pltpu.roll(xc, LANES - half, 1) * up + pltpu.roll(xc, half, 1) * dn


def _cols_kernel(*refs, half, width):
    if half:
        x_ref, c_ref, u_ref, d_ref = refs[:4]
        outs = refs[4:]
        c, up, dn = c_ref[...], u_ref[...], d_ref[...]
        x = x_ref[...]
        y = jnp.concatenate(
            [_rope_chunk(x[:, k * LANES:(k + 1) * LANES], c, up, dn, half) for k in range(width // LANES)], axis=1)
    else:
        x_ref = refs[0]
        outs = refs[1:]
        y = x_ref[...]
    for o in outs:
        o[...] = y.astype(o.dtype)


def take_cols(x, col0, width, out_dtypes, tabs=None, n_pos_blocks=1):
    t = x.shape[0]
    bw = 512 if width % 512 == 0 else width
    assert col0 % bw == 0
    tables, half = tabs if tabs is not None else ((), 0)
    tm = _pick(t, 512) if tabs is None else _pick(tables[0].shape[0], 512)
    assert t % tm == 0
    npb = (tables[0].shape[0] // tm) if tabs is not None else 1
    c0 = col0 // bw
    in_specs = [pl.BlockSpec((tm, bw), lambda i, j: (i, j + c0))]
    in_specs += [pl.BlockSpec((tm, LANES), lambda i, j: (i % npb, 0)) for _ in tables]
    outs = pl.pallas_call(
        functools.partial(_cols_kernel, half=half, width=bw),
        grid=(t // tm, width // bw),
        in_specs=in_specs,
        out_specs=[pl.BlockSpec((tm, bw), lambda i, j: (i, j)) for _ in out_dtypes],
        out_shape=[jax.ShapeDtypeStruct((t, width), dt) for dt in out_dtypes],
        compiler_params=_cparams("parallel", "parallel"), name="take_cols",
    )(x, *tables)
    return outs


def _stack_heads(q, sel=None):
    parts = []
    for r in range(HEADS_PER_GROUP):
        qr = q[:, r * HD:(r + 1) * HD]
        if sel is not None:
            qr = jnp.where(sel, qr, jnp.zeros((), q.dtype))
        parts.append(qr)
    return _concat_rows(parts)


def _concat_rows(parts):
    if parts[0].shape[0] % (2 * SUBLANES):
        return jnp.concatenate([p.astype(F32) for p in parts], axis=0).astype(parts[0].dtype)
    return jnp.concatenate(parts, axis=0)


def _unstack_heads(o, tq):
    return jnp.concatenate([o[r * tq:(r + 1) * tq] for r in range(HEADS_PER_GROUP)], axis=1)


def _qk(q, k):
    return lax.dot_general(q, k, (((1,), (1,)), ((), ())), preferred_element_type=F32)


def _mask_rows(mask, s, tq):
    tk = s.shape[-1]
    return jnp.where(mask[None], s.reshape(HEADS_PER_GROUP, tq, tk), NEG_INF).reshape(HEADS_PER_GROUP * tq, tk)


def _online_update(slot, s, vt, m_sc, l_sc, acc_sc):
    m_prev = m_sc[slot]
    m_new = jnp.maximum(m_prev, jnp.max(s, axis=-1, keepdims=True))
    alpha = jnp.exp2(m_prev - m_new)
    p = jnp.exp2(s - pltpu.repeat(m_new, s.shape[-1] // LANES, axis=1))
    l_sc[slot] = alpha * l_sc[slot] + jnp.sum(p, axis=-1, keepdims=True)
    acc_sc[slot] = alpha * acc_sc[slot] + jnp.dot(p.astype(BF16), vt, preferred_element_type=F32)
    m_sc[slot] = m_new


def _init_state(m_sc, l_sc, acc_sc):
    m_sc[...] = jnp.full(m_sc.shape, M_INIT, F32)
    l_sc[...] = jnp.zeros(l_sc.shape, F32)
    acc_sc[...] = jnp.zeros(acc_sc.shape, F32)


def _order_key(x):
    b = lax.bitcast_convert_type(x, I32)
    return jnp.where(b < 0, b ^ jnp.int32(0x7FFFFFFF), b)


def _kth_largest_key(key, k):
    kf = jnp.float32(k)

    def count_ge(t):
        return jnp.sum(jnp.where(key >= t, 1.0, 0.0), axis=-1, keepdims=True)

    t0 = jnp.where(count_ge(jnp.int32(0)) >= kf, jnp.int32(0), jnp.int32(INT_MIN))

    def one_bit(bit, t):
        cand = t | (jnp.int32(1) << bit)
        return jnp.where(count_ge(cand) >= kf, cand, t)

    if key.shape[0] * key.shape[1] > 128 * 1024:
        return lax.fori_loop(0, 31, lambda it, t: one_bit(jnp.int32(30) - it, t), t0)

    def two_bits(it, t):
        hi = jnp.int32(1) << (jnp.int32(30) - 2 * it)
        lo = jnp.int32(1) << (jnp.int32(29) - 2 * it)
        c1, c2, c3 = t | lo, t | hi, t | hi | lo
        return jnp.where(count_ge(c3) >= kf, c3,
                         jnp.where(count_ge(c2) >= kf, c2, jnp.where(count_ge(c1) >= kf, c1, t)))

    return one_bit(jnp.int32(0), lax.fori_loop(0, 15, two_bits, t0))


def _n_kv_tiles(qpos0, i, tq, tk, lp):
    hi = qpos0 + (i + 1) * tq
    return jnp.minimum((hi + tk - 1) // tk, lp // tk)


def _kv_tile(lp, tq):
    return _pick(lp, 512 if tq >= 64 else 2048)


def _kv_len(k):
    return k.shape[1] if k.ndim == 3 else k.shape[2]


def _kv_spec(k, bg):
    lp = _kv_len(k)
    if k.ndim == 3:
        return pl.BlockSpec((None, lp, HD), lambda *idx: (bg(*idx)[0], 0, bg(*idx)[1]))
    return pl.BlockSpec((None, None, lp, HD), lambda *idx: (bg(*idx)[0], bg(*idx)[1], 0, 0))


def _softmax_scratch(slots, rows):
    return [pltpu.VMEM((slots, rows, LANES), F32), pltpu.VMEM((slots, rows, LANES), F32),
            pltpu.VMEM((slots, rows, HD), F32)]


def _diff_attn_kernel(lam_ref, q_ref, k_ref, v_ref, g_ref, o_ref, m_sc, l_sc, acc_sc, *, tq, tk, qpos0, lp, post):
    i = pl.program_id(2)
    q = q_ref[...]
    lane = lax.broadcasted_iota(I32, (1, HD), 1)
    lo = lane < A_HALF
    qs = (_stack_heads(q, lo), _stack_heads(q, jnp.logical_not(lo)))
    qpos = qpos0 + i * tq + lax.broadcasted_iota(I32, (tq, 1), 0)
    _init_state(m_sc, l_sc, acc_sc)
    scale = (A_HALF ** -0.5) * LOG2E

    def step(j, masked):
        off = pl.multiple_of(j * tk, tk)
        kt = k_ref[pl.ds(off, tk), :]
        vt = v_ref[pl.ds(off, tk), :]
        if masked:
            mask = off + lax.broadcasted_iota(I32, (1, tk), 1) <= qpos
        for c in range(2):
            s = _qk(qs[c], kt) * scale
            if masked:
                s = _mask_rows(mask, s, tq)
            _online_update(c, s, vt, m_sc, l_sc, acc_sc)

    n_full = jnp.minimum((qpos0 + i * tq + 1) // tk, lp // tk)
    lax.fori_loop(0, n_full, lambda j, c: (step(j, False), c)[1], 0)
    lax.fori_loop(n_full, _n_kv_tiles(qpos0, i, tq, tk, lp), lambda j, c: (step(j, True), c)[1], 0)
    o = acc_sc[0] / l_sc[0] - lam_ref[0] * (acc_sc[1] / l_sc[1])
    ms = jnp.mean(o * o, axis=-1, keepdims=True)
    o = o * lax.rsqrt(ms + EPS) * g_ref[...] * post
    o_ref[...] = _unstack_heads(o, tq).astype(o_ref.dtype)


def diff_attention(q, k, v, lam_f, subln, lam_init, qpos0):
    b, tqn, _ = q.shape
    lp = _kv_len(k)
    tq = _pick(tqn, 256)
    tk = _kv_tile(lp, tq)
    kern = functools.partial(_diff_attn_kernel, tq=tq, tk=tk, qpos0=qpos0, lp=lp, post=1.0 - lam_init)
    return pl.pallas_call(
        kern, grid=(b, GROUPS, tqn // tq),
        in_specs=[pl.BlockSpec(memory_space=pltpu.SMEM),
                  pl.BlockSpec((None, tq, GW), lambda bi, g, i: (bi, i, g)),
                  _kv_spec(k, lambda bi, g, i: (bi, g)),
                  _kv_spec(v, lambda bi, g, i: (bi, g)),
                  pl.BlockSpec((1, HD), lambda bi, g, i: (0, 0))],
        out_specs=pl.BlockSpec((None, tq, GW), lambda bi, g, i: (bi, i, g)),
        out_shape=jax.ShapeDtypeStruct(q.shape, BF16),
        scratch_shapes=_softmax_scratch(2, HEADS_PER_GROUP * tq),
        compiler_params=_cparams("parallel", "parallel", "parallel"), name="diff_attention",
    )(lam_f.reshape(1).astype(F32), q, k, v, subln.reshape(1, HD))


def _nsa_cmp_kernel(q_ref, fsk_ref, fsv_ref, b_ref, o_ref, sel_ref, *, tq, w, nblk, nch, qpos0, klanes):
    i = pl.program_id(2)
    qst = _stack_heads(q_ref[...])
    fsk = fsk_ref[...]
    fsv = fsv_ref[...]
    kcb = fsk[:, :HD] + pltpu.roll(fsk[:, HD:], w - 1, 0) + b_ref[0:1, :]
    vcb = fsv[:, :HD] + pltpu.roll(fsv[:, HD:], w - 1, 0) + b_ref[1:2, :]
    qpos = qpos0 + i * tq + lax.broadcasted_iota(I32, (tq, 1), 0)
    lanei = lax.broadcasted_iota(I32, (1, w), 1)
    s = _qk(qst, kcb.astype(BF16)) * (B_HD ** -0.5)
    valid = (lanei * CMP_STRIDE + (CMP_LEN - 1) <= qpos) & (lanei < nblk)
    s3 = jnp.where(valid[None], s.reshape(HEADS_PER_GROUP, tq, w), NEG_INF)
    e = jnp.exp(s3 - jnp.max(s3, axis=-1, keepdims=True))
    p = e / jnp.sum(e, axis=-1, keepdims=True)
    any_valid = (qpos >= CMP_LEN - 1) & (nblk > 0)
    p = jnp.where(any_valid[None], p, 0.0)
    o = jnp.dot(p.reshape(HEADS_PER_GROUP * tq, w).astype(BF16), vcb.astype(BF16), preferred_element_type=F32)
    o_ref[...] = _unstack_heads(o, tq).astype(o_ref.dtype)

    grp = p[0] + p[1] + p[2] + p[3]
    chunk = grp + jnp.where(lanei == 0, 0.0, pltpu.roll(grp, 1, 1))
    pair = jnp.where((lanei & 1) == 0, chunk + pltpu.roll(chunk, w - 1, 1), chunk + pltpu.roll(chunk, 1, 1))
    quad = jnp.where((lanei & 2) == 0, pair + pltpu.roll(pair, w - 2, 1), pair + pltpu.roll(pair, 2, 1))
    jb = lanei >> 2
    cur = qpos >> 6
    forced = (jb == 0) | (jb == cur) | (jb == cur - 1)
    score = jnp.where(forced, FORCE, jnp.where(jb * SEL_BLOCK <= qpos, quad, NEG_INF))
    score = jnp.where(lanei < nch, score, NEG_INF)
    key = _order_key(score)
    thr = _kth_largest_key(key, klanes)
    sel_ref[...] = jnp.where((key >= thr) | (lanei >= nch), 1.0, 0.0).astype(sel_ref.dtype)


def nsa_compressed(q_raw, fsk, fsv, bias, *, nblk, nch, qpos0, n_top_rep):
    b, tqn, _ = q_raw.shape
    w = fsk.shape[2]
    tq = _pick(tqn, 256)
    kern = functools.partial(_nsa_cmp_kernel, tq=tq, w=w, nblk=nblk, nch=nch, qpos0=qpos0,
                             klanes=(SEL_BLOCK // CMP_STRIDE) * n_top_rep)
    return pl.pallas_call(
        kern, grid=(b, GROUPS, tqn // tq),
        in_specs=[pl.BlockSpec((None, tq, GW), lambda bi, g, i: (bi, i, g)),
                  pl.BlockSpec((None, None, w, 2 * HD), lambda bi, g, i: (bi, g, 0, 0)),
                  pl.BlockSpec((None, None, w, 2 * HD), lambda bi, g, i: (bi, g, 0, 0)),
                  pl.BlockSpec((2, HD), lambda bi, g, i: (0, 0))],
        out_specs=[pl.BlockSpec((None, tq, GW), lambda bi, g, i: (bi, i, g)),
                   pl.BlockSpec((None, None, tq, w), lambda bi, g, i: (bi, g, i, 0))],
        out_shape=[jax.ShapeDtypeStruct(q_raw.shape, F32), jax.ShapeDtypeStruct((b, GROUPS, tqn, w), BF16)],
        compiler_params=_cparams("parallel", "parallel", "parallel"), name="nsa_compressed",
    )(q_raw, fsk, fsv, bias)


def _nsa_sw_kernel(*refs, tq, tk, tkw, qpos0, lp, lw, kwpos0, w, sel_given):
    if sel_given:
        q_ref, kw_ref, vw_ref, osel_ref, oc_ref, gate_ref, o_ref, m_sc, l_sc, acc_sc = refs
    else:
        q_ref, ks_ref, vs_ref, kw_ref, vw_ref, sel_ref, oc_ref, gate_ref, o_ref, m_sc, l_sc, acc_sc = refs
    i = pl.program_id(2)
    qst = _stack_heads(q_ref[...])
    qpos = qpos0 + i * tq + lax.broadcasted_iota(I32, (tq, 1), 0)
    _init_state(m_sc, l_sc, acc_sc)
    scale = (B_HD ** -0.5) * LOG2E
    chunks_per_tile = tk // CMP_STRIDE
    assert LANES % chunks_per_tile == 0 and w % LANES == 0

    def sel_body(j, carry):
        off = pl.multiple_of(j * tk, tk)
        kt = ks_ref[pl.ds(off, tk), :]
        vt = vs_ref[pl.ds(off, tk), :]
        kpos = off + lax.broadcasted_iota(I32, (1, tk), 1)
        cbase = pl.multiple_of((j * chunks_per_tile) // LANES * LANES, LANES)
        chunk_row = cbase + lax.broadcasted_iota(I32, (LANES, 1), 0)
        expand = jnp.where(chunk_row == (kpos >> 4), 1.0, 0.0).astype(BF16)
        chosen = jnp.dot(sel_ref[:, pl.ds(cbase, LANES)], expand, preferred_element_type=F32) > 0.5
        mask = chosen & (kpos <= qpos)
        _online_update(0, _mask_rows(mask, _qk(qst, kt) * scale, tq), vt, m_sc, l_sc, acc_sc)
        return carry

    if not sel_given:
        lax.fori_loop(0, _n_kv_tiles(qpos0, i, tq, tk, lp), sel_body, 0)

    def win_body(j, carry):
        off = pl.multiple_of(j * tkw, tkw)
        kt = kw_ref[pl.ds(off, tkw), :]
        vt = vw_ref[pl.ds(off, tkw), :]
        kpos = kwpos0 + off + lax.broadcasted_iota(I32, (1, tkw), 1)
        dt = qpos - kpos
        mask = (dt >= 0) & (dt <= WINDOW)
        _online_update(1, _mask_rows(mask, _qk(qst, kt) * scale, tq), vt, m_sc, l_sc, acc_sc)
        return carry

    first = jnp.maximum(qpos0 + i * tq - WINDOW - kwpos0, 0) // tkw
    last = jnp.minimum((qpos0 + (i + 1) * tq - kwpos0 + tkw - 1) // tkw, lw // tkw)
    lax.fori_loop(first, last, win_body, 0)

    o_sel = osel_ref[...] if sel_given else _unstack_heads(acc_sc[0] / l_sc[0], tq)
    o_win = _unstack_heads(acc_sc[1] / l_sc[1], tq)
    gsig = jax.nn.sigmoid(gate_ref[...])
    oc = oc_ref[...]
    outs = []
    for r in range(HEADS_PER_GROUP):
        cs = slice(r * HD, (r + 1) * HD)
        outs.append(oc[:, cs] * gsig[:, 3 * r:3 * r + 1] + o_sel[:, cs] * gsig[:, 3 * r + 1:3 * r + 2]
                    + o_win[:, cs] * gsig[:, 3 * r + 2:3 * r + 3])
    o_ref[...] = jnp.concatenate(outs, axis=1).astype(o_ref.dtype)


def nsa_select_window(q_rot, ks, vs, kw, vw, selmask, o_cmp, gates, *, qpos0, kwpos0, o_sel=None):
    b, tqn, _ = q_rot.shape
    sel_given = o_sel is not None
    lw = _kv_len(kw)
    lp, w = (lw, LANES) if sel_given else (_kv_len(ks), selmask.shape[3])
    tq = _pick(tqn, 256)
    tk = _kv_tile(lp, tq)
    tkw = _kv_tile(lw, tq)
    kern = functools.partial(_nsa_sw_kernel, tq=tq, tk=tk, tkw=tkw, qpos0=qpos0, lp=lp, lw=lw, kwpos0=kwpos0, w=w,
                             sel_given=sel_given)
    qspec = pl.BlockSpec((None, tq, GW), lambda bi, g, i: (bi, i, g))
    gspec = pl.BlockSpec((None, None, tq, LANES), lambda bi, g, i: (bi, g, i, 0))
    bg = lambda bi, g, i: (bi, g)
    if sel_given:
        in_specs = [qspec, _kv_spec(kw, bg), _kv_spec(vw, bg), qspec, qspec, gspec]
        args = (q_rot, kw, vw, o_sel, o_cmp, gates)
    else:
        in_specs = [qspec, _kv_spec(ks, bg), _kv_spec(vs, bg), _kv_spec(kw, bg), _kv_spec(vw, bg),
                    pl.BlockSpec((None, None, tq, w), lambda bi, g, i: (bi, g, i, 0)), qspec, gspec]
        args = (q_rot, ks, vs, kw, vw, selmask, o_cmp, gates)
    return pl.pallas_call(
        kern, grid=(b, GROUPS, tqn // tq), in_specs=in_specs, out_specs=qspec,
        out_shape=jax.ShapeDtypeStruct(q_rot.shape, BF16),
        scratch_shapes=_softmax_scratch(2, HEADS_PER_GROUP * tq),
        compiler_params=_cparams("parallel", "parallel", "parallel"), name="nsa_select_window",
    )(*args)


def _ik_kernel(t_ref, g_ref, c_ref, u_ref, d_ref, ikf_ref, ik2_ref, *, half):
    x = t_ref[...]
    lane = lax.broadcasted_iota(I32, (1, LANES), 1)
    xm = jnp.where(lane < IDX_DIM, x, 0.0)
    ms = jnp.sum(xm * xm, axis=-1, keepdims=True) * (1.0 / IDX_DIM)
    y = xm * lax.rsqrt(ms + EPS) * g_ref[...]
    y = _rope_chunk(y, c_ref[...], u_ref[...], d_ref[...], half)
    ikf_ref[...] = y
    ik2_ref[...] = (y + pltpu.roll(y, IDX_DIM, 1)).astype(ik2_ref.dtype)


def index_keys(tail, knorm, tabs):
    t = tail.shape[0]
    tables, half = tabs
    tm = _pick(tables[0].shape[0], 512)
    npb = tables[0].shape[0] // tm
    g = jnp.concatenate([knorm.astype(F32), jnp.zeros((LANES - IDX_DIM,), F32)]).reshape(1, LANES)
    rspec = pl.BlockSpec((tm, LANES), lambda i: (i, 0))
    tspec = pl.BlockSpec((tm, LANES), lambda i: (i % npb, 0))
    return pl.pallas_call(
        functools.partial(_ik_kernel, half=half), grid=(t // tm,),
        in_specs=[rspec, pl.BlockSpec((1, LANES), lambda i: (0, 0)), tspec, tspec, tspec],
        out_specs=[rspec, rspec],
        out_shape=[jax.ShapeDtypeStruct((t, LANES), F32), jax.ShapeDtypeStruct((t, LANES), BF16)],
        compiler_params=_cparams("parallel"), name="index_keys",
    )(tail, g, *tables)


def _dsa_index_kernel(iq_ref, ik_ref, tail_ref, mask_ref, sc_sc, *, tq, tk, qpos0, lp, top):
    i = pl.program_id(1)
    iq = iq_ref[...]
    lane = lax.broadcasted_iota(I32, (1, LANES), 1)
    lo = lane < IDX_DIM
    hi = jnp.logical_not(lo)
    zero = jnp.zeros((), iq.dtype)
    parts = []
    for h in range(IDX_HEADS):
        pair = iq[:, (h // 2) * LANES:(h // 2 + 1) * LANES]
        parts.append(jnp.where(lo if h % 2 == 0 else hi, pair, zero))
    iqst = _concat_rows(parts)
    iw = tail_ref[:, IDX_DIM:IDX_DIM + IDX_HEADS] * ((IDX_DIM ** -0.5) * (IDX_HEADS ** -0.5))
    qpos = qpos0 + i * tq + lax.broadcasted_iota(I32, (tq, 1), 0)
    sc_sc[...] = jnp.full(sc_sc.shape, NEG_INF, F32)

    def body(j, carry):
        off = pl.multiple_of(j * tk, tk)
        sc = _qk(iqst, ik_ref[pl.ds(off, tk), :])
        acc = jnp.zeros((tq, tk), F32)
        for h in range(IDX_HEADS):
            acc = acc + jnp.maximum(sc[h * tq:(h + 1) * tq], 0.0) * iw[:, h:h + 1]
        kpos = off + lax.broadcasted_iota(I32, (1, tk), 1)
        sc_sc[:, pl.ds(off, tk)] = jnp.where(kpos <= qpos, acc, NEG_INF)
        return carry

    lax.fori_loop(0, _n_kv_tiles(qpos0, i, tq, tk, lp), body, 0)
    key = _order_key(sc_sc[...])
    thr = _kth_largest_key(key, top)
    kpos_all = lax.broadcasted_iota(I32, (1, lp), 1)
    mask_ref[...] = jnp.where((key >= thr) & (kpos_all <= qpos), 1.0, 0.0).astype(mask_ref.dtype)


def dsa_select(iq, ik2, tail, *, qpos0, top):
    b, tqn, _ = iq.shape
    lp = ik2.shape[1]
    tq = _pick(tqn, 128)
    tk = _kv_tile(lp, tq)
    kern = functools.partial(_dsa_index_kernel, tq=tq, tk=tk, qpos0=qpos0, lp=lp, top=top)
    return pl.pallas_call(
        kern, grid=(b, tqn // tq),
        in_specs=[pl.BlockSpec((None, tq, IDX_HEADS * IDX_DIM), lambda bi, i: (bi, i, 0)),
                  pl.BlockSpec((None, lp, LANES), lambda bi, i: (bi, 0, 0)),
                  pl.BlockSpec((None, tq, LANES), lambda bi, i: (bi, i, 0))],
        out_specs=pl.BlockSpec((None, tq, lp), lambda bi, i: (bi, i, 0)),
        out_shape=jax.ShapeDtypeStruct((b, tqn, lp), BF16),
        scratch_shapes=[pltpu.VMEM((tq, lp), F32)],
        compiler_params=_cparams("parallel", "parallel"), name="dsa_select",
    )(iq, ik2, tail)


def _masked_attn_kernel(q_ref, k_ref, v_ref, mask_ref, o_ref, m_sc, l_sc, acc_sc, *, tq, tk, qpos0, lp):
    i = pl.program_id(1)
    qst = _stack_heads(q_ref[...])
    _init_state(m_sc, l_sc, acc_sc)
    scale = (C_HD ** -0.5) * LOG2E

    def body(j, carry):
        off = pl.multiple_of(j * tk, tk)
        kt = k_ref[pl.ds(off, tk), :]
        vt = v_ref[pl.ds(off, tk), :]
        mask = mask_ref[:, pl.ds(off, tk)] > 0.5
        _online_update(0, _mask_rows(mask, _qk(qst, kt) * scale, tq), vt, m_sc, l_sc, acc_sc)
        return carry

    lax.fori_loop(0, _n_kv_tiles(qpos0, i, tq, tk, lp), body, 0)
    o_ref[...] = _unstack_heads(acc_sc[0] / l_sc[0], tq).astype(o_ref.dtype)


def masked_attention(q, k, v, mask, *, qpos0):
    b, tqn, _ = q.shape
    lp = _kv_len(k)
    tq = _pick(tqn, 128)
    tk = _kv_tile(lp, tq)
    kern = functools.partial(_masked_attn_kernel, tq=tq, tk=tk, qpos0=qpos0, lp=lp)
    qspec = pl.BlockSpec((None, tq, GW), lambda bi, i, g: (bi, i, g))
    bg = lambda bi, i, g: (bi, g)
    return pl.pallas_call(
        kern, grid=(b, tqn // tq, GROUPS),
        in_specs=[qspec, _kv_spec(k, bg), _kv_spec(v, bg),
                  pl.BlockSpec((None, tq, lp), lambda bi, i, g: (bi, i, 0))],
        out_specs=qspec,
        out_shape=jax.ShapeDtypeStruct(q.shape, BF16),
        scratch_shapes=_softmax_scratch(1, HEADS_PER_GROUP * tq),
        compiler_params=_cparams("parallel", "parallel", "parallel"), name="masked_attention",
    )(q, k, v, mask)


def _cross_kernel(q_ref, k_ref, v_ref, o_ref):
    q = q_ref[...]
    k = k_ref[...].astype(BF16)
    v = v_ref[...].astype(BF16)
    outs = []
    for h in range(X_HEADS):
        cs = slice(h * X_HD, (h + 1) * X_HD)
        s = _qk(q[:, cs], k[:, cs]) * (X_HD ** -0.5)
        e = jnp.exp(s - jnp.max(s, axis=-1, keepdims=True))
        p = e / jnp.sum(e, axis=-1, keepdims=True)
        outs.append(jnp.dot(p.astype(BF16), v[:, cs], preferred_element_type=F32))
    o_ref[...] = jnp.concatenate(outs, axis=1).astype(o_ref.dtype)


def cross_attention(q, mk, mv):
    b, tqn, _ = q.shape
    tq = _pick(tqn, 512)
    ml = mk.shape[1]
    qspec = pl.BlockSpec((None, tq, X_W), lambda bi, i: (bi, i, 0))
    mspec = pl.BlockSpec((None, ml, X_W), lambda bi, i: (bi, 0, 0))
    return pl.pallas_call(
        _cross_kernel, grid=(b, tqn // tq), in_specs=[qspec, mspec, mspec], out_specs=qspec,
        out_shape=jax.ShapeDtypeStruct(q.shape, BF16),
        compiler_params=_cparams("parallel", "parallel"), name="cross_attention",
    )(q, mk, mv)


PAGES_PER_STEP = 16


def _gather_kernel(pt_ref, *refs, pps):
    pools, new_ref, o_ref = refs[:pps], refs[pps], refs[pps + 1]
    s = pl.program_id(1)
    n_steps = pl.num_programs(1) - 1

    @pl.when(s < n_steps)
    def _():
        for k in range(pps):
            x = pools[k][...].astype(o_ref.dtype)
            o_ref[k * PAGE_SIZE:(k + 1) * PAGE_SIZE, :] = jnp.concatenate([x, x], axis=1)

    @pl.when(s == n_steps)
    def _():
        o_ref[...] = jnp.zeros(o_ref.shape, o_ref.dtype)
        o_ref[0:PAGE_SIZE, :] = new_ref[...]


def gather_index_keys(pool, layer, page_table, new_rows):
    b, n_pages = page_table.shape
    pps = math.gcd(PAGES_PER_STEP, n_pages)
    n_steps = n_pages // pps
    rows = pps * PAGE_SIZE
    grid_spec = pltpu.PrefetchScalarGridSpec(
        num_scalar_prefetch=1, grid=(b, n_steps + 1),
        in_specs=_page_specs(pool, layer, pps, n_pages)
        + [pl.BlockSpec((None, PAGE_SIZE, LANES), lambda bi, s, pt: (bi, 0, 0))],
        out_specs=pl.BlockSpec((None, rows, LANES), lambda bi, s, pt: (bi, s, 0)))
    return pl.pallas_call(
        functools.partial(_gather_kernel, pps=pps), grid_spec=grid_spec,
        out_shape=jax.ShapeDtypeStruct((b, (n_steps + 1) * rows, LANES), BF16),
        compiler_params=_cparams("parallel", "arbitrary"), name="gather_index_keys",
    )(page_table, *([pool] * pps), new_rows)


def _paged_steps(pool, page_table):
    n_pages = page_table.shape[1]
    view = pool.reshape(pool.shape[0], pool.shape[1], PAGE_SIZE * GROUPS, HD)
    pps = math.gcd(PAGES_PER_STEP, n_pages)
    return view, pps, n_pages // pps


def _page_specs(view, layer, pps, n_pages):
    def spec(k):
        return pl.BlockSpec((None, None) + view.shape[2:],
                            lambda bi, s, pt: (layer, pt[bi, jnp.minimum(s * pps + k, n_pages - 1)], 0, 0))
    return [spec(k) for k in range(pps)]


def _group_rows(page_refs, g):
    return jnp.concatenate([p[pl.ds(g, PAGE_SIZE, stride=GROUPS), :].astype(BF16) for p in page_refs], axis=0)


def _paged_decode_kernel(pt_ref, *refs, pps, mode, qpos0, past_len, post):
    it = iter(refs)
    lam_ref = next(it) if mode == 'diff' else None
    q_ref = next(it)
    kpools = [next(it) for _ in range(pps)]
    vpools = [next(it) for _ in range(pps)]
    knew_ref, vnew_ref = next(it), next(it)
    mask_ref = next(it) if mode != 'diff' else None
    g_ref = next(it) if mode == 'diff' else None
    o_ref, m_sc, l_sc, acc_sc = next(it), next(it), next(it), next(it)

    step = pl.program_id(1)
    n_steps = pl.num_programs(1) - 1
    tq = q_ref.shape[0]
    n_c = 2 if mode == 'diff' else 1
    rows = pps * PAGE_SIZE
    scale = ((A_HALF if mode == 'diff' else HD) ** -0.5) * LOG2E
    qpos = qpos0 + lax.broadcasted_iota(I32, (tq, 1), 0)

    @pl.when(step == 0)
    def _():
        _init_state(m_sc, l_sc, acc_sc)

    def queries(g):
        q = q_ref[:, g * GW:(g + 1) * GW]
        if mode != 'diff':
            return (_stack_heads(q),)
        lo = lax.broadcasted_iota(I32, (1, HD), 1) < A_HALF
        return (_stack_heads(q, lo), _stack_heads(q, jnp.logical_not(lo)))

    def attend(g, kt, vt, mask):
        for c, qs in enumerate(queries(g)):
            s = _qk(qs, kt) * scale
            if mask is not None:
                s = _mask_rows(mask, s, tq)
            _online_update(g * n_c + c, s, vt, m_sc, l_sc, acc_sc)

    @pl.when(step < n_steps)
    def _():
        mask = mask_ref[...] > 0.5 if mode == 'mask' else None
        if mode == 'chunks':
            assert rows // CMP_STRIDE == LANES
            kpos = step * rows + lax.broadcasted_iota(I32, (1, rows), 1)
            chunk_row = step * LANES + lax.broadcasted_iota(I32, (LANES, 1), 0)
            expand = jnp.where(chunk_row == (kpos >> 4), 1.0, 0.0).astype(BF16)
        for g in range(GROUPS):
            if mode == 'chunks':
                mask = jnp.dot(mask_ref[g], expand, preferred_element_type=F32) > 0.5
            attend(g, _group_rows(kpools, g), _group_rows(vpools, g), mask)

    @pl.when(step == n_steps)
    def _():
        kpos = past_len + lax.broadcasted_iota(I32, (1, PAGE_SIZE), 1)
        mask = kpos <= qpos
        if mode == 'mask':
            mask = mask & (mask_ref[:, 0:PAGE_SIZE] > 0.5)
        for g in range(GROUPS):
            attend(g, knew_ref[g], vnew_ref[g], mask)
        for g in range(GROUPS):
            if mode == 'diff':
                o = acc_sc[2 * g] / l_sc[2 * g] - lam_ref[0] * (acc_sc[2 * g + 1] / l_sc[2 * g + 1])
                ms = jnp.mean(o * o, axis=-1, keepdims=True)
                o = o * lax.rsqrt(ms + EPS) * g_ref[...] * post
            else:
                o = acc_sc[g] / l_sc[g]
            o_ref[:, g * GW:(g + 1) * GW] = _unstack_heads(o, tq).astype(o_ref.dtype)


def paged_decode_attention(q, kpool, vpool, layer, page_table, k_new, v_new, *, mode, qpos0, out_dtype=BF16,
                           mask=None, lam_f=None, subln=None, lam_init=0.0):
    b, tq, _ = q.shape
    n_pages = page_table.shape[1]
    kview, pps, n_steps = _paged_steps(kpool, page_table)
    vview, _, _ = _paged_steps(vpool, page_table)
    rows = pps * PAGE_SIZE
    n_c = 2 if mode == 'diff' else 1
    in_specs, args = [], []
    if mode == 'diff':
        in_specs.append(pl.BlockSpec(memory_space=pltpu.SMEM))
        args.append(lam_f.reshape(1).astype(F32))
    in_specs.append(pl.BlockSpec((None, tq, D_MODEL), lambda bi, s, pt: (bi, 0, 0)))
    args.append(q)
    in_specs += _page_specs(kview, layer, pps, n_pages) + _page_specs(vview, layer, pps, n_pages)
    args += [kview] * pps + [vview] * pps
    new_spec = pl.BlockSpec((None, GROUPS, PAGE_SIZE, HD), lambda bi, s, pt: (bi, 0, 0, 0))
    in_specs += [new_spec, new_spec]
    args += [k_new, v_new]
    if mode == 'mask':
        in_specs.append(pl.BlockSpec((None, tq, rows), lambda bi, s, pt: (bi, 0, s)))
        args.append(mask)
    elif mode == 'chunks':
        in_specs.append(pl.BlockSpec((None, GROUPS, tq, LANES), lambda bi, s, pt: (bi, 0, 0, s)))
        args.append(mask)
    else:
        in_specs.append(pl.BlockSpec((1, HD), lambda bi, s, pt: (0, 0)))
        args.append(subln.reshape(1, HD))
    kern = functools.partial(_paged_decode_kernel, pps=pps, mode=mode, qpos0=qpos0,
                             past_len=n_pages * PAGE_SIZE, post=1.0 - lam_init)
    grid_spec = pltpu.PrefetchScalarGridSpec(
        num_scalar_prefetch=1, grid=(b, n_steps + 1), in_specs=in_specs,
        out_specs=pl.BlockSpec((None, tq, D_MODEL), lambda bi, s, pt: (bi, 0, 0)),
        scratch_shapes=_softmax_scratch(GROUPS * n_c, HEADS_PER_GROUP * tq))
    return pl.pallas_call(
        kern, grid_spec=grid_spec, out_shape=jax.ShapeDtypeStruct(q.shape, out_dtype),
        compiler_params=_cparams("parallel", "arbitrary"), name="paged_decode_attention",
    )(page_table, *args)


def _paged_compress_kernel(pt_ref, *refs, pps):
    pools, w_ref, o_ref = refs[:pps], refs[pps], refs[pps + 1]
    cpp = PAGE_SIZE // CMP_STRIDE
    acc = jnp.zeros((GROUPS * pps * cpp, 2 * HD), F32)
    for j in range(CMP_STRIDE):
        x = jnp.concatenate([p[pl.ds(GROUPS * j + g, cpp, stride=GROUPS * CMP_STRIDE), :]
                             for g in range(GROUPS) for p in pools], axis=0)
        acc = acc + jnp.dot(x.astype(BF16), w_ref[j], preferred_element_type=F32)
    for g in range(GROUPS):
        o_ref[g] = acc[g * pps * cpp:(g + 1) * pps * cpp]


def paged_compress(pool, layer, page_table, w):
    b, n_pages = page_table.shape
    view, pps, n_steps = _paged_steps(pool, page_table)
    cps = pps * (PAGE_SIZE // CMP_STRIDE)
    grid_spec = pltpu.PrefetchScalarGridSpec(
        num_scalar_prefetch=1, grid=(b, n_steps),
        in_specs=_page_specs(view, layer, pps, n_pages) + [pl.BlockSpec(w.shape, lambda bi, s, pt: (0, 0, 0))],
        out_specs=pl.BlockSpec((None, GROUPS, cps, 2 * HD), lambda bi, s, pt: (bi, 0, s, 0)))
    return pl.pallas_call(
        functools.partial(_paged_compress_kernel, pps=pps), grid_spec=grid_spec,
        out_shape=jax.ShapeDtypeStruct((b, GROUPS, n_steps * cps, 2 * HD), F32),
        compiler_params=_cparams("parallel", "parallel"), name="paged_compress",
    )(page_table, *([view] * pps), w)


def lambda_init(layer):
    return 0.8 - 0.6 * math.exp(-0.3 * layer)


def _pad_rows(x, n):
    return jnp.pad(x, ((0, 0), (0, n - x.shape[1]), (0, 0)))


def trunk(x, nb, tq_real, qpos0, mem_k, mem_v, W, past):
    t = nb * tq_real
    tqp = max(tq_real, SUBLANES)
    pos = qpos0 + jnp.arange(tq_real, dtype=I32)
    pos_rows = jnp.tile(pos, nb) if tq_real < SUBLANES else pos
    tab64 = rope_tables(pos_rows, 64)
    tab128 = rope_tables(pos_rows, 128)
    new = {}

    def to_attn(a):
        return _pad_rows(a.reshape(nb, tq_real, a.shape[-1]), tqp)

    def from_attn(a):
        return a[:, :tq_real].reshape(t, a.shape[-1])

    if past is not None:
        pt = past['page_table']
        past_len = pt.shape[1] * PAGE_SIZE

        def new_block(new_bf):
            nr = new_bf.reshape(nb, tq_real, GROUPS, HD).transpose(0, 2, 1, 3)
            return jnp.pad(nr, ((0, 0), (0, 0), (0, PAGE_SIZE - tq_real), (0, 0)))
    else:
        past_len = 0

    for l in range(DEPTH):
        x = ffn(x, W['norm_ffn1'], W['ffn1_w_in'], W['ffn1_w_out'], l)
        h = rmsnorm(x, W['norm_mix'][l], BF16)
        i = l // N_MIXERS
        kind = l % N_MIXERS
        if kind == 0:
            proj = linear(h, W['a_w_in'], i)
            (q_rot,) = take_cols(proj, 0, A_Q, (BF16,), tab64)
            k_f, k_b = take_cols(proj, A_Q, A_K, (F32, BF16), tab64)
            (v_b,) = take_cols(proj, A_Q + A_K, A_KV_HEADS * A_VDIM, (BF16,))
            v_f = proj[:, A_Q + A_K:]
            new.setdefault('a_k', []).append(k_f.reshape(nb, tq_real, A_KV_HEADS, 2 * A_HALF))
            new.setdefault('a_v', []).append(v_f.reshape(nb, tq_real, A_KV_HEADS, A_VDIM))
            lam = W['a_lambda'][i]
            lam_f = (jnp.exp(jnp.sum(lam[0] * lam[1])) - jnp.exp(jnp.sum(lam[2] * lam[3]))).astype(F32) + lambda_init(l)
            if past is None:
                o = diff_attention(to_attn(q_rot), k_b.reshape(nb, tq_real, -1), v_b.reshape(nb, tq_real, -1),
                                   lam_f, W['a_subln'][i], lambda_init(l), qpos0)
            else:
                o = paged_decode_attention(to_attn(q_rot), past['a_k'], past['a_v'], i, pt, new_block(k_b),
                                           new_block(v_b), mode='diff', qpos0=qpos0, lam_f=lam_f,
                                           subln=W['a_subln'][i], lam_init=lambda_init(l))
            y_in, w_out = from_attn(o), W['a_w_out']
        elif kind == 1:
            w_in = W['b_w_in']
            n_main = B_Q + 6 * B_KV
            proj = linear(h, w_in, i, ncols=n_main)
            w_tail = jnp.pad(w_in[i, :, n_main:], ((0, 0), (0, LANES - (w_in.shape[-1] - n_main))))
            gate_logits = linear(h, w_tail, tn=LANES)
            (q_raw,) = take_cols(proj, 0, B_Q, (BF16,))
            (q_rot,) = take_cols(proj, 0, B_Q, (BF16,), tab128)
            kc_f = proj[:, B_Q:B_Q + B_KV]
            vc_f = proj[:, B_Q + B_KV:B_Q + 2 * B_KV]
            ks_f, ks_b = take_cols(proj, B_Q + 2 * B_KV, B_KV, (F32, BF16), tab128)
            vs_f = proj[:, B_Q + 3 * B_KV:B_Q + 4 * B_KV]
            (vs_b,) = take_cols(proj, B_Q + 3 * B_KV, B_KV, (BF16,))
            kw_f, kw_b = take_cols(proj, B_Q + 4 * B_KV, B_KV, (F32, BF16), tab128)
            vw_f = proj[:, B_Q + 5 * B_KV:B_Q + 6 * B_KV]
            (vw_b,) = take_cols(proj, B_Q + 5 * B_KV, B_KV, (BF16,))
            shp = (nb, tq_real, B_KV_HEADS, B_HD)
            for nm, a in (('b_cmp_k', kc_f), ('b_cmp_v', vc_f), ('b_sel_k', ks_f), ('b_sel_v', vs_f)):
                new.setdefault(nm, []).append(a.reshape(shp))

            cw = W['b_cmp_w'][i]
            wfs = [jnp.concatenate([cw[s, :CMP_STRIDE], cw[s, CMP_STRIDE:]], axis=-1) for s in range(2)]
            if past is None:
                l_all = tq_real
                ks_all, vs_all = ks_b.reshape(nb, tq_real, -1), vs_b.reshape(nb, tq_real, -1)
                kw_all, vw_all = kw_b.reshape(nb, tq_real, -1), vw_b.reshape(nb, tq_real, -1)
                kwpos0 = 0
                keep = min(WINDOW, tq_real)
                new.setdefault('b_win_k', []).append(kw_f.reshape(shp)[:, tq_real - keep:])
                new.setdefault('b_win_v', []).append(vw_f.reshape(shp)[:, tq_real - keep:])
                lp_sel = tq_real
            else:
                l_all = past_len + tq_real
                bkw = past['b_win_k'][i].reshape(nb, -1, B_KV)
                bvw = past['b_win_v'][i].reshape(nb, -1, B_KV)
                wb = bkw.shape[1]
                kb = jnp.concatenate([bkw, kw_f.reshape(nb, tq_real, B_KV)], axis=1)
                vb = jnp.concatenate([bvw, vw_f.reshape(nb, tq_real, B_KV)], axis=1)
                new.setdefault('b_win_k', []).append(kb[:, tq_real:].reshape(nb, wb, B_KV_HEADS, B_HD))
                new.setdefault('b_win_v', []).append(vb[:, tq_real:].reshape(nb, wb, B_KV_HEADS, B_HD))
                lw = -(-(wb + tq_real) // 512) * 512
                kw_all = _pad_rows(kb, lw).astype(BF16)
                vw_all = _pad_rows(vb, lw).astype(BF16)
                kwpos0 = qpos0 - wb
                lp_sel = past_len + math.gcd(PAGES_PER_STEP, pt.shape[1]) * PAGE_SIZE
            nch = (l_all // CMP_STRIDE)
            nblk = nch - 1
            n_sel = -(-l_all // SEL_BLOCK)
            n_rep = nch // (SEL_BLOCK // CMP_STRIDE)
            assert nch % (SEL_BLOCK // CMP_STRIDE) == 0 and n_sel - n_rep in (0, 1)
            n_top_rep = min(SEL_N, n_sel) - (n_sel - n_rep)
            wch = -(-(lp_sel // CMP_STRIDE) // LANES) * LANES

            def partials(rows, wf):
                xg = rows.reshape(nb, nch, CMP_STRIDE, B_KV_HEADS, B_HD)
                xg = xg.transpose(0, 3, 1, 2, 4).reshape(nb * B_KV_HEADS * nch, CMP_STRIDE * B_HD)
                return linear(xg, wf.reshape(CMP_STRIDE * B_HD, 2 * B_HD), tn=2 * B_HD).reshape(
                    nb, B_KV_HEADS, nch, 2 * B_HD)

            if past is None:
                fsk, fsv = partials(kc_f, wfs[0]), partials(vc_f, wfs[1])
            else:
                fsk = paged_compress(past['b_cmp_k'], i, pt, wfs[0].astype(BF16))
                fsv = paged_compress(past['b_cmp_v'], i, pt, wfs[1].astype(BF16))
            fsk, fsv = (jnp.pad(f, ((0, 0), (0, 0), (0, wch - nch), (0, 0))) for f in (fsk, fsv))
            o_cmp, selmask = nsa_compressed(to_attn(q_raw), fsk, fsv, W['b_cmp_b'][i].astype(F32),
                                            nblk=nblk, nch=nch, qpos0=qpos0, n_top_rep=n_top_rep)
            gates = gate_logits[:, :3 * B_HEADS].reshape(nb, tq_real, B_KV_HEADS, 3 * HEADS_PER_GROUP)
            gates = jnp.pad(gates.transpose(0, 2, 1, 3),
                            ((0, 0), (0, 0), (0, tqp - tq_real), (0, LANES - 3 * HEADS_PER_GROUP)))
            if past is None:
                o = nsa_select_window(to_attn(q_rot), ks_all, vs_all, kw_all, vw_all, selmask, o_cmp, gates,
                                      qpos0=qpos0, kwpos0=kwpos0)
            else:
                o_sel = paged_decode_attention(to_attn(q_rot), past['b_sel_k'], past['b_sel_v'], i, pt,
                                               new_block(ks_b), new_block(vs_b), mode='chunks', qpos0=qpos0,
                                               mask=selmask, out_dtype=F32)
                o = nsa_select_window(to_attn(q_rot), None, None, kw_all, vw_all, None, o_cmp, gates,
                                      qpos0=qpos0, kwpos0=kwpos0, o_sel=o_sel)
            y_in, w_out = from_attn(o), W['b_w_out']
        else:
            w_in = W['c_w_in']
            n_main = C_Q + 2 * C_KV + IDX_HEADS * IDX_DIM
            proj = linear(h, w_in, i, ncols=n_main)
            w_tail = jnp.pad(w_in[i, :, n_main:], ((0, 0), (0, LANES - (w_in.shape[-1] - n_main))))
            tail = linear(h, w_tail, tn=LANES)
            (q_rot,) = take_cols(proj, 0, C_Q, (BF16,), tab128)
            k_f, k_b = take_cols(proj, C_Q, C_KV, (F32, BF16), tab128)
            v_f = proj[:, C_Q + C_KV:C_Q + 2 * C_KV]
            (v_b,) = take_cols(proj, C_Q + C_KV, C_KV, (BF16,))
            (iq_rot,) = take_cols(proj, C_Q + 2 * C_KV, IDX_HEADS * IDX_DIM, (BF16,), tab64)
            ik_f, ik2 = index_keys(tail, W['c_idx_knorm'][i], tab64)
            shp = (nb, tq_real, C_KV_HEADS, C_HD)
            new.setdefault('c_k', []).append(k_f.reshape(shp))
            new.setdefault('c_v', []).append(v_f.reshape(shp))
            new.setdefault('c_idx_k', []).append(ik_f[:, :IDX_DIM].reshape(nb, tq_real, IDX_DIM))
            l_all = past_len + tq_real
            top = min(IDX_TOPK_MAX, l_all // 4)
            if past is None:
                mask = dsa_select(to_attn(iq_rot), ik2.reshape(nb, tq_real, LANES), to_attn(tail), qpos0=qpos0, top=top)
                o = masked_attention(to_attn(q_rot), k_b.reshape(nb, tq_real, -1), v_b.reshape(nb, tq_real, -1),
                                     mask, qpos0=qpos0)
            else:
                ik_all = gather_index_keys(past['c_idx_k'], i, pt,
                                           _pad_rows(ik2.reshape(nb, tq_real, LANES), PAGE_SIZE))
                mask = dsa_select(to_attn(iq_rot), ik_all, to_attn(tail), qpos0=qpos0, top=top)
                o = paged_decode_attention(to_attn(q_rot), past['c_k'], past['c_v'], i, pt, new_block(k_b),
                                           new_block(v_b), mode='mask', qpos0=qpos0, mask=mask)
            y_in, w_out = from_attn(o), W['c_w_out']
        x = linear(y_in, w_out, i, res=x, scale=1.0)

        h = rmsnorm(x, W['norm_cross'][l], BF16)
        qx = linear(h, W['x_w_q'], l, out_dtype=BF16)
        ox = cross_attention(to_attn(qx), mem_k[l], mem_v[l])
        x = linear(from_attn(ox), W['x_w_o'], l, res=x, scale=1.0)
        x = ffn(x, W['norm_ffn2'], W['ffn2_w_in'], W['ffn2_w_out'], l)
    y = rmsnorm(x, W['final_norm'], F32)
    return y, {nm: jnp.stack(v) for nm, v in new.items()}


def kernel(x_prompt, x_sample, cache_a_k, cache_a_v, cache_b_cmp_k, cache_b_cmp_v, cache_b_sel_k, cache_b_sel_v, state_b_win_k, state_b_win_v, cache_c_k, cache_c_v, cache_c_idx_k, cache_mem_k, cache_mem_v, page_table, mem_prompt, norm_ffn1, norm_mix, norm_cross, norm_ffn2, final_norm, ffn1_w_in, ffn1_w_out, ffn2_w_in, ffn2_w_out, x_w_q, x_w_kv, x_w_o, a_w_in, a_w_out, a_lambda, a_subln, b_w_in, b_w_out, b_cmp_w, b_cmp_b, c_w_in, c_w_out, c_idx_knorm):
    W = dict(norm_ffn1=norm_ffn1, norm_mix=norm_mix, norm_cross=norm_cross, norm_ffn2=norm_ffn2,
             final_norm=final_norm, ffn1_w_in=ffn1_w_in, ffn1_w_out=ffn1_w_out, ffn2_w_in=ffn2_w_in,
             ffn2_w_out=ffn2_w_out, x_w_q=x_w_q, x_w_o=x_w_o, a_w_in=a_w_in, a_w_out=a_w_out,
             a_lambda=a_lambda, a_subln=a_subln, b_w_in=b_w_in, b_w_out=b_w_out, b_cmp_w=b_cmp_w,
             b_cmp_b=b_cmp_b, c_w_in=c_w_in, c_w_out=c_w_out, c_idx_knorm=c_idx_knorm)
    nbp, seq, d = x_prompt.shape
    nbs, dseq, _ = x_sample.shape
    ml = mem_prompt.shape[1]

    mem2d = mem_prompt.reshape(nbp * ml, d)
    mkv = [linear(mem2d, x_w_kv, l) for l in range(DEPTH)]
    p_mem_k = jnp.stack([m[:, :X_W].reshape(nbp, ml, X_HEADS, X_HD) for m in mkv])
    p_mem_v = jnp.stack([m[:, X_W:].reshape(nbp, ml, X_HEADS, X_HD) for m in mkv])
    y_p, ps = trunk(x_prompt.reshape(nbp * seq, d), nbp, seq, 0,
                    p_mem_k.reshape(DEPTH, nbp, ml, X_W), p_mem_v.reshape(DEPTH, nbp, ml, X_W), W, None)

    past = dict(a_k=cache_a_k, a_v=cache_a_v, b_cmp_k=cache_b_cmp_k, b_cmp_v=cache_b_cmp_v,
                b_sel_k=cache_b_sel_k, b_sel_v=cache_b_sel_v, b_win_k=state_b_win_k, b_win_v=state_b_win_v,
                c_k=cache_c_k, c_v=cache_c_v, c_idx_k=cache_c_idx_k, page_table=page_table)
    past_len = page_table.shape[1] * PAGE_SIZE
    sml = cache_mem_k.shape[2]
    y_s, ss = trunk(x_sample.reshape(nbs * dseq, d), nbs, dseq, past_len,
                    cache_mem_k.reshape(DEPTH, nbs, sml, X_W), cache_mem_v.reshape(DEPTH, nbs, sml, X_W), W, past)

    return (y_p.reshape(nbp, seq, d), y_s.reshape(nbs, dseq, d),
            ps['a_k'], ps['a_v'], ps['b_cmp_k'], ps['b_cmp_v'], ps['b_sel_k'], ps['b_sel_v'],
            ps['b_win_k'], ps['b_win_v'], ps['c_k'], ps['c_v'], ps['c_idx_k'], p_mem_k, p_mem_v,
            ss['a_k'], ss['a_v'], ss['b_cmp_k'], ss['b_cmp_v'], ss['b_sel_k'], ss['b_sel_v'],
            ss['b_win_k'], ss['b_win_v'], ss['c_k'], ss['c_v'], ss['c_idx_k'])
```

```python
import functools
import math

import jax
import jax.numpy as jnp
from jax import lax
from jax.experimental import pallas as pl
from jax.experimental.pallas import tpu as pltpu

F32 = jnp.float32
BF16 = jnp.bfloat16
I32 = jnp.int32

D_MODEL = 2048
DEPTH = 4
PAGE_SIZE = 128
N_MIXERS = 3
ROPE_THETA = 500000.0
EPS = 1e-6
NEG_INF = -1e30
FORCE = 1e9
D_FF = 256 * math.ceil(8 * D_MODEL / 3 / 256)

A_HEADS = D_MODEL // 128
A_HALF = 64
A_VDIM = 128
A_KV_HEADS = 4
A_Q = A_HEADS * 2 * A_HALF
A_K = A_KV_HEADS * 2 * A_HALF

B_HEADS = D_MODEL // 128
B_HD = 128
B_KV_HEADS = 4
B_KV = B_KV_HEADS * B_HD
B_Q = B_HEADS * B_HD
CMP_STRIDE = 16
CMP_LEN = 32
SEL_BLOCK = 64
SEL_N = 16
WINDOW = 512

C_HEADS = D_MODEL // 128
C_HD = 128
C_KV_HEADS = 4
C_Q = C_HEADS * C_HD
C_KV = C_KV_HEADS * C_HD
IDX_HEADS = 16
IDX_DIM = 64
IDX_TOPK_MAX = 256

MEM_LEN = 256
X_HEADS = 4
X_HD = 128
X_W = X_HEADS * X_HD

GROUPS = 4
HEADS_PER_GROUP = 4
HD = 128
GW = HEADS_PER_GROUP * HD

LANES = 128
SUBLANES = 8
VMEM_LIMIT_BYTES = 56 * 1024 * 1024
M_INIT = -1e29
LOG2E = 1.4426950408889634
INT_MIN = -2147483648


def _cparams(*sem):
    return pltpu.CompilerParams(dimension_semantics=sem, vmem_limit_bytes=VMEM_LIMIT_BYTES)


def _pick(n, pref):
    if n <= pref:
        return n
    t = pref
    while n % t:
        t //= 2
    return t


def _rms_kernel(x_ref, g_ref, o_ref):
    x = x_ref[...]
    ms = jnp.mean(x * x, axis=-1, keepdims=True)
    o_ref[...] = (x * lax.rsqrt(ms + EPS) * g_ref[...]).astype(o_ref.dtype)


def rmsnorm(x, g, out_dtype):
    t, d = x.shape
    tm = _pick(t, 512)
    return pl.pallas_call(
        _rms_kernel,
        grid=(t // tm,),
        in_specs=[pl.BlockSpec((tm, d), lambda i: (i, 0)), pl.BlockSpec((1, d), lambda i: (0, 0))],
        out_specs=pl.BlockSpec((tm, d), lambda i: (i, 0)),
        out_shape=jax.ShapeDtypeStruct((t, d), out_dtype),
        compiler_params=_cparams("parallel"),
        name="rmsnorm",
    )(x, g.reshape(1, d))


def _mm_kernel(a_ref, w_ref, o_ref, wb_ref):
    @pl.when(pl.program_id(1) == 0)
    def _():
        wb_ref[...] = w_ref[...].astype(BF16)

    o_ref[...] = jnp.dot(a_ref[...].astype(BF16), wb_ref[...], preferred_element_type=F32).astype(o_ref.dtype)


def _mm_res_kernel(a_ref, w_ref, r_ref, o_ref, wb_ref, *, scale):
    @pl.when(pl.program_id(1) == 0)
    def _():
        wb_ref[...] = w_ref[...].astype(BF16)

    acc = jnp.dot(a_ref[...].astype(BF16), wb_ref[...], preferred_element_type=F32)
    o_ref[...] = r_ref[...] + scale * acc


def _mm_swiglu_kernel(a_ref, wg_ref, wu_ref, o_ref, wgb_ref, wub_ref):
    @pl.when(pl.program_id(1) == 0)
    def _():
        wgb_ref[...] = wg_ref[...].astype(BF16)
        wub_ref[...] = wu_ref[...].astype(BF16)

    a = a_ref[...].astype(BF16)
    g = jnp.dot(a, wgb_ref[...], preferred_element_type=F32)
    u = jnp.dot(a, wub_ref[...], preferred_element_type=F32)
    o_ref[...] = (g * jax.nn.sigmoid(g) * u).astype(o_ref.dtype)


def _w_spec(w, layer, k, tn, c0):
    if w.ndim == 3:
        return pl.BlockSpec((None, k, tn), lambda j, i: (layer, 0, j + c0))
    return pl.BlockSpec((k, tn), lambda j, i: (0, j + c0))


def linear(a, w, layer=0, *, col0=0, ncols=None, out_dtype=F32, tm=None, tn=512, res=None, scale=1.0):
    m, k = a.shape
    ncols = w.shape[-1] - col0 if ncols is None else ncols
    tm = _pick(m, (2048 if k <= D_MODEL else 512) if tm is None else tm)
    tn = _pick(ncols, tn)
    assert col0 % tn == 0 and ncols % tn == 0 and m % tm == 0
    c0 = col0 // tn
    grid = (ncols // tn, m // tm)
    a_spec = pl.BlockSpec((tm, k), lambda j, i: (i, 0))
    w_spec = _w_spec(w, layer, k, tn, c0)
    o_spec = pl.BlockSpec((tm, tn), lambda j, i: (i, j))
    scratch = [pltpu.VMEM((k, tn), BF16)]
    if res is None:
        return pl.pallas_call(
            _mm_kernel, grid=grid, in_specs=[a_spec, w_spec], out_specs=o_spec,
            out_shape=jax.ShapeDtypeStruct((m, ncols), out_dtype), scratch_shapes=scratch,
            compiler_params=_cparams("parallel", "arbitrary"), name="linear",
        )(a, w)
    return pl.pallas_call(
        functools.partial(_mm_res_kernel, scale=scale), grid=grid,
        in_specs=[a_spec, w_spec, o_spec], out_specs=o_spec,
        out_shape=jax.ShapeDtypeStruct((m, ncols), F32), scratch_shapes=scratch,
        compiler_params=_cparams("parallel", "arbitrary"), name="linear_res",
    )(a, w, res)


def swiglu_in(a, w_in, layer):
    m, k = a.shape
    tm = _pick(m, 2048)
    tn = 512
    nj = D_FF // tn
    return pl.pallas_call(
        _mm_swiglu_kernel, grid=(nj, m // tm),
        in_specs=[pl.BlockSpec((tm, k), lambda j, i: (i, 0)),
                  _w_spec(w_in, layer, k, tn, 0),
                  _w_spec(w_in, layer, k, tn, nj)],
        out_specs=pl.BlockSpec((tm, tn), lambda j, i: (i, j)),
        out_shape=jax.ShapeDtypeStruct((m, D_FF), BF16),
        scratch_shapes=[pltpu.VMEM((k, tn), BF16), pltpu.VMEM((k, tn), BF16)],
        compiler_params=_cparams("parallel", "arbitrary"), name="swiglu_in",
    )(a, w_in, w_in)


def ffn(x, g, w_in, w_out, layer):
    h = rmsnorm(x, g[layer], BF16)
    hid = swiglu_in(h, w_in, layer)
    return linear(hid, w_out, layer, res=x, scale=0.5)


def rope_tables(pos, dh):
    rot = dh // 4
    half = rot // 2
    inv = ROPE_THETA ** (-jnp.arange(half, dtype=F32) / half)
    ang = pos.astype(F32)[:, None] * inv[None, :]
    cos, sin = jnp.cos(ang), jnp.sin(ang)
    p = pos.shape[0]
    rest1 = jnp.ones((p, dh - rot), F32)
    rest0 = jnp.zeros((p, dh - rot), F32)
    z = jnp.zeros((p, half), F32)
    c = jnp.concatenate([cos, cos, rest1], axis=1)
    up = jnp.concatenate([-sin, z, rest0], axis=1)
    dn = jnp.concatenate([z, sin, rest0], axis=1)
    rep = LANES // dh
    return tuple(jnp.tile(t, (1, rep)) for t in (c, up, dn)), half


def _rope_chunk(xc, c, up, dn, half):
    return xc * c + pltpu.roll(xc, LANES - half, 1) * up + pltpu.roll(xc, half, 1) * dn


def _cols_kernel(*refs, half, width):
    if half:
        x_ref, c_ref, u_ref, d_ref = refs[:4]
        outs = refs[4:]
        c, up, dn = c_ref[...], u_ref[...], d_ref[...]
        x = x_ref[...]
        y = jnp.concatenate(
            [_rope_chunk(x[:, k * LANES:(k + 1) * LANES], c, up, dn, half) for k in range(width // LANES)], axis=1)
    else:
        x_ref = refs[0]
        outs = refs[1:]
        y = x_ref[...]
    for o in outs:
        o[...] = y.astype(o.dtype)


def take_cols(x, col0, width, out_dtypes, tabs=None, n_pos_blocks=1):
    t = x.shape[0]
    bw = 512 if width % 512 == 0 else width
    assert col0 % bw == 0
    tables, half = tabs if tabs is not None else ((), 0)
    tm = _pick(t, 512) if tabs is None else _pick(tables[0].shape[0], 512)
    assert t % tm == 0
    npb = (tables[0].shape[0] // tm) if tabs is not None else 1
    c0 = col0 // bw
    in_specs = [pl.BlockSpec((tm, bw), lambda i, j: (i, j + c0))]
    in_specs += [pl.BlockSpec((tm, LANES), lambda i, j: (i % npb, 0)) for _ in tables]
    outs = pl.pallas_call(
        functools.partial(_cols_kernel, half=half, width=bw),
        grid=(t // tm, width // bw),
        in_specs=in_specs,
        out_specs=[pl.BlockSpec((tm, bw), lambda i, j: (i, j)) for _ in out_dtypes],
        out_shape=[jax.ShapeDtypeStruct((t, width), dt) for dt in out_dtypes],
        compiler_params=_cparams("parallel", "parallel"), name="take_cols",
    )(x, *tables)
    return outs


def _stack_heads(q, sel=None):
    parts = []
    for r in range(HEADS_PER_GROUP):
        qr = q[:, r * HD:(r + 1) * HD]
        if sel is not None:
            qr = jnp.where(sel, qr, jnp.zeros((), q.dtype))
        parts.append(qr)
    return _concat_rows(parts)


def _concat_rows(parts):
    if parts[0].shape[0] % (2 * SUBLANES):
        return jnp.concatenate([p.astype(F32) for p in parts], axis=0).astype(parts[0].dtype)
    return jnp.concatenate(parts, axis=0)


def _unstack_heads(o, tq):
    return jnp.concatenate([o[r * tq:(r + 1) * tq] for r in range(HEADS_PER_GROUP)], axis=1)


def _qk(q, k):
    return lax.dot_general(q, k, (((1,), (1,)), ((), ())), preferred_element_type=F32)


def _mask_rows(mask, s, tq):
    tk = s.shape[-1]
    return jnp.where(mask[None], s.reshape(-1, tq, tk), NEG_INF).reshape(s.shape)


def _online_update(slot, s, vt, m_sc, l_sc, acc_sc):
    m_prev = m_sc[slot]
    m_new = jnp.maximum(m_prev, jnp.max(s, axis=-1, keepdims=True))
    alpha = jnp.exp2(m_prev - m_new)
    p = jnp.exp2(s - jnp.concatenate([m_new] * (s.shape[-1] // LANES), axis=1))
    l_sc[slot] = alpha * l_sc[slot] + jnp.sum(p, axis=-1, keepdims=True)
    acc_sc[slot] = alpha * acc_sc[slot] + jnp.dot(p.astype(BF16), vt, preferred_element_type=F32)
    m_sc[slot] = m_new


def _init_state(m_sc, l_sc, acc_sc):
    m_sc[...] = jnp.full(m_sc.shape, M_INIT, F32)
    l_sc[...] = jnp.zeros(l_sc.shape, F32)
    acc_sc[...] = jnp.zeros(acc_sc.shape, F32)


def _order_key(x):
    b = lax.bitcast_convert_type(x, I32)
    return jnp.where(b < 0, b ^ jnp.int32(0x7FFFFFFF), b)


def _kth_largest_key(key, k):
    kf = jnp.float32(k)

    def count_ge(t):
        return jnp.sum(jnp.where(key >= t, 1.0, 0.0), axis=-1, keepdims=True)

    t0 = jnp.where(count_ge(jnp.int32(0)) >= kf, jnp.int32(0), jnp.int32(INT_MIN))

    def one_bit(bit, t):
        cand = t | (jnp.int32(1) << bit)
        return jnp.where(count_ge(cand) >= kf, cand, t)

    if key.shape[0] * key.shape[1] > 128 * 1024:
        return lax.fori_loop(0, 31, lambda it, t: one_bit(jnp.int32(30) - it, t), t0)

    def two_bits(it, t):
        hi = jnp.int32(1) << (jnp.int32(30) - 2 * it)
        lo = jnp.int32(1) << (jnp.int32(29) - 2 * it)
        c1, c2, c3 = t | lo, t | hi, t | hi | lo
        return jnp.where(count_ge(c3) >= kf, c3,
                         jnp.where(count_ge(c2) >= kf, c2, jnp.where(count_ge(c1) >= kf, c1, t)))

    return one_bit(jnp.int32(0), lax.fori_loop(0, 15, two_bits, t0))


def _n_kv_tiles(qpos0, i, tq, tk, lp):
    hi = qpos0 + (i + 1) * tq
    return jnp.minimum((hi + tk - 1) // tk, lp // tk)


def _kv_tile(lp, tq):
    return _pick(lp, 512 if tq >= 64 else 2048)


def _kv_len(k):
    return k.shape[1] if k.ndim == 3 else k.shape[2]


def _kv_spec(k, bg):
    lp = _kv_len(k)
    if k.ndim == 3:
        return pl.BlockSpec((None, lp, HD), lambda *idx: (bg(*idx)[0], 0, bg(*idx)[1]))
    return pl.BlockSpec((None, None, lp, HD), lambda *idx: (bg(*idx)[0], bg(*idx)[1], 0, 0))


def _softmax_scratch(slots, rows):
    return [pltpu.VMEM((slots, rows, LANES), F32), pltpu.VMEM((slots, rows, LANES), F32),
            pltpu.VMEM((slots, rows, HD), F32)]


def _diff_attn_kernel(lam_ref, q_ref, k_ref, v_ref, g_ref, o_ref, m_sc, l_sc, acc_sc, *, tq, tk, qpos0, lp, post):
    i = pl.program_id(2)
    q = q_ref[...]
    lane = lax.broadcasted_iota(I32, (1, HD), 1)
    lo = lane < A_HALF
    qs = _concat_rows([_stack_heads(q, lo), _stack_heads(q, jnp.logical_not(lo))])
    qpos = qpos0 + i * tq + lax.broadcasted_iota(I32, (tq, 1), 0)
    _init_state(m_sc, l_sc, acc_sc)
    scale = (A_HALF ** -0.5) * LOG2E
    half = HEADS_PER_GROUP * tq

    def step(j, masked):
        off = pl.multiple_of(j * tk, tk)
        s = _qk(qs, k_ref[pl.ds(off, tk), :]) * scale
        if masked:
            s = _mask_rows(off + lax.broadcasted_iota(I32, (1, tk), 1) <= qpos, s, tq)
        _online_update(0, s, v_ref[pl.ds(off, tk), :], m_sc, l_sc, acc_sc)

    n_full = jnp.minimum((qpos0 + i * tq + 1) // tk, lp // tk)
    lax.fori_loop(0, n_full, lambda j, c: (step(j, False), c)[1], 0)
    lax.fori_loop(n_full, _n_kv_tiles(qpos0, i, tq, tk, lp), lambda j, c: (step(j, True), c)[1], 0)
    on = acc_sc[0] / l_sc[0]
    o = on[:half] - lam_ref[0] * on[half:]
    ms = jnp.mean(o * o, axis=-1, keepdims=True)
    o = o * lax.rsqrt(ms + EPS) * g_ref[...] * post
    o_ref[...] = _unstack_heads(o, tq).astype(o_ref.dtype)


def diff_attention(q, k, v, lam_f, subln, lam_init, qpos0):
    b, tqn, _ = q.shape
    lp = _kv_len(k)
    tq = _pick(tqn, 256)
    tk = _kv_tile(lp, tq)
    kern = functools.partial(_diff_attn_kernel, tq=tq, tk=tk, qpos0=qpos0, lp=lp, post=1.0 - lam_init)
    return pl.pallas_call(
        kern, grid=(b, GROUPS, tqn // tq),
        in_specs=[pl.BlockSpec(memory_space=pltpu.SMEM),
                  pl.BlockSpec((None, tq, GW), lambda bi, g, i: (bi, i, g)),
                  _kv_spec(k, lambda bi, g, i: (bi, g)),
                  _kv_spec(v, lambda bi, g, i: (bi, g)),
                  pl.BlockSpec((1, HD), lambda bi, g, i: (0, 0))],
        out_specs=pl.BlockSpec((None, tq, GW), lambda bi, g, i: (bi, i, g)),
        out_shape=jax.ShapeDtypeStruct(q.shape, BF16),
        scratch_shapes=_softmax_scratch(1, 2 * HEADS_PER_GROUP * tq),
        compiler_params=_cparams("parallel", "parallel", "parallel"), name="diff_attention",
    )(lam_f.reshape(1).astype(F32), q, k, v, subln.reshape(1, HD))


def _nsa_cmp_kernel(q_ref, fsk_ref, fsv_ref, b_ref, o_ref, sel_ref, *, tq, w, nblk, nch, qpos0, klanes):
    i = pl.program_id(2)
    qst = _stack_heads(q_ref[...])
    fsk = fsk_ref[...]
    fsv = fsv_ref[...]
    kcb = fsk[:, :HD] + pltpu.roll(fsk[:, HD:], w - 1, 0) + b_ref[0:1, :]
    vcb = fsv[:, :HD] + pltpu.roll(fsv[:, HD:], w - 1, 0) + b_ref[1:2, :]
    qpos = qpos0 + i * tq + lax.broadcasted_iota(I32, (tq, 1), 0)
    lanei = lax.broadcasted_iota(I32, (1, w), 1)
    s = _qk(qst, kcb.astype(BF16)) * (B_HD ** -0.5)
    valid = (lanei * CMP_STRIDE + (CMP_LEN - 1) <= qpos) & (lanei < nblk)
    s3 = jnp.where(valid[None], s.reshape(HEADS_PER_GROUP, tq, w), NEG_INF)
    e = jnp.exp(s3 - jnp.max(s3, axis=-1, keepdims=True))
    p = e / jnp.sum(e, axis=-1, keepdims=True)
    any_valid = (qpos >= CMP_LEN - 1) & (nblk > 0)
    p = jnp.where(any_valid[None], p, 0.0)
    o = jnp.dot(p.reshape(HEADS_PER_GROUP * tq, w).astype(BF16), vcb.astype(BF16), preferred_element_type=F32)
    o_ref[...] = _unstack_heads(o, tq).astype(o_ref.dtype)

    grp = p[0] + p[1] + p[2] + p[3]
    chunk = grp + jnp.where(lanei == 0, 0.0, pltpu.roll(grp, 1, 1))
    pair = jnp.where((lanei & 1) == 0, chunk + pltpu.roll(chunk, w - 1, 1), chunk + pltpu.roll(chunk, 1, 1))
    quad = jnp.where((lanei & 2) == 0, pair + pltpu.roll(pair, w - 2, 1), pair + pltpu.roll(pair, 2, 1))
    jb = lanei >> 2
    cur = qpos >> 6
    forced = (jb == 0) | (jb == cur) | (jb == cur - 1)
    score = jnp.where(forced, FORCE, jnp.where(jb * SEL_BLOCK <= qpos, quad, NEG_INF))
    score = jnp.where(lanei < nch, score, NEG_INF)
    key = _order_key(score)
    thr = _kth_largest_key(key, klanes)
    sel_ref[...] = jnp.where((key >= thr) | (lanei >= nch), 1.0, 0.0).astype(sel_ref.dtype)


def nsa_compressed(q_raw, fsk, fsv, bias, *, nblk, nch, qpos0, n_top_rep):
    b, tqn, _ = q_raw.shape
    w = fsk.shape[2]
    tq = _pick(tqn, 256)
    kern = functools.partial(_nsa_cmp_kernel, tq=tq, w=w, nblk=nblk, nch=nch, qpos0=qpos0,
                             klanes=(SEL_BLOCK // CMP_STRIDE) * n_top_rep)
    return pl.pallas_call(
        kern, grid=(b, GROUPS, tqn // tq),
        in_specs=[pl.BlockSpec((None, tq, GW), lambda bi, g, i: (bi, i, g)),
                  pl.BlockSpec((None, None, w, 2 * HD), lambda bi, g, i: (bi, g, 0, 0)),
                  pl.BlockSpec((None, None, w, 2 * HD), lambda bi, g, i: (bi, g, 0, 0)),
                  pl.BlockSpec((2, HD), lambda bi, g, i: (0, 0))],
        out_specs=[pl.BlockSpec((None, tq, GW), lambda bi, g, i: (bi, i, g)),
                   pl.BlockSpec((None, None, tq, w), lambda bi, g, i: (bi, g, i, 0))],
        out_shape=[jax.ShapeDtypeStruct(q_raw.shape, F32), jax.ShapeDtypeStruct((b, GROUPS, tqn, w), BF16)],
        compiler_params=_cparams("parallel", "parallel", "parallel"), name="nsa_compressed",
    )(q_raw, fsk, fsv, bias)


def _nsa_sw_kernel(*refs, tq, tk, tkw, qpos0, lp, lw, kwpos0, w, sel_given):
    if sel_given:
        q_ref, kw_ref, vw_ref, osel_ref, oc_ref, gate_ref, o_ref, m_sc, l_sc, acc_sc = refs
    else:
        q_ref, ks_ref, vs_ref, kw_ref, vw_ref, sel_ref, oc_ref, gate_ref, o_ref, m_sc, l_sc, acc_sc = refs
    i = pl.program_id(2)
    qst = _stack_heads(q_ref[...])
    qpos = qpos0 + i * tq + lax.broadcasted_iota(I32, (tq, 1), 0)
    _init_state(m_sc, l_sc, acc_sc)
    scale = (B_HD ** -0.5) * LOG2E
    chunks_per_tile = tk // CMP_STRIDE
    assert LANES % chunks_per_tile == 0 and w % LANES == 0

    def sel_body(j, carry):
        off = pl.multiple_of(j * tk, tk)
        kt = ks_ref[pl.ds(off, tk), :]
        vt = vs_ref[pl.ds(off, tk), :]
        kpos = off + lax.broadcasted_iota(I32, (1, tk), 1)
        cbase = pl.multiple_of((j * chunks_per_tile) // LANES * LANES, LANES)
        chunk_row = cbase + lax.broadcasted_iota(I32, (LANES, 1), 0)
        expand = jnp.where(chunk_row == (kpos >> 4), 1.0, 0.0).astype(BF16)
        chosen = jnp.dot(sel_ref[:, pl.ds(cbase, LANES)], expand, preferred_element_type=F32) > 0.5
        mask = chosen & (kpos <= qpos)
        _online_update(0, _mask_rows(mask, _qk(qst, kt) * scale, tq), vt, m_sc, l_sc, acc_sc)
        return carry

    if not sel_given:
        lax.fori_loop(0, _n_kv_tiles(qpos0, i, tq, tk, lp), sel_body, 0)

    def win_body(j, carry):
        off = pl.multiple_of(j * tkw, tkw)
        kt = kw_ref[pl.ds(off, tkw), :]
        vt = vw_ref[pl.ds(off, tkw), :]
        kpos = kwpos0 + off + lax.broadcasted_iota(I32, (1, tkw), 1)
        dt = qpos - kpos
        mask = (dt >= 0) & (dt <= WINDOW)
        _online_update(1, _mask_rows(mask, _qk(qst, kt) * scale, tq), vt, m_sc, l_sc, acc_sc)
        return carry

    first = jnp.maximum(qpos0 + i * tq - WINDOW - kwpos0, 0) // tkw
    last = jnp.minimum((qpos0 + (i + 1) * tq - kwpos0 + tkw - 1) // tkw, lw // tkw)
    lax.fori_loop(first, last, win_body, 0)

    o_sel = osel_ref[...] if sel_given else _unstack_heads(acc_sc[0] / l_sc[0], tq)
    o_win = _unstack_heads(acc_sc[1] / l_sc[1], tq)
    gsig = jax.nn.sigmoid(gate_ref[...])
    oc = oc_ref[...]
    outs = []
    for r in range(HEADS_PER_GROUP):
        cs = slice(r * HD, (r + 1) * HD)
        outs.append(oc[:, cs] * gsig[:, 3 * r:3 * r + 1] + o_sel[:, cs] * gsig[:, 3 * r + 1:3 * r + 2]
                    + o_win[:, cs] * gsig[:, 3 * r + 2:3 * r + 3])
    o_ref[...] = jnp.concatenate(outs, axis=1).astype(o_ref.dtype)


def nsa_select_window(q_rot, ks, vs, kw, vw, selmask, o_cmp, gates, *, qpos0, kwpos0, o_sel=None):
    b, tqn, _ = q_rot.shape
    sel_given = o_sel is not None
    lw = _kv_len(kw)
    lp, w = (lw, LANES) if sel_given else (_kv_len(ks), selmask.shape[3])
    tq = _pick(tqn, 256)
    tk = _kv_tile(lp, tq)
    tkw = _kv_tile(lw, tq)
    kern = functools.partial(_nsa_sw_kernel, tq=tq, tk=tk, tkw=tkw, qpos0=qpos0, lp=lp, lw=lw, kwpos0=kwpos0, w=w,
                             sel_given=sel_given)
    qspec = pl.BlockSpec((None, tq, GW), lambda bi, g, i: (bi, i, g))
    gspec = pl.BlockSpec((None, None, tq, LANES), lambda bi, g, i: (bi, g, i, 0))
    bg = lambda bi, g, i: (bi, g)
    if sel_given:
        in_specs = [qspec, _kv_spec(kw, bg), _kv_spec(vw, bg), qspec, qspec, gspec]
        args = (q_rot, kw, vw, o_sel, o_cmp, gates)
    else:
        in_specs = [qspec, _kv_spec(ks, bg), _kv_spec(vs, bg), _kv_spec(kw, bg), _kv_spec(vw, bg),
                    pl.BlockSpec((None, None, tq, w), lambda bi, g, i: (bi, g, i, 0)), qspec, gspec]
        args = (q_rot, ks, vs, kw, vw, selmask, o_cmp, gates)
    return pl.pallas_call(
        kern, grid=(b, GROUPS, tqn // tq), in_specs=in_specs, out_specs=qspec,
        out_shape=jax.ShapeDtypeStruct(q_rot.shape, BF16),
        scratch_shapes=_softmax_scratch(2, HEADS_PER_GROUP * tq),
        compiler_params=_cparams("parallel", "parallel", "parallel"), name="nsa_select_window",
    )(*args)


def _ik_kernel(t_ref, g_ref, c_ref, u_ref, d_ref, ikf_ref, ik2_ref, *, half):
    x = t_ref[...]
    lane = lax.broadcasted_iota(I32, (1, LANES), 1)
    xm = jnp.where(lane < IDX_DIM, x, 0.0)
    ms = jnp.sum(xm * xm, axis=-1, keepdims=True) * (1.0 / IDX_DIM)
    y = xm * lax.rsqrt(ms + EPS) * g_ref[...]
    y = _rope_chunk(y, c_ref[...], u_ref[...], d_ref[...], half)
    ikf_ref[...] = y
    ik2_ref[...] = (y + pltpu.roll(y, IDX_DIM, 1)).astype(ik2_ref.dtype)


def index_keys(tail, knorm, tabs):
    t = tail.shape[0]
    tables, half = tabs
    tm = _pick(tables[0].shape[0], 512)
    npb = tables[0].shape[0] // tm
    g = jnp.concatenate([knorm.astype(F32), jnp.zeros((LANES - IDX_DIM,), F32)]).reshape(1, LANES)
    rspec = pl.BlockSpec((tm, LANES), lambda i: (i, 0))
    tspec = pl.BlockSpec((tm, LANES), lambda i: (i % npb, 0))
    return pl.pallas_call(
        functools.partial(_ik_kernel, half=half), grid=(t // tm,),
        in_specs=[rspec, pl.BlockSpec((1, LANES), lambda i: (0, 0)), tspec, tspec, tspec],
        out_specs=[rspec, rspec],
        out_shape=[jax.ShapeDtypeStruct((t, LANES), F32), jax.ShapeDtypeStruct((t, LANES), BF16)],
        compiler_params=_cparams("parallel"), name="index_keys",
    )(tail, g, *tables)


def _dsa_index_kernel(iq_ref, ik_ref, tail_ref, mask_ref, sc_sc, *, tq, tk, qpos0, lp, top, n_widths):
    i = pl.program_id(1)
    iq = iq_ref[...]
    lane = lax.broadcasted_iota(I32, (1, LANES), 1)
    lo = lane < IDX_DIM
    hi = jnp.logical_not(lo)
    zero = jnp.zeros((), iq.dtype)
    parts = []
    for h in range(IDX_HEADS):
        pair = iq[:, (h // 2) * LANES:(h // 2 + 1) * LANES]
        parts.append(jnp.where(lo if h % 2 == 0 else hi, pair, zero))
    iqst = _concat_rows(parts)
    iw = tail_ref[:, IDX_DIM:IDX_DIM + IDX_HEADS] * ((IDX_DIM ** -0.5) * (IDX_HEADS ** -0.5))
    qpos = qpos0 + i * tq + lax.broadcasted_iota(I32, (tq, 1), 0)
    sc_sc[...] = jnp.full(sc_sc.shape, NEG_INF, F32)

    def body(j, carry):
        off = pl.multiple_of(j * tk, tk)
        sc = _qk(iqst, ik_ref[pl.ds(off, tk), :])
        acc = jnp.zeros((tq, tk), F32)
        for h in range(IDX_HEADS):
            acc = acc + jnp.maximum(sc[h * tq:(h + 1) * tq], 0.0) * iw[:, h:h + 1]
        kpos = off + lax.broadcasted_iota(I32, (1, tk), 1)
        sc_sc[:, pl.ds(off, tk)] = jnp.where(kpos <= qpos, acc, NEG_INF)
        return carry

    lax.fori_loop(0, _n_kv_tiles(qpos0, i, tq, tk, lp), body, 0)
    key = _order_key(sc_sc[...])
    quarter = lp // 4
    if n_widths > 1 and quarter % LANES == 0 and quarter > top:
        reach = jnp.clip((qpos0 + (i + 1) * tq + quarter - 1) // quarter, 1, 4)
        thr = lax.switch(reach - 1, [functools.partial(_kth_largest_key, key[:, :quarter * n], top)
                                     for n in range(1, 5)])
    else:
        thr = _kth_largest_key(key, top)
    kpos_all = lax.broadcasted_iota(I32, (1, lp), 1)
    mask_ref[...] = jnp.where((key >= thr) & (kpos_all <= qpos), 1.0, 0.0).astype(mask_ref.dtype)


def dsa_select(iq, ik2, tail, *, qpos0, top):
    b, tqn, _ = iq.shape
    lp = ik2.shape[1]
    tq = _pick(tqn, 128)
    tk = _kv_tile(lp, tq)
    kern = functools.partial(_dsa_index_kernel, tq=tq, tk=tk, qpos0=qpos0, lp=lp, top=top,
                             n_widths=4 if tqn > tq else 1)
    return pl.pallas_call(
        kern, grid=(b, tqn // tq),
        in_specs=[pl.BlockSpec((None, tq, IDX_HEADS * IDX_DIM), lambda bi, i: (bi, i, 0)),
                  pl.BlockSpec((None, lp, LANES), lambda bi, i: (bi, 0, 0)),
                  pl.BlockSpec((None, tq, LANES), lambda bi, i: (bi, i, 0))],
        out_specs=pl.BlockSpec((None, tq, lp), lambda bi, i: (bi, i, 0)),
        out_shape=jax.ShapeDtypeStruct((b, tqn, lp), BF16),
        scratch_shapes=[pltpu.VMEM((tq, lp), F32)],
        compiler_params=_cparams("parallel", "parallel"), name="dsa_select",
    )(iq, ik2, tail)


def _masked_attn_kernel(q_ref, k_ref, v_ref, mask_ref, o_ref, m_sc, l_sc, acc_sc, *, tq, tk, qpos0, lp):
    i = pl.program_id(1)
    qst = _stack_heads(q_ref[...])
    _init_state(m_sc, l_sc, acc_sc)
    scale = (C_HD ** -0.5) * LOG2E

    def body(j, carry):
        off = pl.multiple_of(j * tk, tk)
        kt = k_ref[pl.ds(off, tk), :]
        vt = v_ref[pl.ds(off, tk), :]
        mask = mask_ref[:, pl.ds(off, tk)] > 0.5
        _online_update(0, _mask_rows(mask, _qk(qst, kt) * scale, tq), vt, m_sc, l_sc, acc_sc)
        return carry

    lax.fori_loop(0, _n_kv_tiles(qpos0, i, tq, tk, lp), body, 0)
    o_ref[...] = _unstack_heads(acc_sc[0] / l_sc[0], tq).astype(o_ref.dtype)


def masked_attention(q, k, v, mask, *, qpos0):
    b, tqn, _ = q.shape
    lp = _kv_len(k)
    tq = _pick(tqn, 256)
    tk = _kv_tile(lp, tq)
    kern = functools.partial(_masked_attn_kernel, tq=tq, tk=tk, qpos0=qpos0, lp=lp)
    qspec = pl.BlockSpec((None, tq, GW), lambda bi, i, g: (bi, i, g))
    bg = lambda bi, i, g: (bi, g)
    return pl.pallas_call(
        kern, grid=(b, tqn // tq, GROUPS),
        in_specs=[qspec, _kv_spec(k, bg), _kv_spec(v, bg),
                  pl.BlockSpec((None, tq, lp), lambda bi, i, g: (bi, i, 0))],
        out_specs=qspec,
        out_shape=jax.ShapeDtypeStruct(q.shape, BF16),
        scratch_shapes=_softmax_scratch(1, HEADS_PER_GROUP * tq),
        compiler_params=_cparams("parallel", "parallel", "parallel"), name="masked_attention",
    )(q, k, v, mask)


def _cross_kernel(q_ref, k_ref, v_ref, o_ref):
    q = q_ref[...]
    k = k_ref[...].astype(BF16)
    v = v_ref[...].astype(BF16)
    outs = []
    for h in range(X_HEADS):
        cs = slice(h * X_HD, (h + 1) * X_HD)
        s = _qk(q[:, cs], k[:, cs]) * (X_HD ** -0.5)
        e = jnp.exp(s - jnp.max(s, axis=-1, keepdims=True))
        p = e / jnp.sum(e, axis=-1, keepdims=True)
        outs.append(jnp.dot(p.astype(BF16), v[:, cs], preferred_element_type=F32))
    o_ref[...] = jnp.concatenate(outs, axis=1).astype(o_ref.dtype)


def cross_attention(q, mk, mv):
    b, tqn, _ = q.shape
    tq = _pick(tqn, 512)
    ml = mk.shape[1]
    qspec = pl.BlockSpec((None, tq, X_W), lambda bi, i: (bi, i, 0))
    mspec = pl.BlockSpec((None, ml, X_W), lambda bi, i: (bi, 0, 0))
    return pl.pallas_call(
        _cross_kernel, grid=(b, tqn // tq), in_specs=[qspec, mspec, mspec], out_specs=qspec,
        out_shape=jax.ShapeDtypeStruct(q.shape, BF16),
        compiler_params=_cparams("parallel", "parallel"), name="cross_attention",
    )(q, mk, mv)


PAGES_PER_STEP = 16


def _gather_kernel(pt_ref, *refs, pps):
    pools, new_ref, o_ref = refs[:pps], refs[pps], refs[pps + 1]
    s = pl.program_id(1)
    n_steps = pl.num_programs(1) - 1

    @pl.when(s < n_steps)
    def _():
        for k in range(pps):
            x = pools[k][...].astype(o_ref.dtype)
            o_ref[k * PAGE_SIZE:(k + 1) * PAGE_SIZE, :] = jnp.concatenate([x, x], axis=1)

    @pl.when(s == n_steps)
    def _():
        o_ref[...] = jnp.zeros(o_ref.shape, o_ref.dtype)
        o_ref[0:PAGE_SIZE, :] = new_ref[...]


def gather_index_keys(pool, layer, page_table, new_rows):
    b, n_pages = page_table.shape
    pps = math.gcd(PAGES_PER_STEP, n_pages)
    n_steps = n_pages // pps
    rows = pps * PAGE_SIZE
    grid_spec = pltpu.PrefetchScalarGridSpec(
        num_scalar_prefetch=1, grid=(b, n_steps + 1),
        in_specs=_page_specs(pool, layer, pps, n_pages)
        + [pl.BlockSpec((None, PAGE_SIZE, LANES), lambda bi, s, pt: (bi, 0, 0))],
        out_specs=pl.BlockSpec((None, rows, LANES), lambda bi, s, pt: (bi, s, 0)))
    return pl.pallas_call(
        functools.partial(_gather_kernel, pps=pps), grid_spec=grid_spec,
        out_shape=jax.ShapeDtypeStruct((b, (n_steps + 1) * rows, LANES), BF16),
        compiler_params=_cparams("parallel", "arbitrary"), name="gather_index_keys",
    )(page_table, *([pool] * pps), new_rows)


def _paged_steps(pool, page_table):
    n_pages = page_table.shape[1]
    view = pool.reshape(pool.shape[0], pool.shape[1], PAGE_SIZE * GROUPS, HD)
    pps = math.gcd(PAGES_PER_STEP, n_pages)
    return view, pps, n_pages // pps


def _page_specs(view, layer, pps, n_pages):
    def spec(k):
        return pl.BlockSpec((None, None) + view.shape[2:],
                            lambda bi, s, pt: (layer, pt[bi, jnp.minimum(s * pps + k, n_pages - 1)], 0, 0))
    return [spec(k) for k in range(pps)]


def _group_rows(page_refs, g):
    return jnp.concatenate([p[pl.ds(g, PAGE_SIZE, stride=GROUPS), :].astype(BF16) for p in page_refs], axis=0)


def _paged_decode_kernel(pt_ref, *refs, pps, mode, qpos0, past_len, post):
    it = iter(refs)
    lam_ref = next(it) if mode == 'diff' else None
    q_ref = next(it)
    kpools = [next(it) for _ in range(pps)]
    vpools = [next(it) for _ in range(pps)]
    knew_ref, vnew_ref = next(it), next(it)
    mask_ref = next(it) if mode != 'diff' else None
    g_ref = next(it) if mode == 'diff' else None
    o_ref, m_sc, l_sc, acc_sc = next(it), next(it), next(it), next(it)

    step = pl.program_id(1)
    n_steps = pl.num_programs(1) - 1
    tq = q_ref.shape[0]
    rows = pps * PAGE_SIZE
    scale = ((A_HALF if mode == 'diff' else HD) ** -0.5) * LOG2E
    qpos = qpos0 + lax.broadcasted_iota(I32, (tq, 1), 0)

    @pl.when(step == 0)
    def _():
        _init_state(m_sc, l_sc, acc_sc)

    def queries(g):
        q = q_ref[:, g * GW:(g + 1) * GW]
        if mode != 'diff':
            return _stack_heads(q)
        lo = lax.broadcasted_iota(I32, (1, HD), 1) < A_HALF
        return _concat_rows([_stack_heads(q, lo), _stack_heads(q, jnp.logical_not(lo))])

    def attend(g, kt, vt, mask):
        s = _qk(queries(g), kt) * scale
        if mask is not None:
            s = _mask_rows(mask, s, tq)
        _online_update(g, s, vt, m_sc, l_sc, acc_sc)

    @pl.when(step < n_steps)
    def _():
        mask = mask_ref[...] > 0.5 if mode == 'mask' else None
        if mode == 'chunks':
            assert rows // CMP_STRIDE == LANES
            kpos = step * rows + lax.broadcasted_iota(I32, (1, rows), 1)
            chunk_row = step * LANES + lax.broadcasted_iota(I32, (LANES, 1), 0)
            expand = jnp.where(chunk_row == (kpos >> 4), 1.0, 0.0).astype(BF16)
        for g in range(GROUPS):
            if mode == 'chunks':
                mask = jnp.dot(mask_ref[g], expand, preferred_element_type=F32) > 0.5
            attend(g, _group_rows(kpools, g), _group_rows(vpools, g), mask)

    @pl.when(step == n_steps)
    def _():
        kpos = past_len + lax.broadcasted_iota(I32, (1, PAGE_SIZE), 1)
        mask = kpos <= qpos
        if mode == 'mask':
            mask = mask & (mask_ref[:, 0:PAGE_SIZE] > 0.5)
        for g in range(GROUPS):
            attend(g, knew_ref[g], vnew_ref[g], mask)
        for g in range(GROUPS):
            if mode == 'diff':
                on = acc_sc[g] / l_sc[g]
                o = on[:HEADS_PER_GROUP * tq] - lam_ref[0] * on[HEADS_PER_GROUP * tq:]
                ms = jnp.mean(o * o, axis=-1, keepdims=True)
                o = o * lax.rsqrt(ms + EPS) * g_ref[...] * post
            else:
                o = acc_sc[g] / l_sc[g]
            o_ref[:, g * GW:(g + 1) * GW] = _unstack_heads(o, tq).astype(o_ref.dtype)


def paged_decode_attention(q, kpool, vpool, layer, page_table, k_new, v_new, *, mode, qpos0, out_dtype=BF16,
                           mask=None, lam_f=None, subln=None, lam_init=0.0):
    b, tq, _ = q.shape
    n_pages = page_table.shape[1]
    kview, pps, n_steps = _paged_steps(kpool, page_table)
    vview, _, _ = _paged_steps(vpool, page_table)
    rows = pps * PAGE_SIZE
    n_c = 2 if mode == 'diff' else 1
    in_specs, args = [], []
    if mode == 'diff':
        in_specs.append(pl.BlockSpec(memory_space=pltpu.SMEM))
        args.append(lam_f.reshape(1).astype(F32))
    in_specs.append(pl.BlockSpec((None, tq, D_MODEL), lambda bi, s, pt: (bi, 0, 0)))
    args.append(q)
    in_specs += _page_specs(kview, layer, pps, n_pages) + _page_specs(vview, layer, pps, n_pages)
    args += [kview] * pps + [vview] * pps
    new_spec = pl.BlockSpec((None, GROUPS, PAGE_SIZE, HD), lambda bi, s, pt: (bi, 0, 0, 0))
    in_specs += [new_spec, new_spec]
    args += [k_new, v_new]
    if mode == 'mask':
        in_specs.append(pl.BlockSpec((None, tq, rows), lambda bi, s, pt: (bi, 0, s)))
        args.append(mask)
    elif mode == 'chunks':
        in_specs.append(pl.BlockSpec((None, GROUPS, tq, LANES), lambda bi, s, pt: (bi, 0, 0, s)))
        args.append(mask)
    else:
        in_specs.append(pl.BlockSpec((1, HD), lambda bi, s, pt: (0, 0)))
        args.append(subln.reshape(1, HD))
    kern = functools.partial(_paged_decode_kernel, pps=pps, mode=mode, qpos0=qpos0,
                             past_len=n_pages * PAGE_SIZE, post=1.0 - lam_init)
    grid_spec = pltpu.PrefetchScalarGridSpec(
        num_scalar_prefetch=1, grid=(b, n_steps + 1), in_specs=in_specs,
        out_specs=pl.BlockSpec((None, tq, D_MODEL), lambda bi, s, pt: (bi, 0, 0)),
        scratch_shapes=_softmax_scratch(GROUPS, n_c * HEADS_PER_GROUP * tq))
    return pl.pallas_call(
        kern, grid_spec=grid_spec, out_shape=jax.ShapeDtypeStruct(q.shape, out_dtype),
        compiler_params=_cparams("parallel", "arbitrary"), name="paged_decode_attention",
    )(page_table, *args)


def _paged_compress_kernel(pt_ref, *refs, pps):
    pools, w_ref, o_ref = refs[:pps], refs[pps], refs[pps + 1]
    cpp = PAGE_SIZE // CMP_STRIDE
    acc = jnp.zeros((GROUPS * pps * cpp, 2 * HD), F32)
    for j in range(CMP_STRIDE):
        x = jnp.concatenate([p[pl.ds(GROUPS * j + g, cpp, stride=GROUPS * CMP_STRIDE), :]
                             for g in range(GROUPS) for p in pools], axis=0)
        acc = acc + jnp.dot(x.astype(BF16), w_ref[j], preferred_element_type=F32)
    for g in range(GROUPS):
        o_ref[g] = acc[g * pps * cpp:(g + 1) * pps * cpp]


def paged_compress(pool, layer, page_table, w):
    b, n_pages = page_table.shape
    view, pps, n_steps = _paged_steps(pool, page_table)
    cps = pps * (PAGE_SIZE // CMP_STRIDE)
    grid_spec = pltpu.PrefetchScalarGridSpec(
        num_scalar_prefetch=1, grid=(b, n_steps),
        in_specs=_page_specs(view, layer, pps, n_pages) + [pl.BlockSpec(w.shape, lambda bi, s, pt: (0, 0, 0))],
        out_specs=pl.BlockSpec((None, GROUPS, cps, 2 * HD), lambda bi, s, pt: (bi, 0, s, 0)))
    return pl.pallas_call(
        functools.partial(_paged_compress_kernel, pps=pps), grid_spec=grid_spec,
        out_shape=jax.ShapeDtypeStruct((b, GROUPS, n_steps * cps, 2 * HD), F32),
        compiler_params=_cparams("parallel", "parallel"), name="paged_compress",
    )(page_table, *([view] * pps), w)


def lambda_init(layer):
    return 0.8 - 0.6 * math.exp(-0.3 * layer)


def _pad_rows(x, n):
    return jnp.pad(x, ((0, 0), (0, n - x.shape[1]), (0, 0)))


def trunk(x, nb, tq_real, qpos0, mem_k, mem_v, W, past):
    t = nb * tq_real
    tqp = max(tq_real, SUBLANES)
    pos = qpos0 + jnp.arange(tq_real, dtype=I32)
    pos_rows = jnp.tile(pos, nb) if tq_real < SUBLANES else pos
    tab64 = rope_tables(pos_rows, 64)
    tab128 = rope_tables(pos_rows, 128)
    new = {}

    def to_attn(a):
        return _pad_rows(a.reshape(nb, tq_real, a.shape[-1]), tqp)

    def from_attn(a):
        return a[:, :tq_real].reshape(t, a.shape[-1])

    if past is not None:
        pt = past['page_table']
        past_len = pt.shape[1] * PAGE_SIZE

        def new_block(new_bf):
            nr = new_bf.reshape(nb, tq_real, GROUPS, HD).transpose(0, 2, 1, 3)
            return jnp.pad(nr, ((0, 0), (0, 0), (0, PAGE_SIZE - tq_real), (0, 0)))
    else:
        past_len = 0

    for l in range(DEPTH):
        x = ffn(x, W['norm_ffn1'], W['ffn1_w_in'], W['ffn1_w_out'], l)
        h = rmsnorm(x, W['norm_mix'][l], BF16)
        i = l // N_MIXERS
        kind = l % N_MIXERS
        if kind == 0:
            proj = linear(h, W['a_w_in'], i)
            (q_rot,) = take_cols(proj, 0, A_Q, (BF16,), tab64)
            k_f, k_b = take_cols(proj, A_Q, A_K, (F32, BF16), tab64)
            (v_b,) = take_cols(proj, A_Q + A_K, A_KV_HEADS * A_VDIM, (BF16,))
            v_f = proj[:, A_Q + A_K:]
            new.setdefault('a_k', []).append(k_f.reshape(nb, tq_real, A_KV_HEADS, 2 * A_HALF))
            new.setdefault('a_v', []).append(v_f.reshape(nb, tq_real, A_KV_HEADS, A_VDIM))
            lam = W['a_lambda'][i]
            lam_f = (jnp.exp(jnp.sum(lam[0] * lam[1])) - jnp.exp(jnp.sum(lam[2] * lam[3]))).astype(F32) + lambda_init(l)
            if past is None:
                o = diff_attention(to_attn(q_rot), k_b.reshape(nb, tq_real, -1), v_b.reshape(nb, tq_real, -1),
                                   lam_f, W['a_subln'][i], lambda_init(l), qpos0)
            else:
                o = paged_decode_attention(to_attn(q_rot), past['a_k'], past['a_v'], i, pt, new_block(k_b),
                                           new_block(v_b), mode='diff', qpos0=qpos0, lam_f=lam_f,
                                           subln=W['a_subln'][i], lam_init=lambda_init(l))
            y_in, w_out = from_attn(o), W['a_w_out']
        elif kind == 1:
            w_in = W['b_w_in']
            n_main = B_Q + 6 * B_KV
            proj = linear(h, w_in, i, ncols=n_main)
            w_tail = jnp.pad(w_in[i, :, n_main:], ((0, 0), (0, LANES - (w_in.shape[-1] - n_main))))
            gate_logits = linear(h, w_tail, tn=LANES)
            (q_raw,) = take_cols(proj, 0, B_Q, (BF16,))
            (q_rot,) = take_cols(proj, 0, B_Q, (BF16,), tab128)
            kc_f = proj[:, B_Q:B_Q + B_KV]
            vc_f = proj[:, B_Q + B_KV:B_Q + 2 * B_KV]
            ks_f, ks_b = take_cols(proj, B_Q + 2 * B_KV, B_KV, (F32, BF16), tab128)
            vs_f = proj[:, B_Q + 3 * B_KV:B_Q + 4 * B_KV]
            (vs_b,) = take_cols(proj, B_Q + 3 * B_KV, B_KV, (BF16,))
            kw_f, kw_b = take_cols(proj, B_Q + 4 * B_KV, B_KV, (F32, BF16), tab128)
            vw_f = proj[:, B_Q + 5 * B_KV:B_Q + 6 * B_KV]
            (vw_b,) = take_cols(proj, B_Q + 5 * B_KV, B_KV, (BF16,))
            shp = (nb, tq_real, B_KV_HEADS, B_HD)
            for nm, a in (('b_cmp_k', kc_f), ('b_cmp_v', vc_f), ('b_sel_k', ks_f), ('b_sel_v', vs_f)):
                new.setdefault(nm, []).append(a.reshape(shp))

            cw = W['b_cmp_w'][i]
            wfs = [jnp.concatenate([cw[s, :CMP_STRIDE], cw[s, CMP_STRIDE:]], axis=-1) for s in range(2)]
            if past is None:
                l_all = tq_real
                ks_all, vs_all = ks_b.reshape(nb, tq_real, -1), vs_b.reshape(nb, tq_real, -1)
                kw_all, vw_all = kw_b.reshape(nb, tq_real, -1), vw_b.reshape(nb, tq_real, -1)
                kwpos0 = 0
                keep = min(WINDOW, tq_real)
                new.setdefault('b_win_k', []).append(kw_f.reshape(shp)[:, tq_real - keep:])
                new.setdefault('b_win_v', []).append(vw_f.reshape(shp)[:, tq_real - keep:])
                lp_sel = tq_real
            else:
                l_all = past_len + tq_real
                bkw = past['b_win_k'][i].reshape(nb, -1, B_KV)
                bvw = past['b_win_v'][i].reshape(nb, -1, B_KV)
                wb = bkw.shape[1]
                kb = jnp.concatenate([bkw, kw_f.reshape(nb, tq_real, B_KV)], axis=1)
                vb = jnp.concatenate([bvw, vw_f.reshape(nb, tq_real, B_KV)], axis=1)
                new.setdefault('b_win_k', []).append(kb[:, tq_real:].reshape(nb, wb, B_KV_HEADS, B_HD))
                new.setdefault('b_win_v', []).append(vb[:, tq_real:].reshape(nb, wb, B_KV_HEADS, B_HD))
                lw = -(-(wb + tq_real) // 512) * 512
                kw_all = _pad_rows(kb, lw).astype(BF16)
                vw_all = _pad_rows(vb, lw).astype(BF16)
                kwpos0 = qpos0 - wb
                lp_sel = past_len + math.gcd(PAGES_PER_STEP, pt.shape[1]) * PAGE_SIZE
            nch = (l_all // CMP_STRIDE)
            nblk = nch - 1
            n_sel = -(-l_all // SEL_BLOCK)
            n_rep = nch // (SEL_BLOCK // CMP_STRIDE)
            assert nch % (SEL_BLOCK // CMP_STRIDE) == 0 and n_sel - n_rep in (0, 1)
            n_top_rep = min(SEL_N, n_sel) - (n_sel - n_rep)
            wch = -(-(lp_sel // CMP_STRIDE) // LANES) * LANES

            def partials(rows, wf):
                xg = rows.reshape(nb, nch, CMP_STRIDE, B_KV_HEADS, B_HD)
                xg = xg.transpose(0, 3, 1, 2, 4).reshape(nb * B_KV_HEADS * nch, CMP_STRIDE * B_HD)
                return linear(xg, wf.reshape(CMP_STRIDE * B_HD, 2 * B_HD), tn=2 * B_HD).reshape(
                    nb, B_KV_HEADS, nch, 2 * B_HD)

            if past is None:
                fsk, fsv = partials(kc_f, wfs[0]), partials(vc_f, wfs[1])
            else:
                fsk = paged_compress(past['b_cmp_k'], i, pt, wfs[0].astype(BF16))
                fsv = paged_compress(past['b_cmp_v'], i, pt, wfs[1].astype(BF16))
            fsk, fsv = (jnp.pad(f, ((0, 0), (0, 0), (0, wch - nch), (0, 0))) for f in (fsk, fsv))
            o_cmp, selmask = nsa_compressed(to_attn(q_raw), fsk, fsv, W['b_cmp_b'][i].astype(F32),
                                            nblk=nblk, nch=nch, qpos0=qpos0, n_top_rep=n_top_rep)
            gates = gate_logits[:, :3 * B_HEADS].reshape(nb, tq_real, B_KV_HEADS, 3 * HEADS_PER_GROUP)
            gates = jnp.pad(gates.transpose(0, 2, 1, 3),
                            ((0, 0), (0, 0), (0, tqp - tq_real), (0, LANES - 3 * HEADS_PER_GROUP)))
            if past is None:
                o = nsa_select_window(to_attn(q_rot), ks_all, vs_all, kw_all, vw_all, selmask, o_cmp, gates,
                                      qpos0=qpos0, kwpos0=kwpos0)
            else:
                o_sel = paged_decode_attention(to_attn(q_rot), past['b_sel_k'], past['b_sel_v'], i, pt,
                                               new_block(ks_b), new_block(vs_b), mode='chunks', qpos0=qpos0,
                                               mask=selmask, out_dtype=F32)
                o = nsa_select_window(to_attn(q_rot), None, None, kw_all, vw_all, None, o_cmp, gates,
                                      qpos0=qpos0, kwpos0=kwpos0, o_sel=o_sel)
            y_in, w_out = from_attn(o), W['b_w_out']
        else:
            w_in = W['c_w_in']
            n_main = C_Q + 2 * C_KV + IDX_HEADS * IDX_DIM
            proj = linear(h, w_in, i, ncols=n_main)
            w_tail = jnp.pad(w_in[i, :, n_main:], ((0, 0), (0, LANES - (w_in.shape[-1] - n_main))))
            tail = linear(h, w_tail, tn=LANES)
            (q_rot,) = take_cols(proj, 0, C_Q, (BF16,), tab128)
            k_f, k_b = take_cols(proj, C_Q, C_KV, (F32, BF16), tab128)
            v_f = proj[:, C_Q + C_KV:C_Q + 2 * C_KV]
            (v_b,) = take_cols(proj, C_Q + C_KV, C_KV, (BF16,))
            (iq_rot,) = take_cols(proj, C_Q + 2 * C_KV, IDX_HEADS * IDX_DIM, (BF16,), tab64)
            ik_f, ik2 = index_keys(tail, W['c_idx_knorm'][i], tab64)
            shp = (nb, tq_real, C_KV_HEADS, C_HD)
            new.setdefault('c_k', []).append(k_f.reshape(shp))
            new.setdefault('c_v', []).append(v_f.reshape(shp))
            new.setdefault('c_idx_k', []).append(ik_f[:, :IDX_DIM].reshape(nb, tq_real, IDX_DIM))
            l_all = past_len + tq_real
            top = min(IDX_TOPK_MAX, l_all // 4)
            if past is None:
                mask = dsa_select(to_attn(iq_rot), ik2.reshape(nb, tq_real, LANES), to_attn(tail), qpos0=qpos0, top=top)
                o = masked_attention(to_attn(q_rot), k_b.reshape(nb, tq_real, -1), v_b.reshape(nb, tq_real, -1),
                                     mask, qpos0=qpos0)
            else:
                ik_all = gather_index_keys(past['c_idx_k'], i, pt,
                                           _pad_rows(ik2.reshape(nb, tq_real, LANES), PAGE_SIZE))
                mask = dsa_select(to_attn(iq_rot), ik_all, to_attn(tail), qpos0=qpos0, top=top)
                o = paged_decode_attention(to_attn(q_rot), past['c_k'], past['c_v'], i, pt, new_block(k_b),
                                           new_block(v_b), mode='mask', qpos0=qpos0, mask=mask)
            y_in, w_out = from_attn(o), W['c_w_out']
        x = linear(y_in, w_out, i, res=x, scale=1.0)

        h = rmsnorm(x, W['norm_cross'][l], BF16)
        qx = linear(h, W['x_w_q'], l, out_dtype=BF16)
        ox = cross_attention(to_attn(qx), mem_k[l], mem_v[l])
        x = linear(from_attn(ox), W['x_w_o'], l, res=x, scale=1.0)
        x = ffn(x, W['norm_ffn2'], W['ffn2_w_in'], W['ffn2_w_out'], l)
    y = rmsnorm(x, W['final_norm'], F32)
    return y, {nm: jnp.stack(v) for nm, v in new.items()}


def kernel(x_prompt, x_sample, cache_a_k, cache_a_v, cache_b_cmp_k, cache_b_cmp_v, cache_b_sel_k, cache_b_sel_v, state_b_win_k, state_b_win_v, cache_c_k, cache_c_v, cache_c_idx_k, cache_mem_k, cache_mem_v, page_table, mem_prompt, norm_ffn1, norm_mix, norm_cross, norm_ffn2, final_norm, ffn1_w_in, ffn1_w_out, ffn2_w_in, ffn2_w_out, x_w_q, x_w_kv, x_w_o, a_w_in, a_w_out, a_lambda, a_subln, b_w_in, b_w_out, b_cmp_w, b_cmp_b, c_w_in, c_w_out, c_idx_knorm):
    W = dict(norm_ffn1=norm_ffn1, norm_mix=norm_mix, norm_cross=norm_cross, norm_ffn2=norm_ffn2,
             final_norm=final_norm, ffn1_w_in=ffn1_w_in, ffn1_w_out=ffn1_w_out, ffn2_w_in=ffn2_w_in,
             ffn2_w_out=ffn2_w_out, x_w_q=x_w_q, x_w_o=x_w_o, a_w_in=a_w_in, a_w_out=a_w_out,
             a_lambda=a_lambda, a_subln=a_subln, b_w_in=b_w_in, b_w_out=b_w_out, b_cmp_w=b_cmp_w,
             b_cmp_b=b_cmp_b, c_w_in=c_w_in, c_w_out=c_w_out, c_idx_knorm=c_idx_knorm)
    nbp, seq, d = x_prompt.shape
    nbs, dseq, _ = x_sample.shape
    ml = mem_prompt.shape[1]

    mem2d = mem_prompt.reshape(nbp * ml, d)
    mkv = [linear(mem2d, x_w_kv, l) for l in range(DEPTH)]
    p_mem_k = jnp.stack([m[:, :X_W].reshape(nbp, ml, X_HEADS, X_HD) for m in mkv])
    p_mem_v = jnp.stack([m[:, X_W:].reshape(nbp, ml, X_HEADS, X_HD) for m in mkv])
    y_p, ps = trunk(x_prompt.reshape(nbp * seq, d), nbp, seq, 0,
                    p_mem_k.reshape(DEPTH, nbp, ml, X_W), p_mem_v.reshape(DEPTH, nbp, ml, X_W), W, None)

    past = dict(a_k=cache_a_k, a_v=cache_a_v, b_cmp_k=cache_b_cmp_k, b_cmp_v=cache_b_cmp_v,
                b_sel_k=cache_b_sel_k, b_sel_v=cache_b_sel_v, b_win_k=state_b_win_k, b_win_v=state_b_win_v,
                c_k=cache_c_k, c_v=cache_c_v, c_idx_k=cache_c_idx_k, page_table=page_table)
    past_len = page_table.shape[1] * PAGE_SIZE
    sml = cache_mem_k.shape[2]
    y_s, ss = trunk(x_sample.reshape(nbs * dseq, d), nbs, dseq, past_len,
                    cache_mem_k.reshape(DEPTH, nbs, sml, X_W), cache_mem_v.reshape(DEPTH, nbs, sml, X_W), W, past)

    return (y_p.reshape(nbp, seq, d), y_s.reshape(nbs, dseq, d),
            ps['a_k'], ps['a_v'], ps['b_cmp_k'], ps['b_cmp_v'], ps['b_sel_k'], ps['b_sel_v'],
            ps['b_win_k'], ps['b_win_v'], ps['c_k'], ps['c_v'], ps['c_idx_k'], p_mem_k, p_mem_v,
            ss['a_k'], ss['a_v'], ss['b_cmp_k'], ss['b_cmp_v'], ss['b_sel_k'], ss['b_sel_v'],
            ss['b_win_k'], ss['b_win_v'], ss['c_k'], ss['c_v'], ss['c_idx_k'])
```

```python
import functools
import math

import jax
import jax.numpy as jnp
from jax import lax
from jax.experimental import pallas as pl
from jax.experimental.pallas import tpu as pltpu

F32 = jnp.float32
BF16 = jnp.bfloat16
I32 = jnp.int32

D_MODEL = 2048
DEPTH = 4
PAGE_SIZE = 128
N_MIXERS = 3
ROPE_THETA = 500000.0
EPS = 1e-6
NEG_INF = -1e30
FORCE = 1e9
D_FF = 256 * math.ceil(8 * D_MODEL / 3 / 256)

A_HEADS = D_MODEL // 128
A_HALF = 64
A_VDIM = 128
A_KV_HEADS = 4
A_Q = A_HEADS * 2 * A_HALF
A_K = A_KV_HEADS * 2 * A_HALF

B_HEADS = D_MODEL // 128
B_HD = 128
B_KV_HEADS = 4
B_KV = B_KV_HEADS * B_HD
B_Q = B_HEADS * B_HD
CMP_STRIDE = 16
CMP_LEN = 32
SEL_BLOCK = 64
SEL_N = 16
WINDOW = 512

C_HEADS = D_MODEL // 128
C_HD = 128
C_KV_HEADS = 4
C_Q = C_HEADS * C_HD
C_KV = C_KV_HEADS * C_HD
IDX_HEADS = 16
IDX_DIM = 64
IDX_TOPK_MAX = 256

MEM_LEN = 256
X_HEADS = 4
X_HD = 128
X_W = X_HEADS * X_HD

GROUPS = 4
HEADS_PER_GROUP = 4
HD = 128
GW = HEADS_PER_GROUP * HD

LANES = 128
SUBLANES = 8
VMEM_LIMIT_BYTES = 56 * 1024 * 1024
M_INIT = -1e29
LOG2E = 1.4426950408889634
INT_MIN = -2147483648


def _cparams(*sem):
    return pltpu.CompilerParams(dimension_semantics=sem, vmem_limit_bytes=VMEM_LIMIT_BYTES)


def _pick(n, pref):
    if n <= pref:
        return n
    t = pref
    while n % t:
        t //= 2
    return t


def _rms_kernel(x_ref, g_ref, o_ref):
    x = x_ref[...]
    ms = jnp.mean(x * x, axis=-1, keepdims=True)
    o_ref[...] = (x * lax.rsqrt(ms + EPS) * g_ref[...]).astype(o_ref.dtype)


def rmsnorm(x, g, out_dtype):
    t, d = x.shape
    tm = _pick(t, 512)
    return pl.pallas_call(
        _rms_kernel,
        grid=(t // tm,),
        in_specs=[pl.BlockSpec((tm, d), lambda i: (i, 0)), pl.BlockSpec((1, d), lambda i: (0, 0))],
        out_specs=pl.BlockSpec((tm, d), lambda i: (i, 0)),
        out_shape=jax.ShapeDtypeStruct((t, d), out_dtype),
        compiler_params=_cparams("parallel"),
        name="rmsnorm",
    )(x, g.reshape(1, d))


def _dense_kernel(*refs, mode, scale, extra):
    it = iter(refs)
    a_ref = next(it)
    a2_ref = next(it) if extra else None
    w_refs = [next(it) for _ in range(2 if mode == 'swiglu' else 1)]
    r_ref = next(it) if mode == 'res' else None
    r2_ref = next(it) if mode == 'res' and extra else None
    o_ref = next(it)
    o2_ref = next(it) if extra else None
    wb_refs = [next(it) for _ in w_refs]

    def apply(a, r, out):
        a = a[...].astype(BF16)
        if mode == 'swiglu':
            g = jnp.dot(a, wb_refs[0][...], preferred_element_type=F32)
            u = jnp.dot(a, wb_refs[1][...], preferred_element_type=F32)
            y = g * jax.nn.sigmoid(g) * u
        else:
            y = jnp.dot(a, wb_refs[0][...], preferred_element_type=F32)
            if mode == 'res':
                y = r[...] + scale * y
        out[...] = y.astype(out.dtype)

    @pl.when(pl.program_id(1) == 0)
    def _():
        for w_ref, wb_ref in zip(w_refs, wb_refs):
            wb_ref[...] = w_ref[...].astype(BF16)
        if extra:
            apply(a2_ref, r2_ref, o2_ref)

    apply(a_ref, r_ref, o_ref)


def _w_spec(w, layer, k, tn, c0):
    if w.ndim == 3:
        return pl.BlockSpec((None, k, tn), lambda j, i: (layer, 0, j + c0))
    return pl.BlockSpec((k, tn), lambda j, i: (0, j + c0))


def _dense(a, w_specs, ws, *, mode, k, tn, ncols, out_dtype, res=None, scale=1.0, tm=None, name):
    pair = isinstance(a, (list, tuple))
    a_main, a_x = (a[0], a[1]) if pair else (a, None)
    r_main, r_x = (res[0], res[1]) if (pair and res is not None) else (res, None)
    m = a_main.shape[0]
    tm = _pick(m, (2048 if k <= D_MODEL else 512) if tm is None else tm)
    assert m % tm == 0 and ncols % tn == 0
    row_spec = pl.BlockSpec((tm, k), lambda j, i: (i, 0))
    o_spec = pl.BlockSpec((tm, tn), lambda j, i: (i, j))
    in_specs, args = [row_spec], [a_main]
    out_specs, out_shape = [o_spec], [jax.ShapeDtypeStruct((m, ncols), out_dtype)]
    if pair:
        mx = a_x.shape[0]
        x_out = pl.BlockSpec((mx, tn), lambda j, i: (0, j))
        in_specs.append(pl.BlockSpec((mx, k), lambda j, i: (0, 0)))
        args.append(a_x)
        out_specs.append(x_out)
        out_shape.append(jax.ShapeDtypeStruct((mx, ncols), out_dtype))
    in_specs += w_specs
    args += ws
    if mode == 'res':
        in_specs.append(o_spec)
        args.append(r_main)
        if pair:
            in_specs.append(x_out)
            args.append(r_x)
    outs = pl.pallas_call(
        functools.partial(_dense_kernel, mode=mode, scale=scale, extra=pair),
        grid=(ncols // tn, m // tm), in_specs=in_specs, out_specs=out_specs, out_shape=out_shape,
        scratch_shapes=[pltpu.VMEM((k, tn), BF16) for _ in w_specs],
        compiler_params=_cparams("parallel", "arbitrary"), name=name,
    )(*args)
    return list(outs) if pair else outs[0]


def linear(a, w, layer=0, *, col0=0, ncols=None, out_dtype=F32, tm=None, tn=512, res=None, scale=1.0):
    k = (a[0] if isinstance(a, (list, tuple)) else a).shape[1]
    ncols = w.shape[-1] - col0 if ncols is None else ncols
    tn = _pick(ncols, tn)
    assert col0 % tn == 0
    return _dense(a, [_w_spec(w, layer, k, tn, col0 // tn)], [w], mode='plain' if res is None else 'res', k=k, tn=tn,
                  ncols=ncols, out_dtype=F32 if res is not None else out_dtype, res=res, scale=scale, tm=tm,
                  name='linear' if res is None else 'linear_res')


def swiglu_in(a, w_in, layer):
    k = (a[0] if isinstance(a, (list, tuple)) else a).shape[1]
    tn = 512
    nj = D_FF // tn
    return _dense(a, [_w_spec(w_in, layer, k, tn, 0), _w_spec(w_in, layer, k, tn, nj)], [w_in, w_in], mode='swiglu',
                  k=k, tn=tn, ncols=D_FF, out_dtype=BF16, name='swiglu_in')


def _ffn_steps(x, g, w_in, w_out, layer):
    h = rmsnorm(x, g[layer], BF16)
    hid = yield (swiglu_in, h, None, dict(w_in=w_in, layer=layer))
    return (yield (linear, hid, x, dict(w=w_out, layer=layer, scale=0.5)))


def _run_together(gens):
    reqs = [next(g) for g in gens]
    results = [None] * len(gens)
    while True:
        fn, _, res0, kw = reqs[0]
        if res0 is not None:
            kw = dict(kw, res=[r[2] for r in reqs])
        outs = fn([r[1] for r in reqs], **kw)
        nxt = []
        for gi, (g, o) in enumerate(zip(gens, outs)):
            try:
                nxt.append(g.send(o))
            except StopIteration as stop:
                results[gi] = stop.value
        if not nxt:
            return results
        assert len(nxt) == len(gens)
        reqs = nxt


def rope_tables(pos, dh):
    rot = dh // 4
    half = rot // 2
    inv = ROPE_THETA ** (-jnp.arange(half, dtype=F32) / half)
    ang = pos.astype(F32)[:, None] * inv[None, :]
    cos, sin = jnp.cos(ang), jnp.sin(ang)
    p = pos.shape[0]
    rest1 = jnp.ones((p, dh - rot), F32)
    rest0 = jnp.zeros((p, dh - rot), F32)
    z = jnp.zeros((p, half), F32)
    c = jnp.concatenate([cos, cos, rest1], axis=1)
    up = jnp.concatenate([-sin, z, rest0], axis=1)
    dn = jnp.concatenate([z, sin, rest0], axis=1)
    rep = LANES // dh
    return tuple(jnp.tile(t, (1, rep)) for t in (c, up, dn)), half


def _rope_chunk(xc, c, up, dn, half):
    return xc * c + pltpu.roll(xc, LANES - half, 1) * up + pltpu.roll(xc, half, 1) * dn


def _cols_kernel(*refs, half, width):
    if half:
        x_ref, c_ref, u_ref, d_ref = refs[:4]
        outs = refs[4:]
        c, up, dn = c_ref[...], u_ref[...], d_ref[...]
        x = x_ref[...]
        y = jnp.concatenate(
            [_rope_chunk(x[:, k * LANES:(k + 1) * LANES], c, up, dn, half) for k in range(width // LANES)], axis=1)
    else:
        x_ref = refs[0]
        outs = refs[1:]
        y = x_ref[...]
    for o in outs:
        o[...] = y.astype(o.dtype)


def take_cols(x, col0, width, out_dtypes, tabs=None, n_pos_blocks=1):
    t = x.shape[0]
    bw = 512 if width % 512 == 0 else width
    assert col0 % bw == 0
    tables, half = tabs if tabs is not None else ((), 0)
    tm = _pick(t, 512) if tabs is None else _pick(tables[0].shape[0], 512)
    assert t % tm == 0
    npb = (tables[0].shape[0] // tm) if tabs is not None else 1
    c0 = col0 // bw
    in_specs = [pl.BlockSpec((tm, bw), lambda i, j: (i, j + c0))]
    in_specs += [pl.BlockSpec((tm, LANES), lambda i, j: (i % npb, 0)) for _ in tables]
    outs = pl.pallas_call(
        functools.partial(_cols_kernel, half=half, width=bw),
        grid=(t // tm, width // bw),
        in_specs=in_specs,
        out_specs=[pl.BlockSpec((tm, bw), lambda i, j: (i, j)) for _ in out_dtypes],
        out_shape=[jax.ShapeDtypeStruct((t, width), dt) for dt in out_dtypes],
        compiler_params=_cparams("parallel", "parallel"), name="take_cols",
    )(x, *tables)
    return outs


def _stack_heads(q, sel=None):
    parts = []
    for r in range(HEADS_PER_GROUP):
        qr = q[:, r * HD:(r + 1) * HD]
        if sel is not None:
            qr = jnp.where(sel, qr, jnp.zeros((), q.dtype))
        parts.append(qr)
    return _concat_rows(parts)


def _concat_rows(parts):
    if parts[0].shape[0] % (2 * SUBLANES):
        return jnp.concatenate([p.astype(F32) for p in parts], axis=0).astype(parts[0].dtype)
    return jnp.concatenate(parts, axis=0)


def _unstack_heads(o, tq):
    return jnp.concatenate([o[r * tq:(r + 1) * tq] for r in range(HEADS_PER_GROUP)], axis=1)


def _qk(q, k):
    return lax.dot_general(q, k, (((1,), (1,)), ((), ())), preferred_element_type=F32)


def _mask_rows(mask, s, tq):
    tk = s.shape[-1]
    return jnp.where(mask[None], s.reshape(-1, tq, tk), NEG_INF).reshape(s.shape)


def _online_update(slot, s, vt, m_sc, l_sc, acc_sc):
    m_prev = m_sc[slot]
    m_new = jnp.maximum(m_prev, jnp.max(s, axis=-1, keepdims=True))
    alpha = jnp.exp2(m_prev - m_new)
    p = jnp.exp2(s - jnp.concatenate([m_new] * (s.shape[-1] // LANES), axis=1))
    l_sc[slot] = alpha * l_sc[slot] + jnp.sum(p, axis=-1, keepdims=True)
    acc_sc[slot] = alpha * acc_sc[slot] + jnp.dot(p.astype(BF16), vt, preferred_element_type=F32)
    m_sc[slot] = m_new


def _init_state(m_sc, l_sc, acc_sc):
    m_sc[...] = jnp.full(m_sc.shape, M_INIT, F32)
    l_sc[...] = jnp.zeros(l_sc.shape, F32)
    acc_sc[...] = jnp.zeros(acc_sc.shape, F32)


def _order_key(x):
    b = lax.bitcast_convert_type(x, I32)
    return jnp.where(b < 0, b ^ jnp.int32(0x7FFFFFFF), b)


def _kth_largest_key(key, k):
    kf = jnp.float32(k)

    def count_ge(t):
        return jnp.sum(jnp.where(key >= t, 1.0, 0.0), axis=-1, keepdims=True)

    t0 = jnp.where(count_ge(jnp.int32(0)) >= kf, jnp.int32(0), jnp.int32(INT_MIN))

    def one_bit(bit, t):
        cand = t | (jnp.int32(1) << bit)
        return jnp.where(count_ge(cand) >= kf, cand, t)

    if key.shape[0] * key.shape[1] > 128 * 1024:
        return lax.fori_loop(0, 31, lambda it, t: one_bit(jnp.int32(30) - it, t), t0)

    def two_bits(it, t):
        hi = jnp.int32(1) << (jnp.int32(30) - 2 * it)
        lo = jnp.int32(1) << (jnp.int32(29) - 2 * it)
        c1, c2, c3 = t | lo, t | hi, t | hi | lo
        return jnp.where(count_ge(c3) >= kf, c3,
                         jnp.where(count_ge(c2) >= kf, c2, jnp.where(count_ge(c1) >= kf, c1, t)))

    return one_bit(jnp.int32(0), lax.fori_loop(0, 15, two_bits, t0))


def _n_kv_tiles(qpos0, i, tq, tk, lp):
    hi = qpos0 + (i + 1) * tq
    return jnp.minimum((hi + tk - 1) // tk, lp // tk)


def _kv_tile(lp, tq):
    return _pick(lp, 512 if tq >= 64 else 2048)


def _kv_len(k):
    return k.shape[1] if k.ndim == 3 else k.shape[2]


def _kv_spec(k, bg):
    lp = _kv_len(k)
    if k.ndim == 3:
        return pl.BlockSpec((None, lp, HD), lambda *idx: (bg(*idx)[0], 0, bg(*idx)[1]))
    return pl.BlockSpec((None, None, lp, HD), lambda *idx: (bg(*idx)[0], bg(*idx)[1], 0, 0))


def _softmax_scratch(slots, rows):
    return [pltpu.VMEM((slots, rows, LANES), F32), pltpu.VMEM((slots, rows, LANES), F32),
            pltpu.VMEM((slots, rows, HD), F32)]


def _diff_attn_kernel(lam_ref, q_ref, k_ref, v_ref, g_ref, o_ref, m_sc, l_sc, acc_sc, *, tq, tk, qpos0, lp, post):
    i = pl.program_id(2)
    q = q_ref[...]
    lane = lax.broadcasted_iota(I32, (1, HD), 1)
    lo = lane < A_HALF
    qs = _concat_rows([_stack_heads(q, lo), _stack_heads(q, jnp.logical_not(lo))])
    qpos = qpos0 + i * tq + lax.broadcasted_iota(I32, (tq, 1), 0)
    _init_state(m_sc, l_sc, acc_sc)
    scale = (A_HALF ** -0.5) * LOG2E
    half = HEADS_PER_GROUP * tq

    def step(j, masked):
        off = pl.multiple_of(j * tk, tk)
        s = _qk(qs, k_ref[pl.ds(off, tk), :]) * scale
        if masked:
            s = _mask_rows(off + lax.broadcasted_iota(I32, (1, tk), 1) <= qpos, s, tq)
        _online_update(0, s, v_ref[pl.ds(off, tk), :], m_sc, l_sc, acc_sc)

    n_full = jnp.minimum((qpos0 + i * tq + 1) // tk, lp // tk)
    lax.fori_loop(0, n_full, lambda j, c: (step(j, False), c)[1], 0)
    lax.fori_loop(n_full, _n_kv_tiles(qpos0, i, tq, tk, lp), lambda j, c: (step(j, True), c)[1], 0)
    on = acc_sc[0] / l_sc[0]
    o = on[:half] - lam_ref[0] * on[half:]
    ms = jnp.mean(o * o, axis=-1, keepdims=True)
    o = o * lax.rsqrt(ms + EPS) * g_ref[...] * post
    o_ref[...] = _unstack_heads(o, tq).astype(o_ref.dtype)


def diff_attention(q, k, v, lam_f, subln, lam_init, qpos0):
    b, tqn, _ = q.shape
    lp = _kv_len(k)
    tq = _pick(tqn, 256)
    tk = _kv_tile(lp, tq)
    kern = functools.partial(_diff_attn_kernel, tq=tq, tk=tk, qpos0=qpos0, lp=lp, post=1.0 - lam_init)
    return pl.pallas_call(
        kern, grid=(b, GROUPS, tqn // tq),
        in_specs=[pl.BlockSpec(memory_space=pltpu.SMEM),
                  pl.BlockSpec((None, tq, GW), lambda bi, g, i: (bi, i, g)),
                  _kv_spec(k, lambda bi, g, i: (bi, g)),
                  _kv_spec(v, lambda bi, g, i: (bi, g)),
                  pl.BlockSpec((1, HD), lambda bi, g, i: (0, 0))],
        out_specs=pl.BlockSpec((None, tq, GW), lambda bi, g, i: (bi, i, g)),
        out_shape=jax.ShapeDtypeStruct(q.shape, BF16),
        scratch_shapes=_softmax_scratch(1, 2 * HEADS_PER_GROUP * tq),
        compiler_params=_cparams("parallel", "parallel", "parallel"), name="diff_attention",
    )(lam_f.reshape(1).astype(F32), q, k, v, subln.reshape(1, HD))


def _nsa_cmp_kernel(q_ref, fsk_ref, fsv_ref, b_ref, o_ref, sel_ref, *, tq, w, nblk, nch, qpos0, klanes):
    i = pl.program_id(2)
    qst = _stack_heads(q_ref[...])
    fsk = fsk_ref[...]
    fsv = fsv_ref[...]
    kcb = fsk[:, :HD] + pltpu.roll(fsk[:, HD:], w - 1, 0) + b_ref[0:1, :]
    vcb = fsv[:, :HD] + pltpu.roll(fsv[:, HD:], w - 1, 0) + b_ref[1:2, :]
    qpos = qpos0 + i * tq + lax.broadcasted_iota(I32, (tq, 1), 0)
    lanei = lax.broadcasted_iota(I32, (1, w), 1)
    s = _qk(qst, kcb.astype(BF16)) * (B_HD ** -0.5)
    valid = (lanei * CMP_STRIDE + (CMP_LEN - 1) <= qpos) & (lanei < nblk)
    s3 = jnp.where(valid[None], s.reshape(HEADS_PER_GROUP, tq, w), NEG_INF)
    e = jnp.exp(s3 - jnp.max(s3, axis=-1, keepdims=True))
    p = e / jnp.sum(e, axis=-1, keepdims=True)
    any_valid = (qpos >= CMP_LEN - 1) & (nblk > 0)
    p = jnp.where(any_valid[None], p, 0.0)
    o = jnp.dot(p.reshape(HEADS_PER_GROUP * tq, w).astype(BF16), vcb.astype(BF16), preferred_element_type=F32)
    o_ref[...] = _unstack_heads(o, tq).astype(o_ref.dtype)

    grp = p[0] + p[1] + p[2] + p[3]
    chunk = grp + jnp.where(lanei == 0, 0.0, pltpu.roll(grp, 1, 1))
    pair = jnp.where((lanei & 1) == 0, chunk + pltpu.roll(chunk, w - 1, 1), chunk + pltpu.roll(chunk, 1, 1))
    quad = jnp.where((lanei & 2) == 0, pair + pltpu.roll(pair, w - 2, 1), pair + pltpu.roll(pair, 2, 1))
    jb = lanei >> 2
    cur = qpos >> 6
    forced = (jb == 0) | (jb == cur) | (jb == cur - 1)
    score = jnp.where(forced, FORCE, jnp.where(jb * SEL_BLOCK <= qpos, quad, NEG_INF))
    score = jnp.where(lanei < nch, score, NEG_INF)
    key = _order_key(score)
    thr = _kth_largest_key(key, klanes)
    sel_ref[...] = jnp.where((key >= thr) | (lanei >= nch), 1.0, 0.0).astype(sel_ref.dtype)


def nsa_compressed(q_raw, fsk, fsv, bias, *, nblk, nch, qpos0, n_top_rep):
    b, tqn, _ = q_raw.shape
    w = fsk.shape[2]
    tq = _pick(tqn, 256)
    kern = functools.partial(_nsa_cmp_kernel, tq=tq, w=w, nblk=nblk, nch=nch, qpos0=qpos0,
                             klanes=(SEL_BLOCK // CMP_STRIDE) * n_top_rep)
    return pl.pallas_call(
        kern, grid=(b, GROUPS, tqn // tq),
        in_specs=[pl.BlockSpec((None, tq, GW), lambda bi, g, i: (bi, i, g)),
                  pl.BlockSpec((None, None, w, 2 * HD), lambda bi, g, i: (bi, g, 0, 0)),
                  pl.BlockSpec((None, None, w, 2 * HD), lambda bi, g, i: (bi, g, 0, 0)),
                  pl.BlockSpec((2, HD), lambda bi, g, i: (0, 0))],
        out_specs=[pl.BlockSpec((None, tq, GW), lambda bi, g, i: (bi, i, g)),
                   pl.BlockSpec((None, None, tq, w), lambda bi, g, i: (bi, g, i, 0))],
        out_shape=[jax.ShapeDtypeStruct(q_raw.shape, F32), jax.ShapeDtypeStruct((b, GROUPS, tqn, w), BF16)],
        compiler_params=_cparams("parallel", "parallel", "parallel"), name="nsa_compressed",
    )(q_raw, fsk, fsv, bias)


def _nsa_sw_kernel(*refs, tq, tk, tkw, qpos0, lp, lw, kwpos0, w, sel_given):
    if sel_given:
        q_ref, kw_ref, vw_ref, osel_ref, oc_ref, gate_ref, o_ref, m_sc, l_sc, acc_sc = refs
    else:
        q_ref, ks_ref, vs_ref, kw_ref, vw_ref, sel_ref, oc_ref, gate_ref, o_ref, m_sc, l_sc, acc_sc = refs
    i = pl.program_id(2)
    qst = _stack_heads(q_ref[...])
    qpos = qpos0 + i * tq + lax.broadcasted_iota(I32, (tq, 1), 0)
    _init_state(m_sc, l_sc, acc_sc)
    scale = (B_HD ** -0.5) * LOG2E
    chunks_per_tile = tk // CMP_STRIDE
    assert LANES % chunks_per_tile == 0 and w % LANES == 0

    def sel_body(j, carry):
        off = pl.multiple_of(j * tk, tk)
        kt = ks_ref[pl.ds(off, tk), :]
        vt = vs_ref[pl.ds(off, tk), :]
        kpos = off + lax.broadcasted_iota(I32, (1, tk), 1)
        cbase = pl.multiple_of((j * chunks_per_tile) // LANES * LANES, LANES)
        chunk_row = cbase + lax.broadcasted_iota(I32, (LANES, 1), 0)
        expand = jnp.where(chunk_row == (kpos >> 4), 1.0, 0.0).astype(BF16)
        chosen = jnp.dot(sel_ref[:, pl.ds(cbase, LANES)], expand, preferred_element_type=F32) > 0.5
        mask = chosen & (kpos <= qpos)
        _online_update(0, _mask_rows(mask, _qk(qst, kt) * scale, tq), vt, m_sc, l_sc, acc_sc)
        return carry

    if not sel_given:
        lax.fori_loop(0, _n_kv_tiles(qpos0, i, tq, tk, lp), sel_body, 0)

    def win_body(j, carry):
        off = pl.multiple_of(j * tkw, tkw)
        kt = kw_ref[pl.ds(off, tkw), :]
        vt = vw_ref[pl.ds(off, tkw), :]
        kpos = kwpos0 + off + lax.broadcasted_iota(I32, (1, tkw), 1)
        dt = qpos - kpos
        mask = (dt >= 0) & (dt <= WINDOW)
        _online_update(1, _mask_rows(mask, _qk(qst, kt) * scale, tq), vt, m_sc, l_sc, acc_sc)
        return carry

    first = jnp.maximum(qpos0 + i * tq - WINDOW - kwpos0, 0) // tkw
    last = jnp.minimum((qpos0 + (i + 1) * tq - kwpos0 + tkw - 1) // tkw, lw // tkw)
    lax.fori_loop(first, last, win_body, 0)

    o_sel = osel_ref[...] if sel_given else _unstack_heads(acc_sc[0] / l_sc[0], tq)
    o_win = _unstack_heads(acc_sc[1] / l_sc[1], tq)
    gsig = jax.nn.sigmoid(gate_ref[...])
    oc = oc_ref[...]
    outs = []
    for r in range(HEADS_PER_GROUP):
        cs = slice(r * HD, (r + 1) * HD)
        outs.append(oc[:, cs] * gsig[:, 3 * r:3 * r + 1] + o_sel[:, cs] * gsig[:, 3 * r + 1:3 * r + 2]
                    + o_win[:, cs] * gsig[:, 3 * r + 2:3 * r + 3])
    o_ref[...] = jnp.concatenate(outs, axis=1).astype(o_ref.dtype)


def nsa_select_window(q_rot, ks, vs, kw, vw, selmask, o_cmp, gates, *, qpos0, kwpos0, o_sel=None):
    b, tqn, _ = q_rot.shape
    sel_given = o_sel is not None
    lw = _kv_len(kw)
    lp, w = (lw, LANES) if sel_given else (_kv_len(ks), selmask.shape[3])
    tq = _pick(tqn, 256)
    tk = _kv_tile(lp, tq)
    tkw = _kv_tile(lw, tq)
    kern = functools.partial(_nsa_sw_kernel, tq=tq, tk=tk, tkw=tkw, qpos0=qpos0, lp=lp, lw=lw, kwpos0=kwpos0, w=w,
                             sel_given=sel_given)
    qspec = pl.BlockSpec((None, tq, GW), lambda bi, g, i: (bi, i, g))
    gspec = pl.BlockSpec((None, None, tq, LANES), lambda bi, g, i: (bi, g, i, 0))
    bg = lambda bi, g, i: (bi, g)
    if sel_given:
        in_specs = [qspec, _kv_spec(kw, bg), _kv_spec(vw, bg), qspec, qspec, gspec]
        args = (q_rot, kw, vw, o_sel, o_cmp, gates)
    else:
        in_specs = [qspec, _kv_spec(ks, bg), _kv_spec(vs, bg), _kv_spec(kw, bg), _kv_spec(vw, bg),
                    pl.BlockSpec((None, None, tq, w), lambda bi, g, i: (bi, g, i, 0)), qspec, gspec]
        args = (q_rot, ks, vs, kw, vw, selmask, o_cmp, gates)
    return pl.pallas_call(
        kern, grid=(b, GROUPS, tqn // tq), in_specs=in_specs, out_specs=qspec,
        out_shape=jax.ShapeDtypeStruct(q_rot.shape, BF16),
        scratch_shapes=_softmax_scratch(2, HEADS_PER_GROUP * tq),
        compiler_params=_cparams("parallel", "parallel", "parallel"), name="nsa_select_window",
    )(*args)


def _ik_kernel(t_ref, g_ref, c_ref, u_ref, d_ref, ikf_ref, ik2_ref, *, half):
    x = t_ref[...]
    lane = lax.broadcasted_iota(I32, (1, LANES), 1)
    xm = jnp.where(lane < IDX_DIM, x, 0.0)
    ms = jnp.sum(xm * xm, axis=-1, keepdims=True) * (1.0 / IDX_DIM)
    y = xm * lax.rsqrt(ms + EPS) * g_ref[...]
    y = _rope_chunk(y, c_ref[...], u_ref[...], d_ref[...], half)
    ikf_ref[...] = y
    ik2_ref[...] = (y + pltpu.roll(y, IDX_DIM, 1)).astype(ik2_ref.dtype)


def index_keys(tail, knorm, tabs):
    t = tail.shape[0]
    tables, half = tabs
    tm = _pick(tables[0].shape[0], 512)
    npb = tables[0].shape[0] // tm
    g = jnp.concatenate([knorm.astype(F32), jnp.zeros((LANES - IDX_DIM,), F32)]).reshape(1, LANES)
    rspec = pl.BlockSpec((tm, LANES), lambda i: (i, 0))
    tspec = pl.BlockSpec((tm, LANES), lambda i: (i % npb, 0))
    return pl.pallas_call(
        functools.partial(_ik_kernel, half=half), grid=(t // tm,),
        in_specs=[rspec, pl.BlockSpec((1, LANES), lambda i: (0, 0)), tspec, tspec, tspec],
        out_specs=[rspec, rspec],
        out_shape=[jax.ShapeDtypeStruct((t, LANES), F32), jax.ShapeDtypeStruct((t, LANES), BF16)],
        compiler_params=_cparams("parallel"), name="index_keys",
    )(tail, g, *tables)


def _dsa_index_kernel(iq_ref, ik_ref, tail_ref, mask_ref, sc_sc, *, tq, tk, qpos0, lp, top, n_widths):
    i = pl.program_id(1)
    iq = iq_ref[...]
    lane = lax.broadcasted_iota(I32, (1, LANES), 1)
    lo = lane < IDX_DIM
    hi = jnp.logical_not(lo)
    zero = jnp.zeros((), iq.dtype)
    parts = []
    for h in range(IDX_HEADS):
        pair = iq[:, (h // 2) * LANES:(h // 2 + 1) * LANES]
        parts.append(jnp.where(lo if h % 2 == 0 else hi, pair, zero))
    iqst = _concat_rows(parts)
    iw = tail_ref[:, IDX_DIM:IDX_DIM + IDX_HEADS] * ((IDX_DIM ** -0.5) * (IDX_HEADS ** -0.5))
    qpos = qpos0 + i * tq + lax.broadcasted_iota(I32, (tq, 1), 0)
    sc_sc[...] = jnp.full(sc_sc.shape, NEG_INF, F32)

    def body(j, carry):
        off = pl.multiple_of(j * tk, tk)
        sc = _qk(iqst, ik_ref[pl.ds(off, tk), :])
        acc = jnp.zeros((tq, tk), F32)
        for h in range(IDX_HEADS):
            acc = acc + jnp.maximum(sc[h * tq:(h + 1) * tq], 0.0) * iw[:, h:h + 1]
        kpos = off + lax.broadcasted_iota(I32, (1, tk), 1)
        sc_sc[:, pl.ds(off, tk)] = jnp.where(kpos <= qpos, acc, NEG_INF)
        return carry

    lax.fori_loop(0, _n_kv_tiles(qpos0, i, tq, tk, lp), body, 0)
    key = _order_key(sc_sc[...])
    quarter = lp // 4
    if n_widths > 1 and quarter % LANES == 0 and quarter > top:
        reach = jnp.clip((qpos0 + (i + 1) * tq + quarter - 1) // quarter, 1, 4)
        thr = lax.switch(reach - 1, [functools.partial(_kth_largest_key, key[:, :quarter * n], top)
                                     for n in range(1, 5)])
    else:
        thr = _kth_largest_key(key, top)
    kpos_all = lax.broadcasted_iota(I32, (1, lp), 1)
    mask_ref[...] = jnp.where((key >= thr) & (kpos_all <= qpos), 1.0, 0.0).astype(mask_ref.dtype)


def dsa_select(iq, ik2, tail, *, qpos0, top):
    b, tqn, _ = iq.shape
    lp = ik2.shape[1]
    tq = _pick(tqn, 128)
    tk = _kv_tile(lp, tq)
    kern = functools.partial(_dsa_index_kernel, tq=tq, tk=tk, qpos0=qpos0, lp=lp, top=top,
                             n_widths=4 if tqn > tq else 1)
    return pl.pallas_call(
        kern, grid=(b, tqn // tq),
        in_specs=[pl.BlockSpec((None, tq, IDX_HEADS * IDX_DIM), lambda bi, i: (bi, i, 0)),
                  pl.BlockSpec((None, lp, LANES), lambda bi, i: (bi, 0, 0)),
                  pl.BlockSpec((None, tq, LANES), lambda bi, i: (bi, i, 0))],
        out_specs=pl.BlockSpec((None, tq, lp), lambda bi, i: (bi, i, 0)),
        out_shape=jax.ShapeDtypeStruct((b, tqn, lp), BF16),
        scratch_shapes=[pltpu.VMEM((tq, lp), F32)],
        compiler_params=_cparams("parallel", "parallel"), name="dsa_select",
    )(iq, ik2, tail)


def _masked_attn_kernel(q_ref, k_ref, v_ref, mask_ref, o_ref, m_sc, l_sc, acc_sc, *, tq, tk, qpos0, lp):
    i = pl.program_id(1)
    qst = _stack_heads(q_ref[...])
    _init_state(m_sc, l_sc, acc_sc)
    scale = (C_HD ** -0.5) * LOG2E

    def body(j, carry):
        off = pl.multiple_of(j * tk, tk)
        kt = k_ref[pl.ds(off, tk), :]
        vt = v_ref[pl.ds(off, tk), :]
        mask = mask_ref[:, pl.ds(off, tk)] > 0.5
        _online_update(0, _mask_rows(mask, _qk(qst, kt) * scale, tq), vt, m_sc, l_sc, acc_sc)
        return carry

    lax.fori_loop(0, _n_kv_tiles(qpos0, i, tq, tk, lp), body, 0)
    o_ref[...] = _unstack_heads(acc_sc[0] / l_sc[0], tq).astype(o_ref.dtype)


def masked_attention(q, k, v, mask, *, qpos0):
    b, tqn, _ = q.shape
    lp = _kv_len(k)
    tq = _pick(tqn, 256)
    tk = _kv_tile(lp, tq)
    kern = functools.partial(_masked_attn_kernel, tq=tq, tk=tk, qpos0=qpos0, lp=lp)
    qspec = pl.BlockSpec((None, tq, GW), lambda bi, i, g: (bi, i, g))
    bg = lambda bi, i, g: (bi, g)
    return pl.pallas_call(
        kern, grid=(b, tqn // tq, GROUPS),
        in_specs=[qspec, _kv_spec(k, bg), _kv_spec(v, bg),
                  pl.BlockSpec((None, tq, lp), lambda bi, i, g: (bi, i, 0))],
        out_specs=qspec,
        out_shape=jax.ShapeDtypeStruct(q.shape, BF16),
        scratch_shapes=_softmax_scratch(1, HEADS_PER_GROUP * tq),
        compiler_params=_cparams("parallel", "parallel", "parallel"), name="masked_attention",
    )(q, k, v, mask)


def _cross_kernel(q_ref, k_ref, v_ref, o_ref):
    q = q_ref[...]
    k = k_ref[...].astype(BF16)
    v = v_ref[...].astype(BF16)
    outs = []
    for h in range(X_HEADS):
        cs = slice(h * X_HD, (h + 1) * X_HD)
        s = _qk(q[:, cs], k[:, cs]) * (X_HD ** -0.5)
        e = jnp.exp(s - jnp.max(s, axis=-1, keepdims=True))
        p = e / jnp.sum(e, axis=-1, keepdims=True)
        outs.append(jnp.dot(p.astype(BF16), v[:, cs], preferred_element_type=F32))
    o_ref[...] = jnp.concatenate(outs, axis=1).astype(o_ref.dtype)


def cross_attention(q, mk, mv):
    b, tqn, _ = q.shape
    tq = _pick(tqn, 512)
    ml = mk.shape[1]
    qspec = pl.BlockSpec((None, tq, X_W), lambda bi, i: (bi, i, 0))
    mspec = pl.BlockSpec((None, ml, X_W), lambda bi, i: (bi, 0, 0))
    return pl.pallas_call(
        _cross_kernel, grid=(b, tqn // tq), in_specs=[qspec, mspec, mspec], out_specs=qspec,
        out_shape=jax.ShapeDtypeStruct(q.shape, BF16),
        compiler_params=_cparams("parallel", "parallel"), name="cross_attention",
    )(q, mk, mv)


PAGES_PER_STEP = 16


def _gather_kernel(pt_ref, *refs, pps):
    pools, new_ref, o_ref = refs[:pps], refs[pps], refs[pps + 1]
    s = pl.program_id(1)
    n_steps = pl.num_programs(1) - 1

    @pl.when(s < n_steps)
    def _():
        for k in range(pps):
            x = pools[k][...].astype(o_ref.dtype)
            o_ref[k * PAGE_SIZE:(k + 1) * PAGE_SIZE, :] = jnp.concatenate([x, x], axis=1)

    @pl.when(s == n_steps)
    def _():
        o_ref[...] = jnp.zeros(o_ref.shape, o_ref.dtype)
        o_ref[0:PAGE_SIZE, :] = new_ref[...]


def gather_index_keys(pool, layer, page_table, new_rows):
    b, n_pages = page_table.shape
    pps = math.gcd(PAGES_PER_STEP, n_pages)
    n_steps = n_pages // pps
    rows = pps * PAGE_SIZE
    grid_spec = pltpu.PrefetchScalarGridSpec(
        num_scalar_prefetch=1, grid=(b, n_steps + 1),
        in_specs=_page_specs(pool, layer, pps, n_pages)
        + [pl.BlockSpec((None, PAGE_SIZE, LANES), lambda bi, s, pt: (bi, 0, 0))],
        out_specs=pl.BlockSpec((None, rows, LANES), lambda bi, s, pt: (bi, s, 0)))
    return pl.pallas_call(
        functools.partial(_gather_kernel, pps=pps), grid_spec=grid_spec,
        out_shape=jax.ShapeDtypeStruct((b, (n_steps + 1) * rows, LANES), BF16),
        compiler_params=_cparams("parallel", "arbitrary"), name="gather_index_keys",
    )(page_table, *([pool] * pps), new_rows)


def _paged_steps(pool, page_table):
    n_pages = page_table.shape[1]
    view = pool.reshape(pool.shape[0], pool.shape[1], PAGE_SIZE * GROUPS, HD)
    pps = math.gcd(PAGES_PER_STEP, n_pages)
    return view, pps, n_pages // pps


def _page_specs(view, layer, pps, n_pages):
    def spec(k):
        return pl.BlockSpec((None, None) + view.shape[2:],
                            lambda bi, s, pt: (layer, pt[bi, jnp.minimum(s * pps + k, n_pages - 1)], 0, 0))
    return [spec(k) for k in range(pps)]


def _group_rows(page_refs, g):
    return jnp.concatenate([p[pl.ds(g, PAGE_SIZE, stride=GROUPS), :].astype(BF16) for p in page_refs], axis=0)


def _paged_decode_kernel(pt_ref, *refs, pps, mode, qpos0, past_len, post):
    it = iter(refs)
    lam_ref = next(it) if mode == 'diff' else None
    q_ref = next(it)
    kpools = [next(it) for _ in range(pps)]
    vpools = [next(it) for _ in range(pps)]
    knew_ref, vnew_ref = next(it), next(it)
    mask_ref = next(it) if mode != 'diff' else None
    g_ref = next(it) if mode == 'diff' else None
    o_ref, m_sc, l_sc, acc_sc = next(it), next(it), next(it), next(it)

    step = pl.program_id(1)
    n_steps = pl.num_programs(1) - 1
    tq = q_ref.shape[0]
    rows = pps * PAGE_SIZE
    scale = ((A_HALF if mode == 'diff' else HD) ** -0.5) * LOG2E
    qpos = qpos0 + lax.broadcasted_iota(I32, (tq, 1), 0)

    @pl.when(step == 0)
    def _():
        _init_state(m_sc, l_sc, acc_sc)

    def queries(g):
        q = q_ref[:, g * GW:(g + 1) * GW]
        if mode != 'diff':
            return _stack_heads(q)
        lo = lax.broadcasted_iota(I32, (1, HD), 1) < A_HALF
        return _concat_rows([_stack_heads(q, lo), _stack_heads(q, jnp.logical_not(lo))])

    def attend(g, kt, vt, mask):
        s = _qk(queries(g), kt) * scale
        if mask is not None:
            s = _mask_rows(mask, s, tq)
        _online_update(g, s, vt, m_sc, l_sc, acc_sc)

    @pl.when(step < n_steps)
    def _():
        mask = mask_ref[...] > 0.5 if mode == 'mask' else None
        if mode == 'chunks':
            assert rows // CMP_STRIDE == LANES
            kpos = step * rows + lax.broadcasted_iota(I32, (1, rows), 1)
            chunk_row = step * LANES + lax.broadcasted_iota(I32, (LANES, 1), 0)
            expand = jnp.where(chunk_row == (kpos >> 4), 1.0, 0.0).astype(BF16)
        for g in range(GROUPS):
            if mode == 'chunks':
                mask = jnp.dot(mask_ref[g], expand, preferred_element_type=F32) > 0.5
            attend(g, _group_rows(kpools, g), _group_rows(vpools, g), mask)

    @pl.when(step == n_steps)
    def _():
        kpos = past_len + lax.broadcasted_iota(I32, (1, PAGE_SIZE), 1)
        mask = kpos <= qpos
        if mode == 'mask':
            mask = mask & (mask_ref[:, 0:PAGE_SIZE] > 0.5)
        for g in range(GROUPS):
            attend(g, knew_ref[g], vnew_ref[g], mask)
        for g in range(GROUPS):
            if mode == 'diff':
                on = acc_sc[g] / l_sc[g]
                o = on[:HEADS_PER_GROUP * tq] - lam_ref[0] * on[HEADS_PER_GROUP * tq:]
                ms = jnp.mean(o * o, axis=-1, keepdims=True)
                o = o * lax.rsqrt(ms + EPS) * g_ref[...] * post
            else:
                o = acc_sc[g] / l_sc[g]
            o_ref[:, g * GW:(g + 1) * GW] = _unstack_heads(o, tq).astype(o_ref.dtype)


def paged_decode_attention(q, kpool, vpool, layer, page_table, k_new, v_new, *, mode, qpos0, out_dtype=BF16,
                           mask=None, lam_f=None, subln=None, lam_init=0.0):
    b, tq, _ = q.shape
    n_pages = page_table.shape[1]
    kview, pps, n_steps = _paged_steps(kpool, page_table)
    vview, _, _ = _paged_steps(vpool, page_table)
    rows = pps * PAGE_SIZE
    n_c = 2 if mode == 'diff' else 1
    in_specs, args = [], []
    if mode == 'diff':
        in_specs.append(pl.BlockSpec(memory_space=pltpu.SMEM))
        args.append(lam_f.reshape(1).astype(F32))
    in_specs.append(pl.BlockSpec((None, tq, D_MODEL), lambda bi, s, pt: (bi, 0, 0)))
    args.append(q)
    in_specs += _page_specs(kview, layer, pps, n_pages) + _page_specs(vview, layer, pps, n_pages)
    args += [kview] * pps + [vview] * pps
    new_spec = pl.BlockSpec((None, GROUPS, PAGE_SIZE, HD), lambda bi, s, pt: (bi, 0, 0, 0))
    in_specs += [new_spec, new_spec]
    args += [k_new, v_new]
    if mode == 'mask':
        in_specs.append(pl.BlockSpec((None, tq, rows), lambda bi, s, pt: (bi, 0, s)))
        args.append(mask)
    elif mode == 'chunks':
        in_specs.append(pl.BlockSpec((None, GROUPS, tq, LANES), lambda bi, s, pt: (bi, 0, 0, s)))
        args.append(mask)
    else:
        in_specs.append(pl.BlockSpec((1, HD), lambda bi, s, pt: (0, 0)))
        args.append(subln.reshape(1, HD))
    kern = functools.partial(_paged_decode_kernel, pps=pps, mode=mode, qpos0=qpos0,
                             past_len=n_pages * PAGE_SIZE, post=1.0 - lam_init)
    grid_spec = pltpu.PrefetchScalarGridSpec(
        num_scalar_prefetch=1, grid=(b, n_steps + 1), in_specs=in_specs,
        out_specs=pl.BlockSpec((None, tq, D_MODEL), lambda bi, s, pt: (bi, 0, 0)),
        scratch_shapes=_softmax_scratch(GROUPS, n_c * HEADS_PER_GROUP * tq))
    return pl.pallas_call(
        kern, grid_spec=grid_spec, out_shape=jax.ShapeDtypeStruct(q.shape, out_dtype),
        compiler_params=_cparams("parallel", "arbitrary"), name="paged_decode_attention",
    )(page_table, *args)


def _paged_compress_kernel(pt_ref, *refs, pps):
    pools, w_ref, o_ref = refs[:pps], refs[pps], refs[pps + 1]
    cpp = PAGE_SIZE // CMP_STRIDE
    acc = jnp.zeros((GROUPS * pps * cpp, 2 * HD), F32)
    for j in range(CMP_STRIDE):
        x = jnp.concatenate([p[pl.ds(GROUPS * j + g, cpp, stride=GROUPS * CMP_STRIDE), :]
                             for g in range(GROUPS) for p in pools], axis=0)
        acc = acc + jnp.dot(x.astype(BF16), w_ref[j], preferred_element_type=F32)
    for g in range(GROUPS):
        o_ref[g] = acc[g * pps * cpp:(g + 1) * pps * cpp]


def paged_compress(pool, layer, page_table, w):
    b, n_pages = page_table.shape
    view, pps, n_steps = _paged_steps(pool, page_table)
    cps = pps * (PAGE_SIZE // CMP_STRIDE)
    grid_spec = pltpu.PrefetchScalarGridSpec(
        num_scalar_prefetch=1, grid=(b, n_steps),
        in_specs=_page_specs(view, layer, pps, n_pages) + [pl.BlockSpec(w.shape, lambda bi, s, pt: (0, 0, 0))],
        out_specs=pl.BlockSpec((None, GROUPS, cps, 2 * HD), lambda bi, s, pt: (bi, 0, s, 0)))
    return pl.pallas_call(
        functools.partial(_paged_compress_kernel, pps=pps), grid_spec=grid_spec,
        out_shape=jax.ShapeDtypeStruct((b, GROUPS, n_steps * cps, 2 * HD), F32),
        compiler_params=_cparams("parallel", "parallel"), name="paged_compress",
    )(page_table, *([view] * pps), w)


def lambda_init(layer):
    return 0.8 - 0.6 * math.exp(-0.3 * layer)


def _pad_rows(x, n):
    return jnp.pad(x, ((0, 0), (0, n - x.shape[1]), (0, 0)))


def trunk(x, nb, tq_real, qpos0, mem_k, mem_v, W, past):
    t = nb * tq_real
    tqp = max(tq_real, SUBLANES)
    pos = qpos0 + jnp.arange(tq_real, dtype=I32)
    pos_rows = jnp.tile(pos, nb) if tq_real < SUBLANES else pos
    tab64 = rope_tables(pos_rows, 64)
    tab128 = rope_tables(pos_rows, 128)
    new = {}

    def to_attn(a):
        return _pad_rows(a.reshape(nb, tq_real, a.shape[-1]), tqp)

    def from_attn(a):
        return a[:, :tq_real].reshape(t, a.shape[-1])

    if past is not None:
        pt = past['page_table']
        past_len = pt.shape[1] * PAGE_SIZE

        def new_block(new_bf):
            nr = new_bf.reshape(nb, tq_real, GROUPS, HD).transpose(0, 2, 1, 3)
            return jnp.pad(nr, ((0, 0), (0, 0), (0, PAGE_SIZE - tq_real), (0, 0)))
    else:
        past_len = 0

    for l in range(DEPTH):
        x = yield from _ffn_steps(x, W['norm_ffn1'], W['ffn1_w_in'], W['ffn1_w_out'], l)
        h = rmsnorm(x, W['norm_mix'][l], BF16)
        i = l // N_MIXERS
        kind = l % N_MIXERS
        if kind == 0:
            proj = yield (linear, h, None, dict(w=W['a_w_in'], layer=i))
            (q_rot,) = take_cols(proj, 0, A_Q, (BF16,), tab64)
            k_f, k_b = take_cols(proj, A_Q, A_K, (F32, BF16), tab64)
            (v_b,) = take_cols(proj, A_Q + A_K, A_KV_HEADS * A_VDIM, (BF16,))
            v_f = proj[:, A_Q + A_K:]
            new.setdefault('a_k', []).append(k_f.reshape(nb, tq_real, A_KV_HEADS, 2 * A_HALF))
            new.setdefault('a_v', []).append(v_f.reshape(nb, tq_real, A_KV_HEADS, A_VDIM))
            lam = W['a_lambda'][i]
            lam_f = (jnp.exp(jnp.sum(lam[0] * lam[1])) - jnp.exp(jnp.sum(lam[2] * lam[3]))).astype(F32) + lambda_init(l)
            if past is None:
                o = diff_attention(to_attn(q_rot), k_b.reshape(nb, tq_real, -1), v_b.reshape(nb, tq_real, -1),
                                   lam_f, W['a_subln'][i], lambda_init(l), qpos0)
            else:
                o = paged_decode_attention(to_attn(q_rot), past['a_k'], past['a_v'], i, pt, new_block(k_b),
                                           new_block(v_b), mode='diff', qpos0=qpos0, lam_f=lam_f,
                                           subln=W['a_subln'][i], lam_init=lambda_init(l))
            y_in, w_out = from_attn(o), W['a_w_out']
        elif kind == 1:
            w_in = W['b_w_in']
            n_main = B_Q + 6 * B_KV
            proj = yield (linear, h, None, dict(w=w_in, layer=i, ncols=n_main))
            w_tail = jnp.pad(w_in[i, :, n_main:], ((0, 0), (0, LANES - (w_in.shape[-1] - n_main))))
            gate_logits = yield (linear, h, None, dict(w=w_tail, tn=LANES))
            (q_raw,) = take_cols(proj, 0, B_Q, (BF16,))
            (q_rot,) = take_cols(proj, 0, B_Q, (BF16,), tab128)
            kc_f = proj[:, B_Q:B_Q + B_KV]
            vc_f = proj[:, B_Q + B_KV:B_Q + 2 * B_KV]
            ks_f, ks_b = take_cols(proj, B_Q + 2 * B_KV, B_KV, (F32, BF16), tab128)
            vs_f = proj[:, B_Q + 3 * B_KV:B_Q + 4 * B_KV]
            (vs_b,) = take_cols(proj, B_Q + 3 * B_KV, B_KV, (BF16,))
            kw_f, kw_b = take_cols(proj, B_Q + 4 * B_KV, B_KV, (F32, BF16), tab128)
            vw_f = proj[:, B_Q + 5 * B_KV:B_Q + 6 * B_KV]
            (vw_b,) = take_cols(proj, B_Q + 5 * B_KV, B_KV, (BF16,))
            shp = (nb, tq_real, B_KV_HEADS, B_HD)
            for nm, a in (('b_cmp_k', kc_f), ('b_cmp_v', vc_f), ('b_sel_k', ks_f), ('b_sel_v', vs_f)):
                new.setdefault(nm, []).append(a.reshape(shp))

            cw = W['b_cmp_w'][i]
            wfs = [jnp.concatenate([cw[s, :CMP_STRIDE], cw[s, CMP_STRIDE:]], axis=-1) for s in range(2)]
            if past is None:
                l_all = tq_real
                ks_all, vs_all = ks_b.reshape(nb, tq_real, -1), vs_b.reshape(nb, tq_real, -1)
                kw_all, vw_all = kw_b.reshape(nb, tq_real, -1), vw_b.reshape(nb, tq_real, -1)
                kwpos0 = 0
                keep = min(WINDOW, tq_real)
                new.setdefault('b_win_k', []).append(kw_f.reshape(shp)[:, tq_real - keep:])
                new.setdefault('b_win_v', []).append(vw_f.reshape(shp)[:, tq_real - keep:])
                lp_sel = tq_real
            else:
                l_all = past_len + tq_real
                bkw = past['b_win_k'][i].reshape(nb, -1, B_KV)
                bvw = past['b_win_v'][i].reshape(nb, -1, B_KV)
                wb = bkw.shape[1]
                kb = jnp.concatenate([bkw, kw_f.reshape(nb, tq_real, B_KV)], axis=1)
                vb = jnp.concatenate([bvw, vw_f.reshape(nb, tq_real, B_KV)], axis=1)
                new.setdefault('b_win_k', []).append(kb[:, tq_real:].reshape(nb, wb, B_KV_HEADS, B_HD))
                new.setdefault('b_win_v', []).append(vb[:, tq_real:].reshape(nb, wb, B_KV_HEADS, B_HD))
                lw = -(-(wb + tq_real) // 512) * 512
                kw_all = _pad_rows(kb, lw).astype(BF16)
                vw_all = _pad_rows(vb, lw).astype(BF16)
                kwpos0 = qpos0 - wb
                lp_sel = past_len + math.gcd(PAGES_PER_STEP, pt.shape[1]) * PAGE_SIZE
            nch = (l_all // CMP_STRIDE)
            nblk = nch - 1
            n_sel = -(-l_all // SEL_BLOCK)
            n_rep = nch // (SEL_BLOCK // CMP_STRIDE)
            assert nch % (SEL_BLOCK // CMP_STRIDE) == 0 and n_sel - n_rep in (0, 1)
            n_top_rep = min(SEL_N, n_sel) - (n_sel - n_rep)
            wch = -(-(lp_sel // CMP_STRIDE) // LANES) * LANES

            def partials(rows, wf):
                xg = rows.reshape(nb, nch, CMP_STRIDE, B_KV_HEADS, B_HD)
                xg = xg.transpose(0, 3, 1, 2, 4).reshape(nb * B_KV_HEADS * nch, CMP_STRIDE * B_HD)
                return linear(xg, wf.reshape(CMP_STRIDE * B_HD, 2 * B_HD), tn=2 * B_HD).reshape(
                    nb, B_KV_HEADS, nch, 2 * B_HD)

            if past is None:
                fsk, fsv = partials(kc_f, wfs[0]), partials(vc_f, wfs[1])
            else:
                fsk = paged_compress(past['b_cmp_k'], i, pt, wfs[0].astype(BF16))
                fsv = paged_compress(past['b_cmp_v'], i, pt, wfs[1].astype(BF16))
            fsk, fsv = (jnp.pad(f, ((0, 0), (0, 0), (0, wch - nch), (0, 0))) for f in (fsk, fsv))
            o_cmp, selmask = nsa_compressed(to_attn(q_raw), fsk, fsv, W['b_cmp_b'][i].astype(F32),
                                            nblk=nblk, nch=nch, qpos0=qpos0, n_top_rep=n_top_rep)
            gates = gate_logits[:, :3 * B_HEADS].reshape(nb, tq_real, B_KV_HEADS, 3 * HEADS_PER_GROUP)
            gates = jnp.pad(gates.transpose(0, 2, 1, 3),
                            ((0, 0), (0, 0), (0, tqp - tq_real), (0, LANES - 3 * HEADS_PER_GROUP)))
            if past is None:
                o = nsa_select_window(to_attn(q_rot), ks_all, vs_all, kw_all, vw_all, selmask, o_cmp, gates,
                                      qpos0=qpos0, kwpos0=kwpos0)
            else:
                o_sel = paged_decode_attention(to_attn(q_rot), past['b_sel_k'], past['b_sel_v'], i, pt,
                                               new_block(ks_b), new_block(vs_b), mode='chunks', qpos0=qpos0,
                                               mask=selmask, out_dtype=F32)
                o = nsa_select_window(to_attn(q_rot), None, None, kw_all, vw_all, None, o_cmp, gates,
                                      qpos0=qpos0, kwpos0=kwpos0, o_sel=o_sel)
            y_in, w_out = from_attn(o), W['b_w_out']
        else:
            w_in = W['c_w_in']
            n_main = C_Q + 2 * C_KV + IDX_HEADS * IDX_DIM
            proj = yield (linear, h, None, dict(w=w_in, layer=i, ncols=n_main))
            w_tail = jnp.pad(w_in[i, :, n_main:], ((0, 0), (0, LANES - (w_in.shape[-1] - n_main))))
            tail = yield (linear, h, None, dict(w=w_tail, tn=LANES))
            (q_rot,) = take_cols(proj, 0, C_Q, (BF16,), tab128)
            k_f, k_b = take_cols(proj, C_Q, C_KV, (F32, BF16), tab128)
            v_f = proj[:, C_Q + C_KV:C_Q + 2 * C_KV]
            (v_b,) = take_cols(proj, C_Q + C_KV, C_KV, (BF16,))
            (iq_rot,) = take_cols(proj, C_Q + 2 * C_KV, IDX_HEADS * IDX_DIM, (BF16,), tab64)
            ik_f, ik2 = index_keys(tail, W['c_idx_knorm'][i], tab64)
            shp = (nb, tq_real, C_KV_HEADS, C_HD)
            new.setdefault('c_k', []).append(k_f.reshape(shp))
            new.setdefault('c_v', []).append(v_f.reshape(shp))
            new.setdefault('c_idx_k', []).append(ik_f[:, :IDX_DIM].reshape(nb, tq_real, IDX_DIM))
            l_all = past_len + tq_real
            top = min(IDX_TOPK_MAX, l_all // 4)
            if past is None:
                mask = dsa_select(to_attn(iq_rot), ik2.reshape(nb, tq_real, LANES), to_attn(tail), qpos0=qpos0, top=top)
                o = masked_attention(to_attn(q_rot), k_b.reshape(nb, tq_real, -1), v_b.reshape(nb, tq_real, -1),
                                     mask, qpos0=qpos0)
            else:
                ik_all = gather_index_keys(past['c_idx_k'], i, pt,
                                           _pad_rows(ik2.reshape(nb, tq_real, LANES), PAGE_SIZE))
                mask = dsa_select(to_attn(iq_rot), ik_all, to_attn(tail), qpos0=qpos0, top=top)
                o = paged_decode_attention(to_attn(q_rot), past['c_k'], past['c_v'], i, pt, new_block(k_b),
                                           new_block(v_b), mode='mask', qpos0=qpos0, mask=mask)
            y_in, w_out = from_attn(o), W['c_w_out']
        x = yield (linear, y_in, x, dict(w=w_out, layer=i, scale=1.0))

        h = rmsnorm(x, W['norm_cross'][l], BF16)
        qx = yield (linear, h, None, dict(w=W['x_w_q'], layer=l, out_dtype=BF16))
        ox = cross_attention(to_attn(qx), mem_k[l], mem_v[l])
        x = yield (linear, from_attn(ox), x, dict(w=W['x_w_o'], layer=l, scale=1.0))
        x = yield from _ffn_steps(x, W['norm_ffn2'], W['ffn2_w_in'], W['ffn2_w_out'], l)
    y = rmsnorm(x, W['final_norm'], F32)
    return y, {nm: jnp.stack(v) for nm, v in new.items()}


def kernel(x_prompt, x_sample, cache_a_k, cache_a_v, cache_b_cmp_k, cache_b_cmp_v, cache_b_sel_k, cache_b_sel_v, state_b_win_k, state_b_win_v, cache_c_k, cache_c_v, cache_c_idx_k, cache_mem_k, cache_mem_v, page_table, mem_prompt, norm_ffn1, norm_mix, norm_cross, norm_ffn2, final_norm, ffn1_w_in, ffn1_w_out, ffn2_w_in, ffn2_w_out, x_w_q, x_w_kv, x_w_o, a_w_in, a_w_out, a_lambda, a_subln, b_w_in, b_w_out, b_cmp_w, b_cmp_b, c_w_in, c_w_out, c_idx_knorm):
    W = dict(norm_ffn1=norm_ffn1, norm_mix=norm_mix, norm_cross=norm_cross, norm_ffn2=norm_ffn2,
             final_norm=final_norm, ffn1_w_in=ffn1_w_in, ffn1_w_out=ffn1_w_out, ffn2_w_in=ffn2_w_in,
             ffn2_w_out=ffn2_w_out, x_w_q=x_w_q, x_w_o=x_w_o, a_w_in=a_w_in, a_w_out=a_w_out,
             a_lambda=a_lambda, a_subln=a_subln, b_w_in=b_w_in, b_w_out=b_w_out, b_cmp_w=b_cmp_w,
             b_cmp_b=b_cmp_b, c_w_in=c_w_in, c_w_out=c_w_out, c_idx_knorm=c_idx_knorm)
    nbp, seq, d = x_prompt.shape
    nbs, dseq, _ = x_sample.shape
    ml = mem_prompt.shape[1]

    mem2d = mem_prompt.reshape(nbp * ml, d)
    mkv = [linear(mem2d, x_w_kv, l) for l in range(DEPTH)]
    p_mem_k = jnp.stack([m[:, :X_W].reshape(nbp, ml, X_HEADS, X_HD) for m in mkv])
    p_mem_v = jnp.stack([m[:, X_W:].reshape(nbp, ml, X_HEADS, X_HD) for m in mkv])
    prompt = trunk(x_prompt.reshape(nbp * seq, d), nbp, seq, 0,
                   p_mem_k.reshape(DEPTH, nbp, ml, X_W), p_mem_v.reshape(DEPTH, nbp, ml, X_W), W, None)

    past = dict(a_k=cache_a_k, a_v=cache_a_v, b_cmp_k=cache_b_cmp_k, b_cmp_v=cache_b_cmp_v,
                b_sel_k=cache_b_sel_k, b_sel_v=cache_b_sel_v, b_win_k=state_b_win_k, b_win_v=state_b_win_v,
                c_k=cache_c_k, c_v=cache_c_v, c_idx_k=cache_c_idx_k, page_table=page_table)
    past_len = page_table.shape[1] * PAGE_SIZE
    sml = cache_mem_k.shape[2]
    sample = trunk(x_sample.reshape(nbs * dseq, d), nbs, dseq, past_len,
                   cache_mem_k.reshape(DEPTH, nbs, sml, X_W), cache_mem_v.reshape(DEPTH, nbs, sml, X_W), W, past)
    (y_p, ps), (y_s, ss) = _run_together([prompt, sample])

    return (y_p.reshape(nbp, seq, d), y_s.reshape(nbs, dseq, d),
            ps['a_k'], ps['a_v'], ps['b_cmp_k'], ps['b_cmp_v'], ps['b_sel_k'], ps['b_sel_v'],
            ps['b_win_k'], ps['b_win_v'], ps['c_k'], ps['c_v'], ps['c_idx_k'], p_mem_k, p_mem_v,
            ss['a_k'], ss['a_v'], ss['b_cmp_k'], ss['b_cmp_v'], ss['b_sel_k'], ss['b_sel_v'],
            ss['b_win_k'], ss['b_win_v'], ss['c_k'], ss['c_v'], ss['c_idx_k'])
```

```python
import functools
import math

import jax
import jax.numpy as jnp
from jax import lax
from jax.experimental import pallas as pl
from jax.experimental.pallas import tpu as pltpu

F32 = jnp.float32
BF16 = jnp.bfloat16
I32 = jnp.int32

D_MODEL = 2048
DEPTH = 4
PAGE_SIZE = 128
N_MIXERS = 3
ROPE_THETA = 500000.0
EPS = 1e-6
NEG_INF = -1e30
FORCE = 1e9
D_FF = 256 * math.ceil(8 * D_MODEL / 3 / 256)

A_HEADS = D_MODEL // 128
A_HALF = 64
A_VDIM = 128
A_KV_HEADS = 4
A_Q = A_HEADS * 2 * A_HALF
A_K = A_KV_HEADS * 2 * A_HALF

B_HEADS = D_MODEL // 128
B_HD = 128
B_KV_HEADS = 4
B_KV = B_KV_HEADS * B_HD
B_Q = B_HEADS * B_HD
CMP_STRIDE = 16
CMP_LEN = 32
SEL_BLOCK = 64
SEL_N = 16
WINDOW = 512

C_HEADS = D_MODEL // 128
C_HD = 128
C_KV_HEADS = 4
C_Q = C_HEADS * C_HD
C_KV = C_KV_HEADS * C_HD
IDX_HEADS = 16
IDX_DIM = 64
IDX_TOPK_MAX = 256

MEM_LEN = 256
X_HEADS = 4
X_HD = 128
X_W = X_HEADS * X_HD

GROUPS = 4
HEADS_PER_GROUP = 4
HD = 128
GW = HEADS_PER_GROUP * HD

LANES = 128
SUBLANES = 8
VMEM_LIMIT_BYTES = 56 * 1024 * 1024
M_INIT = -1e29
LOG2E = 1.4426950408889634
INT_MIN = -2147483648


def _cparams(*sem):
    return pltpu.CompilerParams(dimension_semantics=sem, vmem_limit_bytes=VMEM_LIMIT_BYTES)


def _pick(n, pref):
    if n <= pref:
        return n
    t = pref
    while n % t:
        t //= 2
    return t


def _rms_kernel(x_ref, g_ref, o_ref):
    x = x_ref[...]
    ms = jnp.mean(x * x, axis=-1, keepdims=True)
    o_ref[...] = (x * lax.rsqrt(ms + EPS) * g_ref[...]).astype(o_ref.dtype)


def rmsnorm(x, g, out_dtype):
    t, d = x.shape
    tm = _pick(t, 512)
    return pl.pallas_call(
        _rms_kernel,
        grid=(t // tm,),
        in_specs=[pl.BlockSpec((tm, d), lambda i: (i, 0)), pl.BlockSpec((1, d), lambda i: (0, 0))],
        out_specs=pl.BlockSpec((tm, d), lambda i: (i, 0)),
        out_shape=jax.ShapeDtypeStruct((t, d), out_dtype),
        compiler_params=_cparams("parallel"),
        name="rmsnorm",
    )(x, g.reshape(1, d))


def _dense_kernel(*refs, mode, scale, extra):
    it = iter(refs)
    a_ref = next(it)
    a2_ref = next(it) if extra else None
    w_refs = [next(it) for _ in range(2 if mode == 'swiglu' else 1)]
    r_ref = next(it) if mode == 'res' else None
    r2_ref = next(it) if mode == 'res' and extra else None
    o_ref = next(it)
    o2_ref = next(it) if extra else None
    wb_refs = [next(it) for _ in w_refs]

    def apply(a, r, out):
        a = a[...].astype(BF16)
        if mode == 'swiglu':
            g = jnp.dot(a, wb_refs[0][...], preferred_element_type=F32)
            u = jnp.dot(a, wb_refs[1][...], preferred_element_type=F32)
            y = g * jax.nn.sigmoid(g) * u
        else:
            y = jnp.dot(a, wb_refs[0][...], preferred_element_type=F32)
            if mode == 'res':
                y = r[...] + scale * y
        out[...] = y.astype(out.dtype)

    @pl.when(pl.program_id(1) == 0)
    def _():
        for w_ref, wb_ref in zip(w_refs, wb_refs):
            wb_ref[...] = w_ref[...].astype(BF16)
        if extra:
            apply(a2_ref, r2_ref, o2_ref)

    apply(a_ref, r_ref, o_ref)


def _w_spec(w, layer, k, tn, c0):
    if w.ndim == 3:
        return pl.BlockSpec((None, k, tn), lambda j, i: (layer, 0, j + c0))
    return pl.BlockSpec((k, tn), lambda j, i: (0, j + c0))


def _dense(a, w_specs, ws, *, mode, k, tn, ncols, out_dtype, res=None, scale=1.0, tm=None, name):
    pair = isinstance(a, (list, tuple))
    a_main, a_x = (a[0], a[1]) if pair else (a, None)
    r_main, r_x = (res[0], res[1]) if (pair and res is not None) else (res, None)
    m = a_main.shape[0]
    tm = _pick(m, (2048 if k <= D_MODEL else 512) if tm is None else tm)
    assert m % tm == 0 and ncols % tn == 0
    row_spec = pl.BlockSpec((tm, k), lambda j, i: (i, 0))
    o_spec = pl.BlockSpec((tm, tn), lambda j, i: (i, j))
    in_specs, args = [row_spec], [a_main]
    out_specs, out_shape = [o_spec], [jax.ShapeDtypeStruct((m, ncols), out_dtype)]
    if pair:
        mx = a_x.shape[0]
        x_out = pl.BlockSpec((mx, tn), lambda j, i: (0, j))
        in_specs.append(pl.BlockSpec((mx, k), lambda j, i: (0, 0)))
        args.append(a_x)
        out_specs.append(x_out)
        out_shape.append(jax.ShapeDtypeStruct((mx, ncols), out_dtype))
    in_specs += w_specs
    args += ws
    if mode == 'res':
        in_specs.append(o_spec)
        args.append(r_main)
        if pair:
            in_specs.append(x_out)
            args.append(r_x)
    outs = pl.pallas_call(
        functools.partial(_dense_kernel, mode=mode, scale=scale, extra=pair),
        grid=(ncols // tn, m // tm), in_specs=in_specs, out_specs=out_specs, out_shape=out_shape,
        scratch_shapes=[pltpu.VMEM((k, tn), BF16) for _ in w_specs],
        compiler_params=_cparams("parallel", "arbitrary"), name=name,
    )(*args)
    return list(outs) if pair else outs[0]


def linear(a, w, layer=0, *, col0=0, ncols=None, out_dtype=F32, tm=None, tn=512, res=None, scale=1.0):
    k = (a[0] if isinstance(a, (list, tuple)) else a).shape[1]
    ncols = w.shape[-1] - col0 if ncols is None else ncols
    tn = _pick(ncols, tn)
    assert col0 % tn == 0
    return _dense(a, [_w_spec(w, layer, k, tn, col0 // tn)], [w], mode='plain' if res is None else 'res', k=k, tn=tn,
                  ncols=ncols, out_dtype=F32 if res is not None else out_dtype, res=res, scale=scale, tm=tm,
                  name='linear' if res is None else 'linear_res')


def swiglu_in(a, w_in, layer):
    k = (a[0] if isinstance(a, (list, tuple)) else a).shape[1]
    tn = 512
    nj = D_FF // tn
    return _dense(a, [_w_spec(w_in, layer, k, tn, 0), _w_spec(w_in, layer, k, tn, nj)], [w_in, w_in], mode='swiglu',
                  k=k, tn=tn, ncols=D_FF, out_dtype=BF16, tm=1024, name='swiglu_in')


def _ffn_steps(x, g, w_in, w_out, layer):
    h = rmsnorm(x, g[layer], BF16)
    hid = yield (swiglu_in, h, None, dict(w_in=w_in, layer=layer))
    return (yield (linear, hid, x, dict(w=w_out, layer=layer, scale=0.5)))


def _run_together(gens):
    reqs = [next(g) for g in gens]
    results = [None] * len(gens)
    while True:
        fn, _, res0, kw = reqs[0]
        if res0 is not None:
            kw = dict(kw, res=[r[2] for r in reqs])
        outs = fn([r[1] for r in reqs], **kw)
        nxt = []
        for gi, (g, o) in enumerate(zip(gens, outs)):
            try:
                nxt.append(g.send(o))
            except StopIteration as stop:
                results[gi] = stop.value
        if not nxt:
            return results
        assert len(nxt) == len(gens)
        reqs = nxt


def rope_tables(pos, dh):
    rot = dh // 4
    half = rot // 2
    inv = ROPE_THETA ** (-jnp.arange(half, dtype=F32) / half)
    ang = pos.astype(F32)[:, None] * inv[None, :]
    cos, sin = jnp.cos(ang), jnp.sin(ang)
    p = pos.shape[0]
    rest1 = jnp.ones((p, dh - rot), F32)
    rest0 = jnp.zeros((p, dh - rot), F32)
    z = jnp.zeros((p, half), F32)
    c = jnp.concatenate([cos, cos, rest1], axis=1)
    up = jnp.concatenate([-sin, z, rest0], axis=1)
    dn = jnp.concatenate([z, sin, rest0], axis=1)
    rep = LANES // dh
    return tuple(jnp.tile(t, (1, rep)) for t in (c, up, dn)), half


def _rope_chunk(xc, c, up, dn, half):
    return xc * c + pltpu.roll(xc, LANES - half, 1) * up + pltpu.roll(xc, half, 1) * dn


def _cols_kernel(*refs, half, width, scale):
    if half:
        x_ref, c_ref, u_ref, d_ref = refs[:4]
        outs = refs[4:]
        c, up, dn = c_ref[...], u_ref[...], d_ref[...]
        x = x_ref[...]
        y = jnp.concatenate(
            [_rope_chunk(x[:, k * LANES:(k + 1) * LANES], c, up, dn, half) for k in range(width // LANES)], axis=1)
    else:
        x_ref = refs[0]
        outs = refs[1:]
        y = x_ref[...]
    if scale is not None:
        y = y * scale
    for o in outs:
        o[...] = y.astype(o.dtype)


def take_cols(x, col0, width, out_dtypes, tabs=None, scale=None):
    t = x.shape[0]
    bw = 512 if width % 512 == 0 else width
    assert col0 % bw == 0
    tables, half = tabs if tabs is not None else ((), 0)
    tm = _pick(t, 512) if tabs is None else _pick(tables[0].shape[0], 512)
    assert t % tm == 0
    npb = (tables[0].shape[0] // tm) if tabs is not None else 1
    c0 = col0 // bw
    in_specs = [pl.BlockSpec((tm, bw), lambda i, j: (i, j + c0))]
    in_specs += [pl.BlockSpec((tm, LANES), lambda i, j: (i % npb, 0)) for _ in tables]
    outs = pl.pallas_call(
        functools.partial(_cols_kernel, half=half, width=bw, scale=scale),
        grid=(t // tm, width // bw),
        in_specs=in_specs,
        out_specs=[pl.BlockSpec((tm, bw), lambda i, j: (i, j)) for _ in out_dtypes],
        out_shape=[jax.ShapeDtypeStruct((t, width), dt) for dt in out_dtypes],
        compiler_params=_cparams("parallel", "parallel"), name="take_cols",
    )(x, *tables)
    return outs


def _stack_heads(q, sel=None):
    parts = []
    for r in range(HEADS_PER_GROUP):
        qr = q[:, r * HD:(r + 1) * HD]
        if sel is not None:
            qr = jnp.where(sel, qr, jnp.zeros((), q.dtype))
        parts.append(qr)
    return _concat_rows(parts)


def _concat_rows(parts):
    if parts[0].shape[0] % (2 * SUBLANES):
        return jnp.concatenate([p.astype(F32) for p in parts], axis=0).astype(parts[0].dtype)
    return jnp.concatenate(parts, axis=0)


def _unstack_heads(o, tq):
    return jnp.concatenate([o[r * tq:(r + 1) * tq] for r in range(HEADS_PER_GROUP)], axis=1)


def _qk(q, k):
    return lax.dot_general(q, k, (((1,), (1,)), ((), ())), preferred_element_type=F32)


def _mask_rows(mask, s, tq):
    tk = s.shape[-1]
    return jnp.where(mask[None], s.reshape(-1, tq, tk), NEG_INF).reshape(s.shape)


def _logit_scale(dim):
    return (dim ** -0.5) * LOG2E


def _online_update(slot, s, vt, m_sc, l_sc, acc_sc):
    m_prev = m_sc[slot]
    m_new = jnp.maximum(m_prev, jnp.max(s, axis=-1, keepdims=True))
    alpha = jnp.exp2(m_prev - m_new)
    p = jnp.exp2(s - jnp.concatenate([m_new] * (s.shape[-1] // LANES), axis=1))
    l_sc[slot] = alpha * l_sc[slot] + jnp.sum(p, axis=-1, keepdims=True)
    acc_sc[slot] = alpha * acc_sc[slot] + jnp.dot(p.astype(BF16), vt, preferred_element_type=F32)
    m_sc[slot] = m_new


def _init_state(m_sc, l_sc, acc_sc):
    m_sc[...] = jnp.full(m_sc.shape, M_INIT, F32)
    l_sc[...] = jnp.zeros(l_sc.shape, F32)
    acc_sc[...] = jnp.zeros(acc_sc.shape, F32)


def _order_key(x):
    b = lax.bitcast_convert_type(x, I32)
    return jnp.where(b < 0, b ^ jnp.int32(0x7FFFFFFF), b)


def _kth_largest_key(key, k):
    kf = jnp.float32(k)

    def count_ge(t):
        return jnp.sum(jnp.where(key >= t, 1.0, 0.0), axis=-1, keepdims=True)

    t0 = jnp.where(count_ge(jnp.int32(0)) >= kf, jnp.int32(0), jnp.int32(INT_MIN))

    def one_bit(bit, t):
        cand = t | (jnp.int32(1) << bit)
        return jnp.where(count_ge(cand) >= kf, cand, t)

    if key.shape[0] * key.shape[1] > 128 * 1024:
        return lax.fori_loop(0, 31, lambda it, t: one_bit(jnp.int32(30) - it, t), t0)

    def two_bits(it, t):
        hi = jnp.int32(1) << (jnp.int32(30) - 2 * it)
        lo = jnp.int32(1) << (jnp.int32(29) - 2 * it)
        c1, c2, c3 = t | lo, t | hi, t | hi | lo
        return jnp.where(count_ge(c3) >= kf, c3,
                         jnp.where(count_ge(c2) >= kf, c2, jnp.where(count_ge(c1) >= kf, c1, t)))

    return one_bit(jnp.int32(0), lax.fori_loop(0, 15, two_bits, t0))


def _n_kv_tiles(qpos0, i, tq, tk, lp):
    hi = qpos0 + (i + 1) * tq
    return jnp.minimum((hi + tk - 1) // tk, lp // tk)


def _kv_tile(lp, tq):
    return _pick(lp, 512 if tq >= 64 else 2048)


def _kv_len(k):
    return k.shape[1] if k.ndim == 3 else k.shape[2]


def _kv_spec(k, bg):
    lp = _kv_len(k)
    if k.ndim == 3:
        return pl.BlockSpec((None, lp, HD), lambda *idx: (bg(*idx)[0], 0, bg(*idx)[1]))
    return pl.BlockSpec((None, None, lp, HD), lambda *idx: (bg(*idx)[0], bg(*idx)[1], 0, 0))


def _softmax_scratch(slots, rows):
    return [pltpu.VMEM((slots, rows, LANES), F32), pltpu.VMEM((slots, rows, LANES), F32),
            pltpu.VMEM((slots, rows, HD), F32)]


def _diff_attn_kernel(lam_ref, q_ref, k_ref, v_ref, g_ref, o_ref, m_sc, l_sc, acc_sc, *, tq, tk, qpos0, lp, post):
    i = pl.program_id(2)
    q = q_ref[...]
    lane = lax.broadcasted_iota(I32, (1, HD), 1)
    lo = lane < A_HALF
    qs = _concat_rows([_stack_heads(q, lo), _stack_heads(q, jnp.logical_not(lo))])
    qpos = qpos0 + i * tq + lax.broadcasted_iota(I32, (tq, 1), 0)
    _init_state(m_sc, l_sc, acc_sc)
    half = HEADS_PER_GROUP * tq

    def step(j, masked):
        off = pl.multiple_of(j * tk, tk)
        s = _qk(qs, k_ref[pl.ds(off, tk), :])
        if masked:
            s = _mask_rows(off + lax.broadcasted_iota(I32, (1, tk), 1) <= qpos, s, tq)
        _online_update(0, s, v_ref[pl.ds(off, tk), :], m_sc, l_sc, acc_sc)

    n_full = jnp.minimum((qpos0 + i * tq + 1) // tk, lp // tk)
    lax.fori_loop(0, n_full, lambda j, c: (step(j, False), c)[1], 0)
    lax.fori_loop(n_full, _n_kv_tiles(qpos0, i, tq, tk, lp), lambda j, c: (step(j, True), c)[1], 0)
    on = acc_sc[0] / l_sc[0]
    o = on[:half] - lam_ref[0] * on[half:]
    ms = jnp.mean(o * o, axis=-1, keepdims=True)
    o = o * lax.rsqrt(ms + EPS) * g_ref[...] * post
    o_ref[...] = _unstack_heads(o, tq).astype(o_ref.dtype)


def diff_attention(q, k, v, lam_f, subln, lam_init, qpos0):
    b, tqn, _ = q.shape
    lp = _kv_len(k)
    tq = _pick(tqn, 256)
    tk = _kv_tile(lp, tq)
    kern = functools.partial(_diff_attn_kernel, tq=tq, tk=tk, qpos0=qpos0, lp=lp, post=1.0 - lam_init)
    return pl.pallas_call(
        kern, grid=(b, GROUPS, tqn // tq),
        in_specs=[pl.BlockSpec(memory_space=pltpu.SMEM),
                  pl.BlockSpec((None, tq, GW), lambda bi, g, i: (bi, i, g)),
                  _kv_spec(k, lambda bi, g, i: (bi, g)),
                  _kv_spec(v, lambda bi, g, i: (bi, g)),
                  pl.BlockSpec((1, HD), lambda bi, g, i: (0, 0))],
        out_specs=pl.BlockSpec((None, tq, GW), lambda bi, g, i: (bi, i, g)),
        out_shape=jax.ShapeDtypeStruct(q.shape, BF16),
        scratch_shapes=_softmax_scratch(1, 2 * HEADS_PER_GROUP * tq),
        compiler_params=_cparams("parallel", "parallel", "parallel"), name="diff_attention",
    )(lam_f.reshape(1).astype(F32), q, k, v, subln.reshape(1, HD))


def _nsa_cmp_kernel(q_ref, fsk_ref, fsv_ref, b_ref, o_ref, sel_ref, *, tq, w, nblk, nch, qpos0, klanes):
    i = pl.program_id(2)
    qst = _stack_heads(q_ref[...])
    fsk = fsk_ref[...]
    fsv = fsv_ref[...]
    kcb = fsk[:, :HD] + pltpu.roll(fsk[:, HD:], w - 1, 0) + b_ref[0:1, :]
    vcb = fsv[:, :HD] + pltpu.roll(fsv[:, HD:], w - 1, 0) + b_ref[1:2, :]
    qpos = qpos0 + i * tq + lax.broadcasted_iota(I32, (tq, 1), 0)
    lanei = lax.broadcasted_iota(I32, (1, w), 1)
    s = _qk(qst, kcb.astype(BF16)) * (B_HD ** -0.5)
    valid = (lanei * CMP_STRIDE + (CMP_LEN - 1) <= qpos) & (lanei < nblk)
    s3 = jnp.where(valid[None], s.reshape(HEADS_PER_GROUP, tq, w), NEG_INF)
    e = jnp.exp(s3 - jnp.max(s3, axis=-1, keepdims=True))
    p = e / jnp.sum(e, axis=-1, keepdims=True)
    any_valid = (qpos >= CMP_LEN - 1) & (nblk > 0)
    p = jnp.where(any_valid[None], p, 0.0)
    o = jnp.dot(p.reshape(HEADS_PER_GROUP * tq, w).astype(BF16), vcb.astype(BF16), preferred_element_type=F32)
    o_ref[...] = _unstack_heads(o, tq).astype(o_ref.dtype)

    grp = p[0] + p[1] + p[2] + p[3]
    chunk = grp + jnp.where(lanei == 0, 0.0, pltpu.roll(grp, 1, 1))
    pair = jnp.where((lanei & 1) == 0, chunk + pltpu.roll(chunk, w - 1, 1), chunk + pltpu.roll(chunk, 1, 1))
    quad = jnp.where((lanei & 2) == 0, pair + pltpu.roll(pair, w - 2, 1), pair + pltpu.roll(pair, 2, 1))
    jb = lanei >> 2
    cur = qpos >> 6
    forced = (jb == 0) | (jb == cur) | (jb == cur - 1)
    score = jnp.where(forced, FORCE, jnp.where(jb * SEL_BLOCK <= qpos, quad, NEG_INF))
    score = jnp.where(lanei < nch, score, NEG_INF)
    key = _order_key(score)
    thr = _kth_largest_key(key, klanes)
    sel_ref[...] = jnp.where((key >= thr) | (lanei >= nch), 1.0, 0.0).astype(sel_ref.dtype)


def nsa_compressed(q_raw, fsk, fsv, bias, *, nblk, nch, qpos0, n_top_rep):
    b, tqn, _ = q_raw.shape
    w = fsk.shape[2]
    tq = _pick(tqn, 256)
    kern = functools.partial(_nsa_cmp_kernel, tq=tq, w=w, nblk=nblk, nch=nch, qpos0=qpos0,
                             klanes=(SEL_BLOCK // CMP_STRIDE) * n_top_rep)
    return pl.pallas_call(
        kern, grid=(b, GROUPS, tqn // tq),
        in_specs=[pl.BlockSpec((None, tq, GW), lambda bi, g, i: (bi, i, g)),
                  pl.BlockSpec((None, None, w, 2 * HD), lambda bi, g, i: (bi, g, 0, 0)),
                  pl.BlockSpec((None, None, w, 2 * HD), lambda bi, g, i: (bi, g, 0, 0)),
                  pl.BlockSpec((2, HD), lambda bi, g, i: (0, 0))],
        out_specs=[pl.BlockSpec((None, tq, GW), lambda bi, g, i: (bi, i, g)),
                   pl.BlockSpec((None, None, tq, w), lambda bi, g, i: (bi, g, i, 0))],
        out_shape=[jax.ShapeDtypeStruct(q_raw.shape, F32), jax.ShapeDtypeStruct((b, GROUPS, tqn, w), BF16)],
        compiler_params=_cparams("parallel", "parallel", "parallel"), name="nsa_compressed",
    )(q_raw, fsk, fsv, bias)


def _nsa_sw_kernel(*refs, tq, tk, tkw, qpos0, lp, lw, kwpos0, w, sel_given):
    if sel_given:
        q_ref, kw_ref, vw_ref, osel_ref, oc_ref, gate_ref, o_ref, m_sc, l_sc, acc_sc = refs
    else:
        q_ref, ks_ref, vs_ref, kw_ref, vw_ref, sel_ref, oc_ref, gate_ref, o_ref, m_sc, l_sc, acc_sc = refs
    i = pl.program_id(2)
    qst = _stack_heads(q_ref[...])
    qpos = qpos0 + i * tq + lax.broadcasted_iota(I32, (tq, 1), 0)
    _init_state(m_sc, l_sc, acc_sc)
    chunks_per_tile = tk // CMP_STRIDE
    assert LANES % chunks_per_tile == 0 and w % LANES == 0

    def sel_body(j, carry):
        off = pl.multiple_of(j * tk, tk)
        kt = ks_ref[pl.ds(off, tk), :]
        vt = vs_ref[pl.ds(off, tk), :]
        kpos = off + lax.broadcasted_iota(I32, (1, tk), 1)
        cbase = pl.multiple_of((j * chunks_per_tile) // LANES * LANES, LANES)
        chunk_row = cbase + lax.broadcasted_iota(I32, (LANES, 1), 0)
        expand = jnp.where(chunk_row == (kpos >> 4), 1.0, 0.0).astype(BF16)
        chosen = jnp.dot(sel_ref[:, pl.ds(cbase, LANES)], expand, preferred_element_type=F32) > 0.5
        mask = chosen & (kpos <= qpos)
        _online_update(0, _mask_rows(mask, _qk(qst, kt), tq), vt, m_sc, l_sc, acc_sc)
        return carry

    if not sel_given:
        lax.fori_loop(0, _n_kv_tiles(qpos0, i, tq, tk, lp), sel_body, 0)

    def win_body(j, carry):
        off = pl.multiple_of(j * tkw, tkw)
        kt = kw_ref[pl.ds(off, tkw), :]
        vt = vw_ref[pl.ds(off, tkw), :]
        kpos = kwpos0 + off + lax.broadcasted_iota(I32, (1, tkw), 1)
        dt = qpos - kpos
        mask = (dt >= 0) & (dt <= WINDOW)
        _online_update(1, _mask_rows(mask, _qk(qst, kt), tq), vt, m_sc, l_sc, acc_sc)
        return carry

    first = jnp.maximum(qpos0 + i * tq - WINDOW - kwpos0, 0) // tkw
    last = jnp.minimum((qpos0 + (i + 1) * tq - kwpos0 + tkw - 1) // tkw, lw // tkw)
    lax.fori_loop(first, last, win_body, 0)

    o_sel = osel_ref[...] if sel_given else _unstack_heads(acc_sc[0] / l_sc[0], tq)
    o_win = _unstack_heads(acc_sc[1] / l_sc[1], tq)
    gsig = jax.nn.sigmoid(gate_ref[...])
    oc = oc_ref[...]
    outs = []
    for r in range(HEADS_PER_GROUP):
        cs = slice(r * HD, (r + 1) * HD)
        outs.append(oc[:, cs] * gsig[:, 3 * r:3 * r + 1] + o_sel[:, cs] * gsig[:, 3 * r + 1:3 * r + 2]
                    + o_win[:, cs] * gsig[:, 3 * r + 2:3 * r + 3])
    o_ref[...] = jnp.concatenate(outs, axis=1).astype(o_ref.dtype)


def nsa_select_window(q_rot, ks, vs, kw, vw, selmask, o_cmp, gates, *, qpos0, kwpos0, o_sel=None):
    b, tqn, _ = q_rot.shape
    sel_given = o_sel is not None
    lw = _kv_len(kw)
    lp, w = (lw, LANES) if sel_given else (_kv_len(ks), selmask.shape[3])
    tq = _pick(tqn, 256)
    tk = _kv_tile(lp, tq)
    tkw = _kv_tile(lw, tq)
    kern = functools.partial(_nsa_sw_kernel, tq=tq, tk=tk, tkw=tkw, qpos0=qpos0, lp=lp, lw=lw, kwpos0=kwpos0, w=w,
                             sel_given=sel_given)
    qspec = pl.BlockSpec((None, tq, GW), lambda bi, g, i: (bi, i, g))
    gspec = pl.BlockSpec((None, None, tq, LANES), lambda bi, g, i: (bi, g, i, 0))
    bg = lambda bi, g, i: (bi, g)
    if sel_given:
        in_specs = [qspec, _kv_spec(kw, bg), _kv_spec(vw, bg), qspec, qspec, gspec]
        args = (q_rot, kw, vw, o_sel, o_cmp, gates)
    else:
        in_specs = [qspec, _kv_spec(ks, bg), _kv_spec(vs, bg), _kv_spec(kw, bg), _kv_spec(vw, bg),
                    pl.BlockSpec((None, None, tq, w), lambda bi, g, i: (bi, g, i, 0)), qspec, gspec]
        args = (q_rot, ks, vs, kw, vw, selmask, o_cmp, gates)
    return pl.pallas_call(
        kern, grid=(b, GROUPS, tqn // tq), in_specs=in_specs, out_specs=qspec,
        out_shape=jax.ShapeDtypeStruct(q_rot.shape, BF16),
        scratch_shapes=_softmax_scratch(2, HEADS_PER_GROUP * tq),
        compiler_params=_cparams("parallel", "parallel", "parallel"), name="nsa_select_window",
    )(*args)


def _ik_kernel(t_ref, g_ref, c_ref, u_ref, d_ref, ikf_ref, ik2_ref, *, half):
    x = t_ref[...]
    lane = lax.broadcasted_iota(I32, (1, LANES), 1)
    xm = jnp.where(lane < IDX_DIM, x, 0.0)
    ms = jnp.sum(xm * xm, axis=-1, keepdims=True) * (1.0 / IDX_DIM)
    y = xm * lax.rsqrt(ms + EPS) * g_ref[...]
    y = _rope_chunk(y, c_ref[...], u_ref[...], d_ref[...], half)
    ikf_ref[...] = y
    ik2_ref[...] = (y + pltpu.roll(y, IDX_DIM, 1)).astype(ik2_ref.dtype)


def index_keys(tail, knorm, tabs):
    t = tail.shape[0]
    tables, half = tabs
    tm = _pick(tables[0].shape[0], 512)
    npb = tables[0].shape[0] // tm
    g = jnp.concatenate([knorm.astype(F32), jnp.zeros((LANES - IDX_DIM,), F32)]).reshape(1, LANES)
    rspec = pl.BlockSpec((tm, LANES), lambda i: (i, 0))
    tspec = pl.BlockSpec((tm, LANES), lambda i: (i % npb, 0))
    return pl.pallas_call(
        functools.partial(_ik_kernel, half=half), grid=(t // tm,),
        in_specs=[rspec, pl.BlockSpec((1, LANES), lambda i: (0, 0)), tspec, tspec, tspec],
        out_specs=[rspec, rspec],
        out_shape=[jax.ShapeDtypeStruct((t, LANES), F32), jax.ShapeDtypeStruct((t, LANES), BF16)],
        compiler_params=_cparams("parallel"), name="index_keys",
    )(tail, g, *tables)


def _dsa_index_kernel(iq_ref, ik_ref, tail_ref, mask_ref, sc_sc, *, tq, tk, qpos0, lp, top, n_widths):
    i = pl.program_id(1)
    iq = iq_ref[...]
    lane = lax.broadcasted_iota(I32, (1, LANES), 1)
    lo = lane < IDX_DIM
    hi = jnp.logical_not(lo)
    zero = jnp.zeros((), iq.dtype)
    parts = []
    for h in range(IDX_HEADS):
        pair = iq[:, (h // 2) * LANES:(h // 2 + 1) * LANES]
        parts.append(jnp.where(lo if h % 2 == 0 else hi, pair, zero))
    iqst = _concat_rows(parts)
    iw = tail_ref[:, IDX_DIM:IDX_DIM + IDX_HEADS] * ((IDX_DIM ** -0.5) * (IDX_HEADS ** -0.5))
    qpos = qpos0 + i * tq + lax.broadcasted_iota(I32, (tq, 1), 0)
    sc_sc[...] = jnp.full(sc_sc.shape, NEG_INF, F32)

    def body(j, carry):
        off = pl.multiple_of(j * tk, tk)
        sc = _qk(iqst, ik_ref[pl.ds(off, tk), :])
        acc = jnp.zeros((tq, tk), F32)
        for h in range(IDX_HEADS):
            acc = acc + jnp.maximum(sc[h * tq:(h + 1) * tq], 0.0) * iw[:, h:h + 1]
        kpos = off + lax.broadcasted_iota(I32, (1, tk), 1)
        sc_sc[:, pl.ds(off, tk)] = jnp.where(kpos <= qpos, acc, NEG_INF)
        return carry

    lax.fori_loop(0, _n_kv_tiles(qpos0, i, tq, tk, lp), body, 0)
    key = _order_key(sc_sc[...])
    quarter = lp // 4
    if n_widths > 1 and quarter % LANES == 0 and quarter > top:
        reach = jnp.clip((qpos0 + (i + 1) * tq + quarter - 1) // quarter, 1, 4)
        thr = lax.switch(reach - 1, [functools.partial(_kth_largest_key, key[:, :quarter * n], top)
                                     for n in range(1, 5)])
    else:
        thr = _kth_largest_key(key, top)
    kpos_all = lax.broadcasted_iota(I32, (1, lp), 1)
    mask_ref[...] = jnp.where((key >= thr) & (kpos_all <= qpos), 1.0, 0.0).astype(mask_ref.dtype)


def dsa_select(iq, ik2, tail, *, qpos0, top):
    b, tqn, _ = iq.shape
    lp = ik2.shape[1]
    tq = _pick(tqn, 128)
    tk = _kv_tile(lp, tq)
    kern = functools.partial(_dsa_index_kernel, tq=tq, tk=tk, qpos0=qpos0, lp=lp, top=top,
                             n_widths=4 if tqn > tq else 1)
    return pl.pallas_call(
        kern, grid=(b, tqn // tq),
        in_specs=[pl.BlockSpec((None, tq, IDX_HEADS * IDX_DIM), lambda bi, i: (bi, i, 0)),
                  pl.BlockSpec((None, lp, LANES), lambda bi, i: (bi, 0, 0)),
                  pl.BlockSpec((None, tq, LANES), lambda bi, i: (bi, i, 0))],
        out_specs=pl.BlockSpec((None, tq, lp), lambda bi, i: (bi, i, 0)),
        out_shape=jax.ShapeDtypeStruct((b, tqn, lp), BF16),
        scratch_shapes=[pltpu.VMEM((tq, lp), F32)],
        compiler_params=_cparams("parallel", "parallel"), name="dsa_select",
    )(iq, ik2, tail)


def _masked_attn_kernel(q_ref, k_ref, v_ref, mask_ref, o_ref, m_sc, l_sc, acc_sc, *, tq, tk, qpos0, lp):
    i = pl.program_id(1)
    qst = _stack_heads(q_ref[...])
    _init_state(m_sc, l_sc, acc_sc)

    def body(j, carry):
        off = pl.multiple_of(j * tk, tk)
        kt = k_ref[pl.ds(off, tk), :]
        vt = v_ref[pl.ds(off, tk), :]
        mask = mask_ref[:, pl.ds(off, tk)] > 0.5
        _online_update(0, _mask_rows(mask, _qk(qst, kt), tq), vt, m_sc, l_sc, acc_sc)
        return carry

    lax.fori_loop(0, _n_kv_tiles(qpos0, i, tq, tk, lp), body, 0)
    o_ref[...] = _unstack_heads(acc_sc[0] / l_sc[0], tq).astype(o_ref.dtype)


def masked_attention(q, k, v, mask, *, qpos0):
    b, tqn, _ = q.shape
    lp = _kv_len(k)
    tq = _pick(tqn, 256)
    tk = _kv_tile(lp, tq)
    kern = functools.partial(_masked_attn_kernel, tq=tq, tk=tk, qpos0=qpos0, lp=lp)
    qspec = pl.BlockSpec((None, tq, GW), lambda bi, i, g: (bi, i, g))
    bg = lambda bi, i, g: (bi, g)
    return pl.pallas_call(
        kern, grid=(b, tqn // tq, GROUPS),
        in_specs=[qspec, _kv_spec(k, bg), _kv_spec(v, bg),
                  pl.BlockSpec((None, tq, lp), lambda bi, i, g: (bi, i, 0))],
        out_specs=qspec,
        out_shape=jax.ShapeDtypeStruct(q.shape, BF16),
        scratch_shapes=_softmax_scratch(1, HEADS_PER_GROUP * tq),
        compiler_params=_cparams("parallel", "parallel", "parallel"), name="masked_attention",
    )(q, k, v, mask)


def _cross_kernel(q_ref, k_ref, v_ref, o_ref):
    q = q_ref[...]
    k = k_ref[...].astype(BF16)
    v = v_ref[...].astype(BF16)
    outs = []
    for h in range(X_HEADS):
        cs = slice(h * X_HD, (h + 1) * X_HD)
        s = _qk(q[:, cs], k[:, cs]) * (X_HD ** -0.5)
        e = jnp.exp(s - jnp.max(s, axis=-1, keepdims=True))
        p = e / jnp.sum(e, axis=-1, keepdims=True)
        outs.append(jnp.dot(p.astype(BF16), v[:, cs], preferred_element_type=F32))
    o_ref[...] = jnp.concatenate(outs, axis=1).astype(o_ref.dtype)


def cross_attention(q, mk, mv):
    b, tqn, _ = q.shape
    tq = _pick(tqn, 512)
    ml = mk.shape[1]
    qspec = pl.BlockSpec((None, tq, X_W), lambda bi, i: (bi, i, 0))
    mspec = pl.BlockSpec((None, ml, X_W), lambda bi, i: (bi, 0, 0))
    return pl.pallas_call(
        _cross_kernel, grid=(b, tqn // tq), in_specs=[qspec, mspec, mspec], out_specs=qspec,
        out_shape=jax.ShapeDtypeStruct(q.shape, BF16),
        compiler_params=_cparams("parallel", "parallel"), name="cross_attention",
    )(q, mk, mv)


PAGES_PER_STEP = 16


def _gather_kernel(pt_ref, *refs, pps):
    pools, new_ref, o_ref = refs[:pps], refs[pps], refs[pps + 1]
    s = pl.program_id(1)
    n_steps = pl.num_programs(1) - 1

    @pl.when(s < n_steps)
    def _():
        for k in range(pps):
            x = pools[k][...].astype(o_ref.dtype)
            o_ref[k * PAGE_SIZE:(k + 1) * PAGE_SIZE, :] = jnp.concatenate([x, x], axis=1)

    @pl.when(s == n_steps)
    def _():
        o_ref[...] = jnp.zeros(o_ref.shape, o_ref.dtype)
        o_ref[0:PAGE_SIZE, :] = new_ref[...]


def gather_index_keys(pool, layer, page_table, new_rows):
    b, n_pages = page_table.shape
    pps = math.gcd(PAGES_PER_STEP, n_pages)
    n_steps = n_pages // pps
    rows = pps * PAGE_SIZE
    grid_spec = pltpu.PrefetchScalarGridSpec(
        num_scalar_prefetch=1, grid=(b, n_steps + 1),
        in_specs=_page_specs(pool, layer, pps, n_pages)
        + [pl.BlockSpec((None, PAGE_SIZE, LANES), lambda bi, s, pt: (bi, 0, 0))],
        out_specs=pl.BlockSpec((None, rows, LANES), lambda bi, s, pt: (bi, s, 0)))
    return pl.pallas_call(
        functools.partial(_gather_kernel, pps=pps), grid_spec=grid_spec,
        out_shape=jax.ShapeDtypeStruct((b, (n_steps + 1) * rows, LANES), BF16),
        compiler_params=_cparams("parallel", "arbitrary"), name="gather_index_keys",
    )(page_table, *([pool] * pps), new_rows)


def _paged_steps(pool, page_table):
    n_pages = page_table.shape[1]
    view = pool.reshape(pool.shape[0], pool.shape[1], PAGE_SIZE * GROUPS, HD)
    pps = math.gcd(PAGES_PER_STEP, n_pages)
    return view, pps, n_pages // pps


def _page_specs(view, layer, pps, n_pages):
    def spec(k):
        return pl.BlockSpec((None, None) + view.shape[2:],
                            lambda bi, s, pt: (layer, pt[bi, jnp.minimum(s * pps + k, n_pages - 1)], 0, 0))
    return [spec(k) for k in range(pps)]


def _group_rows(page_refs, g):
    return jnp.concatenate([p[pl.ds(g, PAGE_SIZE, stride=GROUPS), :].astype(BF16) for p in page_refs], axis=0)


def _paged_decode_kernel(pt_ref, *refs, pps, mode, qpos0, past_len, post):
    it = iter(refs)
    lam_ref = next(it) if mode == 'diff' else None
    q_ref = next(it)
    kpools = [next(it) for _ in range(pps)]
    vpools = [next(it) for _ in range(pps)]
    knew_ref, vnew_ref = next(it), next(it)
    mask_ref = next(it) if mode != 'diff' else None
    g_ref = next(it) if mode == 'diff' else None
    o_ref, m_sc, l_sc, acc_sc = next(it), next(it), next(it), next(it)

    step = pl.program_id(1)
    n_steps = pl.num_programs(1) - 1
    tq = q_ref.shape[0]
    rows = pps * PAGE_SIZE
    qpos = qpos0 + lax.broadcasted_iota(I32, (tq, 1), 0)

    @pl.when(step == 0)
    def _():
        _init_state(m_sc, l_sc, acc_sc)

    def queries(g):
        q = q_ref[:, g * GW:(g + 1) * GW]
        if mode != 'diff':
            return _stack_heads(q)
        lo = lax.broadcasted_iota(I32, (1, HD), 1) < A_HALF
        return _concat_rows([_stack_heads(q, lo), _stack_heads(q, jnp.logical_not(lo))])

    def attend(g, kt, vt, mask):
        s = _qk(queries(g), kt)
        if mask is not None:
            s = _mask_rows(mask, s, tq)
        _online_update(g, s, vt, m_sc, l_sc, acc_sc)

    @pl.when(step < n_steps)
    def _():
        mask = mask_ref[...] > 0.5 if mode == 'mask' else None
        if mode == 'chunks':
            assert rows // CMP_STRIDE == LANES
            kpos = step * rows + lax.broadcasted_iota(I32, (1, rows), 1)
            chunk_row = step * LANES + lax.broadcasted_iota(I32, (LANES, 1), 0)
            expand = jnp.where(chunk_row == (kpos >> 4), 1.0, 0.0).astype(BF16)
        for g in range(GROUPS):
            if mode == 'chunks':
                mask = jnp.dot(mask_ref[g], expand, preferred_element_type=F32) > 0.5
            attend(g, _group_rows(kpools, g), _group_rows(vpools, g), mask)

    @pl.when(step == n_steps)
    def _():
        kpos = past_len + lax.broadcasted_iota(I32, (1, PAGE_SIZE), 1)
        mask = kpos <= qpos
        if mode == 'mask':
            mask = mask & (mask_ref[:, 0:PAGE_SIZE] > 0.5)
        for g in range(GROUPS):
            attend(g, knew_ref[g], vnew_ref[g], mask)
        for g in range(GROUPS):
            if mode == 'diff':
                on = acc_sc[g] / l_sc[g]
                o = on[:HEADS_PER_GROUP * tq] - lam_ref[0] * on[HEADS_PER_GROUP * tq:]
                ms = jnp.mean(o * o, axis=-1, keepdims=True)
                o = o * lax.rsqrt(ms + EPS) * g_ref[...] * post
            else:
                o = acc_sc[g] / l_sc[g]
            o_ref[:, g * GW:(g + 1) * GW] = _unstack_heads(o, tq).astype(o_ref.dtype)


def paged_decode_attention(q, kpool, vpool, layer, page_table, k_new, v_new, *, mode, qpos0, out_dtype=BF16,
                           mask=None, lam_f=None, subln=None, lam_init=0.0):
    b, tq, _ = q.shape
    n_pages = page_table.shape[1]
    kview, pps, n_steps = _paged_steps(kpool, page_table)
    vview, _, _ = _paged_steps(vpool, page_table)
    rows = pps * PAGE_SIZE
    n_c = 2 if mode == 'diff' else 1
    in_specs, args = [], []
    if mode == 'diff':
        in_specs.append(pl.BlockSpec(memory_space=pltpu.SMEM))
        args.append(lam_f.reshape(1).astype(F32))
    in_specs.append(pl.BlockSpec((None, tq, D_MODEL), lambda bi, s, pt: (bi, 0, 0)))
    args.append(q)
    in_specs += _page_specs(kview, layer, pps, n_pages) + _page_specs(vview, layer, pps, n_pages)
    args += [kview] * pps + [vview] * pps
    new_spec = pl.BlockSpec((None, GROUPS, PAGE_SIZE, HD), lambda bi, s, pt: (bi, 0, 0, 0))
    in_specs += [new_spec, new_spec]
    args += [k_new, v_new]
    if mode == 'mask':
        in_specs.append(pl.BlockSpec((None, tq, rows), lambda bi, s, pt: (bi, 0, s)))
        args.append(mask)
    elif mode == 'chunks':
        in_specs.append(pl.BlockSpec((None, GROUPS, tq, LANES), lambda bi, s, pt: (bi, 0, 0, s)))
        args.append(mask)
    else:
        in_specs.append(pl.BlockSpec((1, HD), lambda bi, s, pt: (0, 0)))
        args.append(subln.reshape(1, HD))
    kern = functools.partial(_paged_decode_kernel, pps=pps, mode=mode, qpos0=qpos0,
                             past_len=n_pages * PAGE_SIZE, post=1.0 - lam_init)
    grid_spec = pltpu.PrefetchScalarGridSpec(
        num_scalar_prefetch=1, grid=(b, n_steps + 1), in_specs=in_specs,
        out_specs=pl.BlockSpec((None, tq, D_MODEL), lambda bi, s, pt: (bi, 0, 0)),
        scratch_shapes=_softmax_scratch(GROUPS, n_c * HEADS_PER_GROUP * tq))
    return pl.pallas_call(
        kern, grid_spec=grid_spec, out_shape=jax.ShapeDtypeStruct(q.shape, out_dtype),
        compiler_params=_cparams("parallel", "arbitrary"), name="paged_decode_attention",
    )(page_table, *args)


def _paged_compress_kernel(pt_ref, *refs, pps):
    pools, w_ref, o_ref = refs[:pps], refs[pps], refs[pps + 1]
    cpp = PAGE_SIZE // CMP_STRIDE
    acc = jnp.zeros((GROUPS * pps * cpp, 2 * HD), F32)
    for j in range(CMP_STRIDE):
        x = jnp.concatenate([p[pl.ds(GROUPS * j + g, cpp, stride=GROUPS * CMP_STRIDE), :]
                             for g in range(GROUPS) for p in pools], axis=0)
        acc = acc + jnp.dot(x.astype(BF16), w_ref[j], preferred_element_type=F32)
    for g in range(GROUPS):
        o_ref[g] = acc[g * pps * cpp:(g + 1) * pps * cpp]


def paged_compress(pool, layer, page_table, w):
    b, n_pages = page_table.shape
    view, pps, n_steps = _paged_steps(pool, page_table)
    cps = pps * (PAGE_SIZE // CMP_STRIDE)
    grid_spec = pltpu.PrefetchScalarGridSpec(
        num_scalar_prefetch=1, grid=(b, n_steps),
        in_specs=_page_specs(view, layer, pps, n_pages) + [pl.BlockSpec(w.shape, lambda bi, s, pt: (0, 0, 0))],
        out_specs=pl.BlockSpec((None, GROUPS, cps, 2 * HD), lambda bi, s, pt: (bi, 0, s, 0)))
    return pl.pallas_call(
        functools.partial(_paged_compress_kernel, pps=pps), grid_spec=grid_spec,
        out_shape=jax.ShapeDtypeStruct((b, GROUPS, n_steps * cps, 2 * HD), F32),
        compiler_params=_cparams("parallel", "parallel"), name="paged_compress",
    )(page_table, *([view] * pps), w)


def lambda_init(layer):
    return 0.8 - 0.6 * math.exp(-0.3 * layer)


def _pad_rows(x, n):
    return jnp.pad(x, ((0, 0), (0, n - x.shape[1]), (0, 0)))


def trunk(x, nb, tq_real, qpos0, mem_k, mem_v, W, past):
    t = nb * tq_real
    tqp = max(tq_real, SUBLANES)
    pos = qpos0 + jnp.arange(tq_real, dtype=I32)
    pos_rows = jnp.tile(pos, nb) if tq_real < SUBLANES else pos
    tab64 = rope_tables(pos_rows, 64)
    tab128 = rope_tables(pos_rows, 128)
    new = {}

    def to_attn(a):
        return _pad_rows(a.reshape(nb, tq_real, a.shape[-1]), tqp)

    def from_attn(a):
        return a[:, :tq_real].reshape(t, a.shape[-1])

    if past is not None:
        pt = past['page_table']
        past_len = pt.shape[1] * PAGE_SIZE

        def new_block(new_bf):
            nr = new_bf.reshape(nb, tq_real, GROUPS, HD).transpose(0, 2, 1, 3)
            return jnp.pad(nr, ((0, 0), (0, 0), (0, PAGE_SIZE - tq_real), (0, 0)))
    else:
        past_len = 0

    for l in range(DEPTH):
        x = yield from _ffn_steps(x, W['norm_ffn1'], W['ffn1_w_in'], W['ffn1_w_out'], l)
        h = rmsnorm(x, W['norm_mix'][l], BF16)
        i = l // N_MIXERS
        kind = l % N_MIXERS
        if kind == 0:
            proj = yield (linear, h, None, dict(w=W['a_w_in'], layer=i))
            (q_rot,) = take_cols(proj, 0, A_Q, (BF16,), tab64, scale=_logit_scale(A_HALF))
            k_f, k_b = take_cols(proj, A_Q, A_K, (F32, BF16), tab64)
            (v_b,) = take_cols(proj, A_Q + A_K, A_KV_HEADS * A_VDIM, (BF16,))
            v_f = proj[:, A_Q + A_K:]
            new.setdefault('a_k', []).append(k_f.reshape(nb, tq_real, A_KV_HEADS, 2 * A_HALF))
            new.setdefault('a_v', []).append(v_f.reshape(nb, tq_real, A_KV_HEADS, A_VDIM))
            lam = W['a_lambda'][i]
            lam_f = (jnp.exp(jnp.sum(lam[0] * lam[1])) - jnp.exp(jnp.sum(lam[2] * lam[3]))).astype(F32) + lambda_init(l)
            if past is None:
                o = diff_attention(to_attn(q_rot), k_b.reshape(nb, tq_real, -1), v_b.reshape(nb, tq_real, -1),
                                   lam_f, W['a_subln'][i], lambda_init(l), qpos0)
            else:
                o = paged_decode_attention(to_attn(q_rot), past['a_k'], past['a_v'], i, pt, new_block(k_b),
                                           new_block(v_b), mode='diff', qpos0=qpos0, lam_f=lam_f,
                                           subln=W['a_subln'][i], lam_init=lambda_init(l))
            y_in, w_out = from_attn(o), W['a_w_out']
        elif kind == 1:
            w_in = W['b_w_in']
            n_main = B_Q + 6 * B_KV
            proj = yield (linear, h, None, dict(w=w_in, layer=i, ncols=n_main))
            w_tail = jnp.pad(w_in[i, :, n_main:], ((0, 0), (0, LANES - (w_in.shape[-1] - n_main))))
            gate_logits = yield (linear, h, None, dict(w=w_tail, tn=LANES))
            (q_raw,) = take_cols(proj, 0, B_Q, (BF16,))
            (q_rot,) = take_cols(proj, 0, B_Q, (BF16,), tab128, scale=_logit_scale(B_HD))
            kc_f = proj[:, B_Q:B_Q + B_KV]
            vc_f = proj[:, B_Q + B_KV:B_Q + 2 * B_KV]
            ks_f, ks_b = take_cols(proj, B_Q + 2 * B_KV, B_KV, (F32, BF16), tab128)
            vs_f = proj[:, B_Q + 3 * B_KV:B_Q + 4 * B_KV]
            (vs_b,) = take_cols(proj, B_Q + 3 * B_KV, B_KV, (BF16,))
            kw_f, kw_b = take_cols(proj, B_Q + 4 * B_KV, B_KV, (F32, BF16), tab128)
            vw_f = proj[:, B_Q + 5 * B_KV:B_Q + 6 * B_KV]
            (vw_b,) = take_cols(proj, B_Q + 5 * B_KV, B_KV, (BF16,))
            shp = (nb, tq_real, B_KV_HEADS, B_HD)
            for nm, a in (('b_cmp_k', kc_f), ('b_cmp_v', vc_f), ('b_sel_k', ks_f), ('b_sel_v', vs_f)):
                new.setdefault(nm, []).append(a.reshape(shp))

            cw = W['b_cmp_w'][i]
            wfs = [jnp.concatenate([cw[s, :CMP_STRIDE], cw[s, CMP_STRIDE:]], axis=-1) for s in range(2)]
            if past is None:
                l_all = tq_real
                ks_all, vs_all = ks_b.reshape(nb, tq_real, -1), vs_b.reshape(nb, tq_real, -1)
                kw_all, vw_all = kw_b.reshape(nb, tq_real, -1), vw_b.reshape(nb, tq_real, -1)
                kwpos0 = 0
                keep = min(WINDOW, tq_real)
                new.setdefault('b_win_k', []).append(kw_f.reshape(shp)[:, tq_real - keep:])
                new.setdefault('b_win_v', []).append(vw_f.reshape(shp)[:, tq_real - keep:])
                lp_sel = tq_real
            else:
                l_all = past_len + tq_real
                bkw = past['b_win_k'][i].reshape(nb, -1, B_KV)
                bvw = past['b_win_v'][i].reshape(nb, -1, B_KV)
                wb = bkw.shape[1]
                kb = jnp.concatenate([bkw, kw_f.reshape(nb, tq_real, B_KV)], axis=1)
                vb = jnp.concatenate([bvw, vw_f.reshape(nb, tq_real, B_KV)], axis=1)
                new.setdefault('b_win_k', []).append(kb[:, tq_real:].reshape(nb, wb, B_KV_HEADS, B_HD))
                new.setdefault('b_win_v', []).append(vb[:, tq_real:].reshape(nb, wb, B_KV_HEADS, B_HD))
                lw = -(-(wb + tq_real) // 512) * 512
                kw_all = _pad_rows(kb, lw).astype(BF16)
                vw_all = _pad_rows(vb, lw).astype(BF16)
                kwpos0 = qpos0 - wb
                lp_sel = past_len + math.gcd(PAGES_PER_STEP, pt.shape[1]) * PAGE_SIZE
            nch = (l_all // CMP_STRIDE)
            nblk = nch - 1
            n_sel = -(-l_all // SEL_BLOCK)
            n_rep = nch // (SEL_BLOCK // CMP_STRIDE)
            assert nch % (SEL_BLOCK // CMP_STRIDE) == 0 and n_sel - n_rep in (0, 1)
            n_top_rep = min(SEL_N, n_sel) - (n_sel - n_rep)
            wch = -(-(lp_sel // CMP_STRIDE) // LANES) * LANES

            def partials(rows, wf):
                xg = rows.reshape(nb, nch, CMP_STRIDE, B_KV_HEADS, B_HD)
                xg = xg.transpose(0, 3, 1, 2, 4).reshape(nb * B_KV_HEADS * nch, CMP_STRIDE * B_HD)
                return linear(xg, wf.reshape(CMP_STRIDE * B_HD, 2 * B_HD), tn=2 * B_HD).reshape(
                    nb, B_KV_HEADS, nch, 2 * B_HD)

            if past is None:
                fsk, fsv = partials(kc_f, wfs[0]), partials(vc_f, wfs[1])
            else:
                fsk = paged_compress(past['b_cmp_k'], i, pt, wfs[0].astype(BF16))
                fsv = paged_compress(past['b_cmp_v'], i, pt, wfs[1].astype(BF16))
            fsk, fsv = (jnp.pad(f, ((0, 0), (0, 0), (0, wch - nch), (0, 0))) for f in (fsk, fsv))
            o_cmp, selmask = nsa_compressed(to_attn(q_raw), fsk, fsv, W['b_cmp_b'][i].astype(F32),
                                            nblk=nblk, nch=nch, qpos0=qpos0, n_top_rep=n_top_rep)
            gates = gate_logits[:, :3 * B_HEADS].reshape(nb, tq_real, B_KV_HEADS, 3 * HEADS_PER_GROUP)
            gates = jnp.pad(gates.transpose(0, 2, 1, 3),
                            ((0, 0), (0, 0), (0, tqp - tq_real), (0, LANES - 3 * HEADS_PER_GROUP)))
            if past is None:
                o = nsa_select_window(to_attn(q_rot), ks_all, vs_all, kw_all, vw_all, selmask, o_cmp, gates,
                                      qpos0=qpos0, kwpos0=kwpos0)
            else:
                o_sel = paged_decode_attention(to_attn(q_rot), past['b_sel_k'], past['b_sel_v'], i, pt,
                                               new_block(ks_b), new_block(vs_b), mode='chunks', qpos0=qpos0,
                                               mask=selmask, out_dtype=F32)
                o = nsa_select_window(to_attn(q_rot), None, None, kw_all, vw_all, None, o_cmp, gates,
                                      qpos0=qpos0, kwpos0=kwpos0, o_sel=o_sel)
            y_in, w_out = from_attn(o), W['b_w_out']
        else:
            w_in = W['c_w_in']
            n_main = C_Q + 2 * C_KV + IDX_HEADS * IDX_DIM
            proj = yield (linear, h, None, dict(w=w_in, layer=i, ncols=n_main))
            w_tail = jnp.pad(w_in[i, :, n_main:], ((0, 0), (0, LANES - (w_in.shape[-1] - n_main))))
            tail = yield (linear, h, None, dict(w=w_tail, tn=LANES))
            (q_rot,) = take_cols(proj, 0, C_Q, (BF16,), tab128, scale=_logit_scale(C_HD))
            k_f, k_b = take_cols(proj, C_Q, C_KV, (F32, BF16), tab128)
            v_f = proj[:, C_Q + C_KV:C_Q + 2 * C_KV]
            (v_b,) = take_cols(proj, C_Q + C_KV, C_KV, (BF16,))
            (iq_rot,) = take_cols(proj, C_Q + 2 * C_KV, IDX_HEADS * IDX_DIM, (BF16,), tab64)
            ik_f, ik2 = index_keys(tail, W['c_idx_knorm'][i], tab64)
            shp = (nb, tq_real, C_KV_HEADS, C_HD)
            new.setdefault('c_k', []).append(k_f.reshape(shp))
            new.setdefault('c_v', []).append(v_f.reshape(shp))
            new.setdefault('c_idx_k', []).append(ik_f[:, :IDX_DIM].reshape(nb, tq_real, IDX_DIM))
            l_all = past_len + tq_real
            top = min(IDX_TOPK_MAX, l_all // 4)
            if past is None:
                mask = dsa_select(to_attn(iq_rot), ik2.reshape(nb, tq_real, LANES), to_attn(tail), qpos0=qpos0, top=top)
                o = masked_attention(to_attn(q_rot), k_b.reshape(nb, tq_real, -1), v_b.reshape(nb, tq_real, -1),
                                     mask, qpos0=qpos0)
            else:
                ik_all = gather_index_keys(past['c_idx_k'], i, pt,
                                           _pad_rows(ik2.reshape(nb, tq_real, LANES), PAGE_SIZE))
                mask = dsa_select(to_attn(iq_rot), ik_all, to_attn(tail), qpos0=qpos0, top=top)
                o = paged_decode_attention(to_attn(q_rot), past['c_k'], past['c_v'], i, pt, new_block(k_b),
                                           new_block(v_b), mode='mask', qpos0=qpos0, mask=mask)
            y_in, w_out = from_attn(o), W['c_w_out']
        x = yield (linear, y_in, x, dict(w=w_out, layer=i, scale=1.0))

        h = rmsnorm(x, W['norm_cross'][l], BF16)
        qx = yield (linear, h, None, dict(w=W['x_w_q'], layer=l, out_dtype=BF16))
        ox = cross_attention(to_attn(qx), mem_k[l], mem_v[l])
        x = yield (linear, from_attn(ox), x, dict(w=W['x_w_o'], layer=l, scale=1.0))
        x = yield from _ffn_steps(x, W['norm_ffn2'], W['ffn2_w_in'], W['ffn2_w_out'], l)
    y = rmsnorm(x, W['final_norm'], F32)
    return y, {nm: jnp.stack(v) for nm, v in new.items()}


def kernel(x_prompt, x_sample, cache_a_k, cache_a_v, cache_b_cmp_k, cache_b_cmp_v, cache_b_sel_k, cache_b_sel_v, state_b_win_k, state_b_win_v, cache_c_k, cache_c_v, cache_c_idx_k, cache_mem_k, cache_mem_v, page_table, mem_prompt, norm_ffn1, norm_mix, norm_cross, norm_ffn2, final_norm, ffn1_w_in, ffn1_w_out, ffn2_w_in, ffn2_w_out, x_w_q, x_w_kv, x_w_o, a_w_in, a_w_out, a_lambda, a_subln, b_w_in, b_w_out, b_cmp_w, b_cmp_b, c_w_in, c_w_out, c_idx_knorm):
    W = dict(norm_ffn1=norm_ffn1, norm_mix=norm_mix, norm_cross=norm_cross, norm_ffn2=norm_ffn2,
             final_norm=final_norm, ffn1_w_in=ffn1_w_in, ffn1_w_out=ffn1_w_out, ffn2_w_in=ffn2_w_in,
             ffn2_w_out=ffn2_w_out, x_w_q=x_w_q, x_w_o=x_w_o, a_w_in=a_w_in, a_w_out=a_w_out,
             a_lambda=a_lambda, a_subln=a_subln, b_w_in=b_w_in, b_w_out=b_w_out, b_cmp_w=b_cmp_w,
             b_cmp_b=b_cmp_b, c_w_in=c_w_in, c_w_out=c_w_out, c_idx_knorm=c_idx_knorm)
    nbp, seq, d = x_prompt.shape
    nbs, dseq, _ = x_sample.shape
    ml = mem_prompt.shape[1]

    mem2d = mem_prompt.reshape(nbp * ml, d)
    mkv = [linear(mem2d, x_w_kv, l) for l in range(DEPTH)]
    p_mem_k = jnp.stack([m[:, :X_W].reshape(nbp, ml, X_HEADS, X_HD) for m in mkv])
    p_mem_v = jnp.stack([m[:, X_W:].reshape(nbp, ml, X_HEADS, X_HD) for m in mkv])
    prompt = trunk(x_prompt.reshape(nbp * seq, d), nbp, seq, 0,
                   p_mem_k.reshape(DEPTH, nbp, ml, X_W), p_mem_v.reshape(DEPTH, nbp, ml, X_W), W, None)

    past = dict(a_k=cache_a_k, a_v=cache_a_v, b_cmp_k=cache_b_cmp_k, b_cmp_v=cache_b_cmp_v,
                b_sel_k=cache_b_sel_k, b_sel_v=cache_b_sel_v, b_win_k=state_b_win_k, b_win_v=state_b_win_v,
                c_k=cache_c_k, c_v=cache_c_v, c_idx_k=cache_c_idx_k, page_table=page_table)
    past_len = page_table.shape[1] * PAGE_SIZE
    sml = cache_mem_k.shape[2]
    sample = trunk(x_sample.reshape(nbs * dseq, d), nbs, dseq, past_len,
                   cache_mem_k.reshape(DEPTH, nbs, sml, X_W), cache_mem_v.reshape(DEPTH, nbs, sml, X_W), W, past)
    (y_p, ps), (y_s, ss) = _run_together([prompt, sample])

    return (y_p.reshape(nbp, seq, d), y_s.reshape(nbs, dseq, d),
            ps['a_k'], ps['a_v'], ps['b_cmp_k'], ps['b_cmp_v'], ps['b_sel_k'], ps['b_sel_v'],
            ps['b_win_k'], ps['b_win_v'], ps['c_k'], ps['c_v'], ps['c_idx_k'], p_mem_k, p_mem_v,
            ss['a_k'], ss['a_v'], ss['b_cmp_k'], ss['b_cmp_v'], ss['b_sel_k'], ss['b_sel_v'],
            ss['b_win_k'], ss['b_win_v'], ss['c_k'], ss['c_v'], ss['c_idx_k'])
```

```python
import functools
import math

import jax
import jax.numpy as jnp
from jax import lax
from jax.experimental import pallas as pl
from jax.experimental.pallas import tpu as pltpu

F32 = jnp.float32
BF16 = jnp.bfloat16
I32 = jnp.int32

D_MODEL = 2048
DEPTH = 4
PAGE_SIZE = 128
N_MIXERS = 3
ROPE_THETA = 500000.0
EPS = 1e-6
NEG_INF = -1e30
FORCE = 1e9
D_FF = 256 * math.ceil(8 * D_MODEL / 3 / 256)

A_HEADS = D_MODEL // 128
A_HALF = 64
A_VDIM = 128
A_KV_HEADS = 4
A_Q = A_HEADS * 2 * A_HALF
A_K = A_KV_HEADS * 2 * A_HALF

B_HEADS = D_MODEL // 128
B_HD = 128
B_KV_HEADS = 4
B_KV = B_KV_HEADS * B_HD
B_Q = B_HEADS * B_HD
CMP_STRIDE = 16
CMP_LEN = 32
SEL_BLOCK = 64
SEL_N = 16
WINDOW = 512

C_HEADS = D_MODEL // 128
C_HD = 128
C_KV_HEADS = 4
C_Q = C_HEADS * C_HD
C_KV = C_KV_HEADS * C_HD
IDX_HEADS = 16
IDX_DIM = 64
IDX_TOPK_MAX = 256

MEM_LEN = 256
X_HEADS = 4
X_HD = 128
X_W = X_HEADS * X_HD

GROUPS = 4
HEADS_PER_GROUP = 4
HD = 128
GW = HEADS_PER_GROUP * HD

LANES = 128
SUBLANES = 8
VMEM_LIMIT_BYTES = 56 * 1024 * 1024
M_INIT = -1e29
LOG2E = 1.4426950408889634
INT_MIN = -2147483648


def _cparams(*sem):
    return pltpu.CompilerParams(dimension_semantics=sem, vmem_limit_bytes=VMEM_LIMIT_BYTES)


def _pick(n, pref):
    if n <= pref:
        return n
    t = pref
    while n % t:
        t //= 2
    return t


def _rms_kernel(x_ref, g_ref, o_ref):
    x = x_ref[...]
    ms = jnp.mean(x * x, axis=-1, keepdims=True)
    o_ref[...] = (x * lax.rsqrt(ms + EPS) * g_ref[...]).astype(o_ref.dtype)


def rmsnorm(x, g, out_dtype):
    t, d = x.shape
    tm = _pick(t, 512)
    return pl.pallas_call(
        _rms_kernel,
        grid=(t // tm,),
        in_specs=[pl.BlockSpec((tm, d), lambda i: (i, 0)), pl.BlockSpec((1, d), lambda i: (0, 0))],
        out_specs=pl.BlockSpec((tm, d), lambda i: (i, 0)),
        out_shape=jax.ShapeDtypeStruct((t, d), out_dtype),
        compiler_params=_cparams("parallel"),
        name="rmsnorm",
    )(x, g.reshape(1, d))


def _dense_kernel(*refs, mode, scale, extra):
    it = iter(refs)
    a_ref = next(it)
    a2_ref = next(it) if extra else None
    w_refs = [next(it) for _ in range(2 if mode == 'swiglu' else 1)]
    r_ref = next(it) if mode == 'res' else None
    r2_ref = next(it) if mode == 'res' and extra else None
    o_ref = next(it)
    o2_ref = next(it) if extra else None
    wb_refs = [next(it) for _ in w_refs]

    def apply(a, r, out):
        a = a[...].astype(BF16)
        if mode == 'swiglu':
            g = jnp.dot(a, wb_refs[0][...], preferred_element_type=F32)
            u = jnp.dot(a, wb_refs[1][...], preferred_element_type=F32)
            y = g * jax.nn.sigmoid(g) * u
        else:
            y = jnp.dot(a, wb_refs[0][...], preferred_element_type=F32)
            if mode == 'res':
                y = r[...] + scale * y
        out[...] = y.astype(out.dtype)

    @pl.when(pl.program_id(1) == 0)
    def _():
        for w_ref, wb_ref in zip(w_refs, wb_refs):
            wb_ref[...] = w_ref[...].astype(BF16)
        if extra:
            apply(a2_ref, r2_ref, o2_ref)

    apply(a_ref, r_ref, o_ref)


def _w_spec(w, layer, k, tn, c0):
    if w.ndim == 3:
        return pl.BlockSpec((None, k, tn), lambda j, i: (layer, 0, j + c0))
    return pl.BlockSpec((k, tn), lambda j, i: (0, j + c0))


def _dense(a, w_specs, ws, *, mode, k, tn, ncols, out_dtype, res=None, scale=1.0, tm=None, name):
    pair = isinstance(a, (list, tuple))
    a_main, a_x = (a[0], a[1]) if pair else (a, None)
    r_main, r_x = (res[0], res[1]) if (pair and res is not None) else (res, None)
    m = a_main.shape[0]
    tm = _pick(m, (2048 if k <= D_MODEL else 512) if tm is None else tm)
    assert m % tm == 0 and ncols % tn == 0
    row_spec = pl.BlockSpec((tm, k), lambda j, i: (i, 0))
    o_spec = pl.BlockSpec((tm, tn), lambda j, i: (i, j))
    in_specs, args = [row_spec], [a_main]
    out_specs, out_shape = [o_spec], [jax.ShapeDtypeStruct((m, ncols), out_dtype)]
    if pair:
        mx = a_x.shape[0]
        x_out = pl.BlockSpec((mx, tn), lambda j, i: (0, j))
        in_specs.append(pl.BlockSpec((mx, k), lambda j, i: (0, 0)))
        args.append(a_x)
        out_specs.append(x_out)
        out_shape.append(jax.ShapeDtypeStruct((mx, ncols), out_dtype))
    in_specs += w_specs
    args += ws
    if mode == 'res':
        in_specs.append(o_spec)
        args.append(r_main)
        if pair:
            in_specs.append(x_out)
            args.append(r_x)
    outs = pl.pallas_call(
        functools.partial(_dense_kernel, mode=mode, scale=scale, extra=pair),
        grid=(ncols // tn, m // tm), in_specs=in_specs, out_specs=out_specs, out_shape=out_shape,
        scratch_shapes=[pltpu.VMEM((k, tn), BF16) for _ in w_specs],
        compiler_params=_cparams("parallel", "arbitrary"), name=name,
    )(*args)
    return list(outs) if pair else outs[0]


def linear(a, w, layer=0, *, col0=0, ncols=None, out_dtype=F32, tm=None, tn=512, res=None, scale=1.0):
    k = (a[0] if isinstance(a, (list, tuple)) else a).shape[1]
    ncols = w.shape[-1] - col0 if ncols is None else ncols
    tn = _pick(ncols, tn)
    assert col0 % tn == 0
    return _dense(a, [_w_spec(w, layer, k, tn, col0 // tn)], [w], mode='plain' if res is None else 'res', k=k, tn=tn,
                  ncols=ncols, out_dtype=F32 if res is not None else out_dtype, res=res, scale=scale, tm=tm,
                  name='linear' if res is None else 'linear_res')


def swiglu_in(a, w_in, layer):
    k = (a[0] if isinstance(a, (list, tuple)) else a).shape[1]
    tn = 512
    nj = D_FF // tn
    return _dense(a, [_w_spec(w_in, layer, k, tn, 0), _w_spec(w_in, layer, k, tn, nj)], [w_in, w_in], mode='swiglu',
                  k=k, tn=tn, ncols=D_FF, out_dtype=BF16, tm=1024, name='swiglu_in')


def _ffn_steps(x, g, w_in, w_out, layer):
    h = rmsnorm(x, g[layer], BF16)
    hid = yield (swiglu_in, h, None, dict(w_in=w_in, layer=layer))
    return (yield (linear, hid, x, dict(w=w_out, layer=layer, scale=0.5)))


def _run_together(gens):
    reqs = [next(g) for g in gens]
    results = [None] * len(gens)
    while True:
        fn, _, res0, kw = reqs[0]
        if res0 is not None:
            kw = dict(kw, res=[r[2] for r in reqs])
        outs = fn([r[1] for r in reqs], **kw)
        nxt = []
        for gi, (g, o) in enumerate(zip(gens, outs)):
            try:
                nxt.append(g.send(o))
            except StopIteration as stop:
                results[gi] = stop.value
        if not nxt:
            return results
        assert len(nxt) == len(gens)
        reqs = nxt


def rope_tables(pos, dh):
    rot = dh // 4
    half = rot // 2
    inv = ROPE_THETA ** (-jnp.arange(half, dtype=F32) / half)
    ang = pos.astype(F32)[:, None] * inv[None, :]
    cos, sin = jnp.cos(ang), jnp.sin(ang)
    p = pos.shape[0]
    rest1 = jnp.ones((p, dh - rot), F32)
    rest0 = jnp.zeros((p, dh - rot), F32)
    z = jnp.zeros((p, half), F32)
    c = jnp.concatenate([cos, cos, rest1], axis=1)
    up = jnp.concatenate([-sin, z, rest0], axis=1)
    dn = jnp.concatenate([z, sin, rest0], axis=1)
    rep = LANES // dh
    return tuple(jnp.tile(t, (1, rep)) for t in (c, up, dn)), half


def _rope_chunk(xc, c, up, dn, half):
    return xc * c + pltpu.roll(xc, LANES - half, 1) * up + pltpu.roll(xc, half, 1) * dn


def _cols_kernel(*refs, half, width, scale):
    if half:
        x_ref, c_ref, u_ref, d_ref = refs[:4]
        outs = refs[4:]
        c, up, dn = c_ref[...], u_ref[...], d_ref[...]
        x = x_ref[...]
        y = jnp.concatenate(
            [_rope_chunk(x[:, k * LANES:(k + 1) * LANES], c, up, dn, half) for k in range(width // LANES)], axis=1)
    else:
        x_ref = refs[0]
        outs = refs[1:]
        y = x_ref[...]
    if scale is not None:
        y = y * scale
    for o in outs:
        o[...] = y.astype(o.dtype)


def take_cols(x, col0, width, out_dtypes, tabs=None, scale=None):
    t = x.shape[0]
    bw = 512 if width % 512 == 0 else width
    assert col0 % bw == 0
    tables, half = tabs if tabs is not None else ((), 0)
    tm = _pick(t, 512) if tabs is None else _pick(tables[0].shape[0], 512)
    assert t % tm == 0
    npb = (tables[0].shape[0] // tm) if tabs is not None else 1
    c0 = col0 // bw
    in_specs = [pl.BlockSpec((tm, bw), lambda i, j: (i, j + c0))]
    in_specs += [pl.BlockSpec((tm, LANES), lambda i, j: (i % npb, 0)) for _ in tables]
    outs = pl.pallas_call(
        functools.partial(_cols_kernel, half=half, width=bw, scale=scale),
        grid=(t // tm, width // bw),
        in_specs=in_specs,
        out_specs=[pl.BlockSpec((tm, bw), lambda i, j: (i, j)) for _ in out_dtypes],
        out_shape=[jax.ShapeDtypeStruct((t, width), dt) for dt in out_dtypes],
        compiler_params=_cparams("parallel", "parallel"), name="take_cols",
    )(x, *tables)
    return outs


def _stack_heads(q, sel=None):
    parts = []
    for r in range(HEADS_PER_GROUP):
        qr = q[:, r * HD:(r + 1) * HD]
        if sel is not None:
            qr = jnp.where(sel, qr, jnp.zeros((), q.dtype))
        parts.append(qr)
    return _concat_rows(parts)


def _concat_rows(parts):
    if parts[0].shape[0] % (2 * SUBLANES):
        return jnp.concatenate([p.astype(F32) for p in parts], axis=0).astype(parts[0].dtype)
    return jnp.concatenate(parts, axis=0)


def _unstack_heads(o, tq):
    return jnp.concatenate([o[r * tq:(r + 1) * tq] for r in range(HEADS_PER_GROUP)], axis=1)


def _qk(q, k):
    return lax.dot_general(q, k, (((1,), (1,)), ((), ())), preferred_element_type=F32)


def _mask_rows(mask, s, tq):
    tk = s.shape[-1]
    return jnp.where(mask[None], s.reshape(-1, tq, tk), NEG_INF).reshape(s.shape)


def _logit_scale(dim):
    return (dim ** -0.5) * LOG2E


def _online_update(slot, s, vt, m_sc, l_sc, acc_sc):
    m_prev = m_sc[slot]
    m_new = jnp.maximum(m_prev, jnp.max(s, axis=-1, keepdims=True))
    alpha = jnp.exp2(m_prev - m_new)
    p = jnp.exp2(s - jnp.concatenate([m_new] * (s.shape[-1] // LANES), axis=1))
    l_sc[slot] = alpha * l_sc[slot] + jnp.sum(p, axis=-1, keepdims=True)
    acc_sc[slot] = alpha * acc_sc[slot] + jnp.dot(p.astype(BF16), vt, preferred_element_type=F32)
    m_sc[slot] = m_new


def _init_state(m_sc, l_sc, acc_sc):
    m_sc[...] = jnp.full(m_sc.shape, M_INIT, F32)
    l_sc[...] = jnp.zeros(l_sc.shape, F32)
    acc_sc[...] = jnp.zeros(acc_sc.shape, F32)


def _order_key(x):
    b = lax.bitcast_convert_type(x, I32)
    return jnp.where(b < 0, b ^ jnp.int32(0x7FFFFFFF), b)


def _select_top(key, thr, k):
    r, w = key.shape
    gt = jnp.where(key > thr, 1.0, 0.0)
    eq = jnp.where(key == thr, 1.0, 0.0)
    need = jnp.float32(k) - jnp.sum(gt, axis=-1, keepdims=True)
    before = (lax.broadcasted_iota(I32, (LANES, LANES), 0) < lax.broadcasted_iota(I32, (LANES, LANES), 1))
    before = jnp.where(before, 1.0, 0.0).astype(BF16)
    ones = jnp.ones((LANES, LANES), BF16)
    taken = jnp.zeros((r, LANES), F32)
    cols = []
    for c in range(w // LANES):
        eq_c = eq[:, c * LANES:(c + 1) * LANES]
        eq_b = eq_c.astype(BF16)
        rank = taken + jnp.dot(eq_b, before, preferred_element_type=F32)
        cols.append(jnp.maximum(gt[:, c * LANES:(c + 1) * LANES], jnp.where(rank < need, eq_c, 0.0)))
        taken = taken + jnp.dot(eq_b, ones, preferred_element_type=F32)
    return jnp.concatenate(cols, axis=1) > 0.5


def _kth_largest_key(key, k):
    kf = jnp.float32(k)

    def count_ge(t):
        return jnp.sum(jnp.where(key >= t, 1.0, 0.0), axis=-1, keepdims=True)

    t0 = jnp.where(count_ge(jnp.int32(0)) >= kf, jnp.int32(0), jnp.int32(INT_MIN))

    def one_bit(bit, t):
        cand = t | (jnp.int32(1) << bit)
        return jnp.where(count_ge(cand) >= kf, cand, t)

    if key.shape[0] * key.shape[1] > 128 * 1024:
        return lax.fori_loop(0, 31, lambda it, t: one_bit(jnp.int32(30) - it, t), t0)

    def two_bits(it, t):
        hi = jnp.int32(1) << (jnp.int32(30) - 2 * it)
        lo = jnp.int32(1) << (jnp.int32(29) - 2 * it)
        c1, c2, c3 = t | lo, t | hi, t | hi | lo
        return jnp.where(count_ge(c3) >= kf, c3,
                         jnp.where(count_ge(c2) >= kf, c2, jnp.where(count_ge(c1) >= kf, c1, t)))

    return one_bit(jnp.int32(0), lax.fori_loop(0, 15, two_bits, t0))


def _n_kv_tiles(qpos0, i, tq, tk, lp):
    hi = qpos0 + (i + 1) * tq
    return jnp.minimum((hi + tk - 1) // tk, lp // tk)


def _kv_tile(lp, tq):
    return _pick(lp, 512 if tq >= 64 else 2048)


def _kv_len(k):
    return k.shape[1] if k.ndim == 3 else k.shape[2]


def _kv_spec(k, bg):
    lp = _kv_len(k)
    if k.ndim == 3:
        return pl.BlockSpec((None, lp, HD), lambda *idx: (bg(*idx)[0], 0, bg(*idx)[1]))
    return pl.BlockSpec((None, None, lp, HD), lambda *idx: (bg(*idx)[0], bg(*idx)[1], 0, 0))


def _softmax_scratch(slots, rows):
    return [pltpu.VMEM((slots, rows, LANES), F32), pltpu.VMEM((slots, rows, LANES), F32),
            pltpu.VMEM((slots, rows, HD), F32)]


def _diff_attn_kernel(lam_ref, q_ref, k_ref, v_ref, g_ref, o_ref, m_sc, l_sc, acc_sc, *, tq, tk, qpos0, lp, post):
    i = pl.program_id(2)
    q = q_ref[...]
    lane = lax.broadcasted_iota(I32, (1, HD), 1)
    lo = lane < A_HALF
    qs = _concat_rows([_stack_heads(q, lo), _stack_heads(q, jnp.logical_not(lo))])
    qpos = qpos0 + i * tq + lax.broadcasted_iota(I32, (tq, 1), 0)
    _init_state(m_sc, l_sc, acc_sc)
    half = HEADS_PER_GROUP * tq

    def step(j, masked):
        off = pl.multiple_of(j * tk, tk)
        s = _qk(qs, k_ref[pl.ds(off, tk), :])
        if masked:
            s = _mask_rows(off + lax.broadcasted_iota(I32, (1, tk), 1) <= qpos, s, tq)
        _online_update(0, s, v_ref[pl.ds(off, tk), :], m_sc, l_sc, acc_sc)

    n_full = jnp.minimum((qpos0 + i * tq + 1) // tk, lp // tk)
    lax.fori_loop(0, n_full, lambda j, c: (step(j, False), c)[1], 0)
    lax.fori_loop(n_full, _n_kv_tiles(qpos0, i, tq, tk, lp), lambda j, c: (step(j, True), c)[1], 0)
    on = acc_sc[0] / l_sc[0]
    o = on[:half] - lam_ref[0] * on[half:]
    ms = jnp.mean(o * o, axis=-1, keepdims=True)
    o = o * lax.rsqrt(ms + EPS) * g_ref[...] * post
    o_ref[...] = _unstack_heads(o, tq).astype(o_ref.dtype)


def diff_attention(q, k, v, lam_f, subln, lam_init, qpos0):
    b, tqn, _ = q.shape
    lp = _kv_len(k)
    tq = _pick(tqn, 256)
    tk = _kv_tile(lp, tq)
    kern = functools.partial(_diff_attn_kernel, tq=tq, tk=tk, qpos0=qpos0, lp=lp, post=1.0 - lam_init)
    return pl.pallas_call(
        kern, grid=(b, GROUPS, tqn // tq),
        in_specs=[pl.BlockSpec(memory_space=pltpu.SMEM),
                  pl.BlockSpec((None, tq, GW), lambda bi, g, i: (bi, i, g)),
                  _kv_spec(k, lambda bi, g, i: (bi, g)),
                  _kv_spec(v, lambda bi, g, i: (bi, g)),
                  pl.BlockSpec((1, HD), lambda bi, g, i: (0, 0))],
        out_specs=pl.BlockSpec((None, tq, GW), lambda bi, g, i: (bi, i, g)),
        out_shape=jax.ShapeDtypeStruct(q.shape, BF16),
        scratch_shapes=_softmax_scratch(1, 2 * HEADS_PER_GROUP * tq),
        compiler_params=_cparams("parallel", "parallel", "parallel"), name="diff_attention",
    )(lam_f.reshape(1).astype(F32), q, k, v, subln.reshape(1, HD))


def _nsa_cmp_kernel(q_ref, fsk_ref, fsv_ref, b_ref, o_ref, sel_ref, *, tq, w, nblk, nch, qpos0, klanes):
    i = pl.program_id(2)
    qst = _stack_heads(q_ref[...])
    fsk = fsk_ref[...]
    fsv = fsv_ref[...]
    kcb = fsk[:, :HD] + pltpu.roll(fsk[:, HD:], w - 1, 0) + b_ref[0:1, :]
    vcb = fsv[:, :HD] + pltpu.roll(fsv[:, HD:], w - 1, 0) + b_ref[1:2, :]
    qpos = qpos0 + i * tq + lax.broadcasted_iota(I32, (tq, 1), 0)
    lanei = lax.broadcasted_iota(I32, (1, w), 1)
    s = _qk(qst, kcb.astype(BF16)) * (B_HD ** -0.5)
    valid = (lanei * CMP_STRIDE + (CMP_LEN - 1) <= qpos) & (lanei < nblk)
    s3 = jnp.where(valid[None], s.reshape(HEADS_PER_GROUP, tq, w), NEG_INF)
    e = jnp.exp(s3 - jnp.max(s3, axis=-1, keepdims=True))
    p = e / jnp.sum(e, axis=-1, keepdims=True)
    any_valid = (qpos >= CMP_LEN - 1) & (nblk > 0)
    p = jnp.where(any_valid[None], p, 0.0)
    o = jnp.dot(p.reshape(HEADS_PER_GROUP * tq, w).astype(BF16), vcb.astype(BF16), preferred_element_type=F32)
    o_ref[...] = _unstack_heads(o, tq).astype(o_ref.dtype)

    grp = p[0] + p[1] + p[2] + p[3]
    chunk = grp + jnp.where(lanei == 0, 0.0, pltpu.roll(grp, 1, 1))
    pair = jnp.where((lanei & 1) == 0, chunk + pltpu.roll(chunk, w - 1, 1), chunk + pltpu.roll(chunk, 1, 1))
    quad = jnp.where((lanei & 2) == 0, pair + pltpu.roll(pair, w - 2, 1), pair + pltpu.roll(pair, 2, 1))
    jb = lanei >> 2
    cur = qpos >> 6
    forced = (jb == 0) | (jb == cur) | (jb == cur - 1)
    score = jnp.where(forced, FORCE, jnp.where(jb * SEL_BLOCK <= qpos, quad, NEG_INF))
    score = jnp.where(lanei < nch, score, NEG_INF)
    key = _order_key(score)
    thr = _kth_largest_key(key, klanes)
    sel_ref[...] = jnp.where(_select_top(key, thr, klanes) | (lanei >= nch), 1.0, 0.0).astype(sel_ref.dtype)


def nsa_compressed(q_raw, fsk, fsv, bias, *, nblk, nch, qpos0, n_top_rep):
    b, tqn, _ = q_raw.shape
    w = fsk.shape[2]
    tq = _pick(tqn, 256)
    kern = functools.partial(_nsa_cmp_kernel, tq=tq, w=w, nblk=nblk, nch=nch, qpos0=qpos0,
                             klanes=(SEL_BLOCK // CMP_STRIDE) * n_top_rep)
    return pl.pallas_call(
        kern, grid=(b, GROUPS, tqn // tq),
        in_specs=[pl.BlockSpec((None, tq, GW), lambda bi, g, i: (bi, i, g)),
                  pl.BlockSpec((None, None, w, 2 * HD), lambda bi, g, i: (bi, g, 0, 0)),
                  pl.BlockSpec((None, None, w, 2 * HD), lambda bi, g, i: (bi, g, 0, 0)),
                  pl.BlockSpec((2, HD), lambda bi, g, i: (0, 0))],
        out_specs=[pl.BlockSpec((None, tq, GW), lambda bi, g, i: (bi, i, g)),
                   pl.BlockSpec((None, None, tq, w), lambda bi, g, i: (bi, g, i, 0))],
        out_shape=[jax.ShapeDtypeStruct(q_raw.shape, F32), jax.ShapeDtypeStruct((b, GROUPS, tqn, w), BF16)],
        compiler_params=_cparams("parallel", "parallel", "parallel"), name="nsa_compressed",
    )(q_raw, fsk, fsv, bias)


def _nsa_sw_kernel(*refs, tq, tk, tkw, qpos0, lp, lw, kwpos0, w, sel_given):
    if sel_given:
        q_ref, kw_ref, vw_ref, osel_ref, oc_ref, gate_ref, o_ref, m_sc, l_sc, acc_sc = refs
    else:
        q_ref, ks_ref, vs_ref, kw_ref, vw_ref, sel_ref, oc_ref, gate_ref, o_ref, m_sc, l_sc, acc_sc = refs
    i = pl.program_id(2)
    qst = _stack_heads(q_ref[...])
    qpos = qpos0 + i * tq + lax.broadcasted_iota(I32, (tq, 1), 0)
    _init_state(m_sc, l_sc, acc_sc)
    chunks_per_tile = tk // CMP_STRIDE
    assert LANES % chunks_per_tile == 0 and w % LANES == 0

    def sel_body(j, carry):
        off = pl.multiple_of(j * tk, tk)
        kt = ks_ref[pl.ds(off, tk), :]
        vt = vs_ref[pl.ds(off, tk), :]
        kpos = off + lax.broadcasted_iota(I32, (1, tk), 1)
        cbase = pl.multiple_of((j * chunks_per_tile) // LANES * LANES, LANES)
        chunk_row = cbase + lax.broadcasted_iota(I32, (LANES, 1), 0)
        expand = jnp.where(chunk_row == (kpos >> 4), 1.0, 0.0).astype(BF16)
        chosen = jnp.dot(sel_ref[:, pl.ds(cbase, LANES)], expand, preferred_element_type=F32) > 0.5
        mask = chosen & (kpos <= qpos)
        _online_update(0, _mask_rows(mask, _qk(qst, kt), tq), vt, m_sc, l_sc, acc_sc)
        return carry

    if not sel_given:
        lax.fori_loop(0, _n_kv_tiles(qpos0, i, tq, tk, lp), sel_body, 0)

    def win_body(j, carry):
        off = pl.multiple_of(j * tkw, tkw)
        kt = kw_ref[pl.ds(off, tkw), :]
        vt = vw_ref[pl.ds(off, tkw), :]
        kpos = kwpos0 + off + lax.broadcasted_iota(I32, (1, tkw), 1)
        dt = qpos - kpos
        mask = (dt >= 0) & (dt <= WINDOW)
        _online_update(1, _mask_rows(mask, _qk(qst, kt), tq), vt, m_sc, l_sc, acc_sc)
        return carry

    first = jnp.maximum(qpos0 + i * tq - WINDOW - kwpos0, 0) // tkw
    last = jnp.minimum((qpos0 + (i + 1) * tq - kwpos0 + tkw - 1) // tkw, lw // tkw)
    lax.fori_loop(first, last, win_body, 0)

    o_sel = osel_ref[...] if sel_given else _unstack_heads(acc_sc[0] / l_sc[0], tq)
    o_win = _unstack_heads(acc_sc[1] / l_sc[1], tq)
    gsig = jax.nn.sigmoid(gate_ref[...])
    oc = oc_ref[...]
    outs = []
    for r in range(HEADS_PER_GROUP):
        cs = slice(r * HD, (r + 1) * HD)
        outs.append(oc[:, cs] * gsig[:, 3 * r:3 * r + 1] + o_sel[:, cs] * gsig[:, 3 * r + 1:3 * r + 2]
                    + o_win[:, cs] * gsig[:, 3 * r + 2:3 * r + 3])
    o_ref[...] = jnp.concatenate(outs, axis=1).astype(o_ref.dtype)


def nsa_select_window(q_rot, ks, vs, kw, vw, selmask, o_cmp, gates, *, qpos0, kwpos0, o_sel=None):
    b, tqn, _ = q_rot.shape
    sel_given = o_sel is not None
    lw = _kv_len(kw)
    lp, w = (lw, LANES) if sel_given else (_kv_len(ks), selmask.shape[3])
    tq = _pick(tqn, 256)
    tk = _kv_tile(lp, tq)
    tkw = _kv_tile(lw, tq)
    kern = functools.partial(_nsa_sw_kernel, tq=tq, tk=tk, tkw=tkw, qpos0=qpos0, lp=lp, lw=lw, kwpos0=kwpos0, w=w,
                             sel_given=sel_given)
    qspec = pl.BlockSpec((None, tq, GW), lambda bi, g, i: (bi, i, g))
    gspec = pl.BlockSpec((None, None, tq, LANES), lambda bi, g, i: (bi, g, i, 0))
    bg = lambda bi, g, i: (bi, g)
    if sel_given:
        in_specs = [qspec, _kv_spec(kw, bg), _kv_spec(vw, bg), qspec, qspec, gspec]
        args = (q_rot, kw, vw, o_sel, o_cmp, gates)
    else:
        in_specs = [qspec, _kv_spec(ks, bg), _kv_spec(vs, bg), _kv_spec(kw, bg), _kv_spec(vw, bg),
                    pl.BlockSpec((None, None, tq, w), lambda bi, g, i: (bi, g, i, 0)), qspec, gspec]
        args = (q_rot, ks, vs, kw, vw, selmask, o_cmp, gates)
    return pl.pallas_call(
        kern, grid=(b, GROUPS, tqn // tq), in_specs=in_specs, out_specs=qspec,
        out_shape=jax.ShapeDtypeStruct(q_rot.shape, BF16),
        scratch_shapes=_softmax_scratch(2, HEADS_PER_GROUP * tq),
        compiler_params=_cparams("parallel", "parallel", "parallel"), name="nsa_select_window",
    )(*args)


def _ik_kernel(t_ref, g_ref, c_ref, u_ref, d_ref, ikf_ref, ik2_ref, *, half):
    x = t_ref[...]
    lane = lax.broadcasted_iota(I32, (1, LANES), 1)
    xm = jnp.where(lane < IDX_DIM, x, 0.0)
    ms = jnp.sum(xm * xm, axis=-1, keepdims=True) * (1.0 / IDX_DIM)
    y = xm * lax.rsqrt(ms + EPS) * g_ref[...]
    y = _rope_chunk(y, c_ref[...], u_ref[...], d_ref[...], half)
    ikf_ref[...] = y
    ik2_ref[...] = (y + pltpu.roll(y, IDX_DIM, 1)).astype(ik2_ref.dtype)


def index_keys(tail, knorm, tabs):
    t = tail.shape[0]
    tables, half = tabs
    tm = _pick(tables[0].shape[0], 512)
    npb = tables[0].shape[0] // tm
    g = jnp.concatenate([knorm.astype(F32), jnp.zeros((LANES - IDX_DIM,), F32)]).reshape(1, LANES)
    rspec = pl.BlockSpec((tm, LANES), lambda i: (i, 0))
    tspec = pl.BlockSpec((tm, LANES), lambda i: (i % npb, 0))
    return pl.pallas_call(
        functools.partial(_ik_kernel, half=half), grid=(t // tm,),
        in_specs=[rspec, pl.BlockSpec((1, LANES), lambda i: (0, 0)), tspec, tspec, tspec],
        out_specs=[rspec, rspec],
        out_shape=[jax.ShapeDtypeStruct((t, LANES), F32), jax.ShapeDtypeStruct((t, LANES), BF16)],
        compiler_params=_cparams("parallel"), name="index_keys",
    )(tail, g, *tables)


def _dsa_index_kernel(iq_ref, ik_ref, tail_ref, mask_ref, sc_sc, *, tq, tk, qpos0, lp, top, n_widths):
    i = pl.program_id(1)
    iq = iq_ref[...]
    lane = lax.broadcasted_iota(I32, (1, LANES), 1)
    lo = lane < IDX_DIM
    hi = jnp.logical_not(lo)
    zero = jnp.zeros((), iq.dtype)
    parts = []
    for h in range(IDX_HEADS):
        pair = iq[:, (h // 2) * LANES:(h // 2 + 1) * LANES]
        parts.append(jnp.where(lo if h % 2 == 0 else hi, pair, zero))
    iqst = _concat_rows(parts)
    iw = tail_ref[:, IDX_DIM:IDX_DIM + IDX_HEADS] * ((IDX_DIM ** -0.5) * (IDX_HEADS ** -0.5))
    qpos = qpos0 + i * tq + lax.broadcasted_iota(I32, (tq, 1), 0)
    sc_sc[...] = jnp.full(sc_sc.shape, NEG_INF, F32)

    def body(j, carry):
        off = pl.multiple_of(j * tk, tk)
        sc = _qk(iqst, ik_ref[pl.ds(off, tk), :])
        acc = jnp.zeros((tq, tk), F32)
        for h in range(IDX_HEADS):
            acc = acc + jnp.maximum(sc[h * tq:(h + 1) * tq], 0.0) * iw[:, h:h + 1]
        kpos = off + lax.broadcasted_iota(I32, (1, tk), 1)
        sc_sc[:, pl.ds(off, tk)] = jnp.where(kpos <= qpos, acc, NEG_INF)
        return carry

    lax.fori_loop(0, _n_kv_tiles(qpos0, i, tq, tk, lp), body, 0)
    key = _order_key(sc_sc[...])
    quarter = lp // 4
    if n_widths > 1 and quarter % LANES == 0 and quarter > top:
        reach = jnp.clip((qpos0 + (i + 1) * tq + quarter - 1) // quarter, 1, 4)
        thr = lax.switch(reach - 1, [functools.partial(_kth_largest_key, key[:, :quarter * n], top)
                                     for n in range(1, 5)])
    else:
        thr = _kth_largest_key(key, top)
    kpos_all = lax.broadcasted_iota(I32, (1, lp), 1)
    mask_ref[...] = jnp.where(_select_top(key, thr, top) & (kpos_all <= qpos), 1.0, 0.0).astype(mask_ref.dtype)


def dsa_select(iq, ik2, tail, *, qpos0, top):
    b, tqn, _ = iq.shape
    lp = ik2.shape[1]
    tq = _pick(tqn, 128)
    tk = _kv_tile(lp, tq)
    kern = functools.partial(_dsa_index_kernel, tq=tq, tk=tk, qpos0=qpos0, lp=lp, top=top,
                             n_widths=4 if tqn > tq else 1)
    return pl.pallas_call(
        kern, grid=(b, tqn // tq),
        in_specs=[pl.BlockSpec((None, tq, IDX_HEADS * IDX_DIM), lambda bi, i: (bi, i, 0)),
                  pl.BlockSpec((None, lp, LANES), lambda bi, i: (bi, 0, 0)),
                  pl.BlockSpec((None, tq, LANES), lambda bi, i: (bi, i, 0))],
        out_specs=pl.BlockSpec((None, tq, lp), lambda bi, i: (bi, i, 0)),
        out_shape=jax.ShapeDtypeStruct((b, tqn, lp), BF16),
        scratch_shapes=[pltpu.VMEM((tq, lp), F32)],
        compiler_params=_cparams("parallel", "parallel"), name="dsa_select",
    )(iq, ik2, tail)


def _masked_attn_kernel(q_ref, k_ref, v_ref, mask_ref, o_ref, m_sc, l_sc, acc_sc, *, tq, tk, qpos0, lp):
    i = pl.program_id(1)
    qst = _stack_heads(q_ref[...])
    _init_state(m_sc, l_sc, acc_sc)

    def body(j, carry):
        off = pl.multiple_of(j * tk, tk)
        kt = k_ref[pl.ds(off, tk), :]
        vt = v_ref[pl.ds(off, tk), :]
        mask = mask_ref[:, pl.ds(off, tk)] > 0.5
        _online_update(0, _mask_rows(mask, _qk(qst, kt), tq), vt, m_sc, l_sc, acc_sc)
        return carry

    lax.fori_loop(0, _n_kv_tiles(qpos0, i, tq, tk, lp), body, 0)
    o_ref[...] = _unstack_heads(acc_sc[0] / l_sc[0], tq).astype(o_ref.dtype)


def masked_attention(q, k, v, mask, *, qpos0):
    b, tqn, _ = q.shape
    lp = _kv_len(k)
    tq = _pick(tqn, 256)
    tk = _kv_tile(lp, tq)
    kern = functools.partial(_masked_attn_kernel, tq=tq, tk=tk, qpos0=qpos0, lp=lp)
    qspec = pl.BlockSpec((None, tq, GW), lambda bi, i, g: (bi, i, g))
    bg = lambda bi, i, g: (bi, g)
    return pl.pallas_call(
        kern, grid=(b, tqn // tq, GROUPS),
        in_specs=[qspec, _kv_spec(k, bg), _kv_spec(v, bg),
                  pl.BlockSpec((None, tq, lp), lambda bi, i, g: (bi, i, 0))],
        out_specs=qspec,
        out_shape=jax.ShapeDtypeStruct(q.shape, BF16),
        scratch_shapes=_softmax_scratch(1, HEADS_PER_GROUP * tq),
        compiler_params=_cparams("parallel", "parallel", "parallel"), name="masked_attention",
    )(q, k, v, mask)


def _cross_kernel(q_ref, k_ref, v_ref, o_ref):
    q = q_ref[...]
    k = k_ref[...].astype(BF16)
    v = v_ref[...].astype(BF16)
    outs = []
    for h in range(X_HEADS):
        cs = slice(h * X_HD, (h + 1) * X_HD)
        s = _qk(q[:, cs], k[:, cs]) * (X_HD ** -0.5)
        e = jnp.exp(s - jnp.max(s, axis=-1, keepdims=True))
        p = e / jnp.sum(e, axis=-1, keepdims=True)
        outs.append(jnp.dot(p.astype(BF16), v[:, cs], preferred_element_type=F32))
    o_ref[...] = jnp.concatenate(outs, axis=1).astype(o_ref.dtype)


def cross_attention(q, mk, mv):
    b, tqn, _ = q.shape
    tq = _pick(tqn, 512)
    ml = mk.shape[1]
    qspec = pl.BlockSpec((None, tq, X_W), lambda bi, i: (bi, i, 0))
    mspec = pl.BlockSpec((None, ml, X_W), lambda bi, i: (bi, 0, 0))
    return pl.pallas_call(
        _cross_kernel, grid=(b, tqn // tq), in_specs=[qspec, mspec, mspec], out_specs=qspec,
        out_shape=jax.ShapeDtypeStruct(q.shape, BF16),
        compiler_params=_cparams("parallel", "parallel"), name="cross_attention",
    )(q, mk, mv)


PAGES_PER_STEP = 16


def _gather_kernel(pt_ref, *refs, pps):
    pools, new_ref, o_ref = refs[:pps], refs[pps], refs[pps + 1]
    s = pl.program_id(1)
    n_steps = pl.num_programs(1) - 1

    @pl.when(s < n_steps)
    def _():
        for k in range(pps):
            x = pools[k][...].astype(o_ref.dtype)
            o_ref[k * PAGE_SIZE:(k + 1) * PAGE_SIZE, :] = jnp.concatenate([x, x], axis=1)

    @pl.when(s == n_steps)
    def _():
        o_ref[...] = jnp.zeros(o_ref.shape, o_ref.dtype)
        o_ref[0:PAGE_SIZE, :] = new_ref[...]


def gather_index_keys(pool, layer, page_table, new_rows):
    b, n_pages = page_table.shape
    pps = math.gcd(PAGES_PER_STEP, n_pages)
    n_steps = n_pages // pps
    rows = pps * PAGE_SIZE
    grid_spec = pltpu.PrefetchScalarGridSpec(
        num_scalar_prefetch=1, grid=(b, n_steps + 1),
        in_specs=_page_specs(pool, layer, pps, n_pages)
        + [pl.BlockSpec((None, PAGE_SIZE, LANES), lambda bi, s, pt: (bi, 0, 0))],
        out_specs=pl.BlockSpec((None, rows, LANES), lambda bi, s, pt: (bi, s, 0)))
    return pl.pallas_call(
        functools.partial(_gather_kernel, pps=pps), grid_spec=grid_spec,
        out_shape=jax.ShapeDtypeStruct((b, (n_steps + 1) * rows, LANES), BF16),
        compiler_params=_cparams("parallel", "arbitrary"), name="gather_index_keys",
    )(page_table, *([pool] * pps), new_rows)


def _paged_steps(pool, page_table):
    n_pages = page_table.shape[1]
    view = pool.reshape(pool.shape[0], pool.shape[1], PAGE_SIZE * GROUPS, HD)
    pps = math.gcd(PAGES_PER_STEP, n_pages)
    return view, pps, n_pages // pps


def _page_specs(view, layer, pps, n_pages):
    def spec(k):
        return pl.BlockSpec((None, None) + view.shape[2:],
                            lambda bi, s, pt: (layer, pt[bi, jnp.minimum(s * pps + k, n_pages - 1)], 0, 0))
    return [spec(k) for k in range(pps)]


def _group_rows(page_refs, g):
    return jnp.concatenate([p[pl.ds(g, PAGE_SIZE, stride=GROUPS), :].astype(BF16) for p in page_refs], axis=0)


def _paged_decode_kernel(pt_ref, *refs, pps, mode, qpos0, past_len, post):
    it = iter(refs)
    lam_ref = next(it) if mode == 'diff' else None
    q_ref = next(it)
    kpools = [next(it) for _ in range(pps)]
    vpools = [next(it) for _ in range(pps)]
    knew_ref, vnew_ref = next(it), next(it)
    mask_ref = next(it) if mode != 'diff' else None
    g_ref = next(it) if mode == 'diff' else None
    o_ref, m_sc, l_sc, acc_sc = next(it), next(it), next(it), next(it)

    step = pl.program_id(1)
    n_steps = pl.num_programs(1) - 1
    tq = q_ref.shape[0]
    rows = pps * PAGE_SIZE
    qpos = qpos0 + lax.broadcasted_iota(I32, (tq, 1), 0)

    @pl.when(step == 0)
    def _():
        _init_state(m_sc, l_sc, acc_sc)

    def queries(g):
        q = q_ref[:, g * GW:(g + 1) * GW]
        if mode != 'diff':
            return _stack_heads(q)
        lo = lax.broadcasted_iota(I32, (1, HD), 1) < A_HALF
        return _concat_rows([_stack_heads(q, lo), _stack_heads(q, jnp.logical_not(lo))])

    def attend(g, kt, vt, mask):
        s = _qk(queries(g), kt)
        if mask is not None:
            s = _mask_rows(mask, s, tq)
        _online_update(g, s, vt, m_sc, l_sc, acc_sc)

    @pl.when(step < n_steps)
    def _():
        mask = mask_ref[...] > 0.5 if mode == 'mask' else None
        if mode == 'chunks':
            assert rows // CMP_STRIDE == LANES
            kpos = step * rows + lax.broadcasted_iota(I32, (1, rows), 1)
            chunk_row = step * LANES + lax.broadcasted_iota(I32, (LANES, 1), 0)
            expand = jnp.where(chunk_row == (kpos >> 4), 1.0, 0.0).astype(BF16)
        for g in range(GROUPS):
            if mode == 'chunks':
                mask = jnp.dot(mask_ref[g], expand, preferred_element_type=F32) > 0.5
            attend(g, _group_rows(kpools, g), _group_rows(vpools, g), mask)

    @pl.when(step == n_steps)
    def _():
        kpos = past_len + lax.broadcasted_iota(I32, (1, PAGE_SIZE), 1)
        mask = kpos <= qpos
        if mode == 'mask':
            mask = mask & (mask_ref[:, 0:PAGE_SIZE] > 0.5)
        for g in range(GROUPS):
            attend(g, knew_ref[g], vnew_ref[g], mask)
        for g in range(GROUPS):
            if mode == 'diff':
                on = acc_sc[g] / l_sc[g]
                o = on[:HEADS_PER_GROUP * tq] - lam_ref[0] * on[HEADS_PER_GROUP * tq:]
                ms = jnp.mean(o * o, axis=-1, keepdims=True)
                o = o * lax.rsqrt(ms + EPS) * g_ref[...] * post
            else:
                o = acc_sc[g] / l_sc[g]
            o_ref[:, g * GW:(g + 1) * GW] = _unstack_heads(o, tq).astype(o_ref.dtype)


def paged_decode_attention(q, kpool, vpool, layer, page_table, k_new, v_new, *, mode, qpos0, out_dtype=BF16,
                           mask=None, lam_f=None, subln=None, lam_init=0.0):
    b, tq, _ = q.shape
    n_pages = page_table.shape[1]
    kview, pps, n_steps = _paged_steps(kpool, page_table)
    vview, _, _ = _paged_steps(vpool, page_table)
    rows = pps * PAGE_SIZE
    n_c = 2 if mode == 'diff' else 1
    in_specs, args = [], []
    if mode == 'diff':
        in_specs.append(pl.BlockSpec(memory_space=pltpu.SMEM))
        args.append(lam_f.reshape(1).astype(F32))
    in_specs.append(pl.BlockSpec((None, tq, D_MODEL), lambda bi, s, pt: (bi, 0, 0)))
    args.append(q)
    in_specs += _page_specs(kview, layer, pps, n_pages) + _page_specs(vview, layer, pps, n_pages)
    args += [kview] * pps + [vview] * pps
    new_spec = pl.BlockSpec((None, GROUPS, PAGE_SIZE, HD), lambda bi, s, pt: (bi, 0, 0, 0))
    in_specs += [new_spec, new_spec]
    args += [k_new, v_new]
    if mode == 'mask':
        in_specs.append(pl.BlockSpec((None, tq, rows), lambda bi, s, pt: (bi, 0, s)))
        args.append(mask)
    elif mode == 'chunks':
        in_specs.append(pl.BlockSpec((None, GROUPS, tq, LANES), lambda bi, s, pt: (bi, 0, 0, s)))
        args.append(mask)
    else:
        in_specs.append(pl.BlockSpec((1, HD), lambda bi, s, pt: (0, 0)))
        args.append(subln.reshape(1, HD))
    kern = functools.partial(_paged_decode_kernel, pps=pps, mode=mode, qpos0=qpos0,
                             past_len=n_pages * PAGE_SIZE, post=1.0 - lam_init)
    grid_spec = pltpu.PrefetchScalarGridSpec(
        num_scalar_prefetch=1, grid=(b, n_steps + 1), in_specs=in_specs,
        out_specs=pl.BlockSpec((None, tq, D_MODEL), lambda bi, s, pt: (bi, 0, 0)),
        scratch_shapes=_softmax_scratch(GROUPS, n_c * HEADS_PER_GROUP * tq))
    return pl.pallas_call(
        kern, grid_spec=grid_spec, out_shape=jax.ShapeDtypeStruct(q.shape, out_dtype),
        compiler_params=_cparams("parallel", "arbitrary"), name="paged_decode_attention",
    )(page_table, *args)


def _paged_compress_kernel(pt_ref, *refs, pps):
    pools, w_ref, o_ref = refs[:pps], refs[pps], refs[pps + 1]
    cpp = PAGE_SIZE // CMP_STRIDE
    acc = jnp.zeros((GROUPS * pps * cpp, 2 * HD), F32)
    for j in range(CMP_STRIDE):
        x = jnp.concatenate([p[pl.ds(GROUPS * j + g, cpp, stride=GROUPS * CMP_STRIDE), :]
                             for g in range(GROUPS) for p in pools], axis=0)
        acc = acc + jnp.dot(x.astype(BF16), w_ref[j], preferred_element_type=F32)
    for g in range(GROUPS):
        o_ref[g] = acc[g * pps * cpp:(g + 1) * pps * cpp]


def paged_compress(pool, layer, page_table, w):
    b, n_pages = page_table.shape
    view, pps, n_steps = _paged_steps(pool, page_table)
    cps = pps * (PAGE_SIZE // CMP_STRIDE)
    grid_spec = pltpu.PrefetchScalarGridSpec(
        num_scalar_prefetch=1, grid=(b, n_steps),
        in_specs=_page_specs(view, layer, pps, n_pages) + [pl.BlockSpec(w.shape, lambda bi, s, pt: (0, 0, 0))],
        out_specs=pl.BlockSpec((None, GROUPS, cps, 2 * HD), lambda bi, s, pt: (bi, 0, s, 0)))
    return pl.pallas_call(
        functools.partial(_paged_compress_kernel, pps=pps), grid_spec=grid_spec,
        out_shape=jax.ShapeDtypeStruct((b, GROUPS, n_steps * cps, 2 * HD), F32),
        compiler_params=_cparams("parallel", "parallel"), name="paged_compress",
    )(page_table, *([view] * pps), w)


def lambda_init(layer):
    return 0.8 - 0.6 * math.exp(-0.3 * layer)


def _pad_rows(x, n):
    return jnp.pad(x, ((0, 0), (0, n - x.shape[1]), (0, 0)))


def trunk(x, nb, tq_real, qpos0, mem_k, mem_v, W, past):
    t = nb * tq_real
    tqp = max(tq_real, SUBLANES)
    pos = qpos0 + jnp.arange(tq_real, dtype=I32)
    pos_rows = jnp.tile(pos, nb) if tq_real < SUBLANES else pos
    tab64 = rope_tables(pos_rows, 64)
    tab128 = rope_tables(pos_rows, 128)
    new = {}

    def to_attn(a):
        return _pad_rows(a.reshape(nb, tq_real, a.shape[-1]), tqp)

    def from_attn(a):
        return a[:, :tq_real].reshape(t, a.shape[-1])

    if past is not None:
        pt = past['page_table']
        past_len = pt.shape[1] * PAGE_SIZE

        def new_block(new_bf):
            nr = new_bf.reshape(nb, tq_real, GROUPS, HD).transpose(0, 2, 1, 3)
            return jnp.pad(nr, ((0, 0), (0, 0), (0, PAGE_SIZE - tq_real), (0, 0)))
    else:
        past_len = 0

    for l in range(DEPTH):
        x = yield from _ffn_steps(x, W['norm_ffn1'], W['ffn1_w_in'], W['ffn1_w_out'], l)
        h = rmsnorm(x, W['norm_mix'][l], BF16)
        i = l // N_MIXERS
        kind = l % N_MIXERS
        if kind == 0:
            proj = yield (linear, h, None, dict(w=W['a_w_in'], layer=i))
            (q_rot,) = take_cols(proj, 0, A_Q, (BF16,), tab64, scale=_logit_scale(A_HALF))
            k_f, k_b = take_cols(proj, A_Q, A_K, (F32, BF16), tab64)
            (v_b,) = take_cols(proj, A_Q + A_K, A_KV_HEADS * A_VDIM, (BF16,))
            v_f = proj[:, A_Q + A_K:]
            new.setdefault('a_k', []).append(k_f.reshape(nb, tq_real, A_KV_HEADS, 2 * A_HALF))
            new.setdefault('a_v', []).append(v_f.reshape(nb, tq_real, A_KV_HEADS, A_VDIM))
            lam = W['a_lambda'][i]
            lam_f = (jnp.exp(jnp.sum(lam[0] * lam[1])) - jnp.exp(jnp.sum(lam[2] * lam[3]))).astype(F32) + lambda_init(l)
            if past is None:
                o = diff_attention(to_attn(q_rot), k_b.reshape(nb, tq_real, -1), v_b.reshape(nb, tq_real, -1),
                                   lam_f, W['a_subln'][i], lambda_init(l), qpos0)
            else:
                o = paged_decode_attention(to_attn(q_rot), past['a_k'], past['a_v'], i, pt, new_block(k_b),
                                           new_block(v_b), mode='diff', qpos0=qpos0, lam_f=lam_f,
                                           subln=W['a_subln'][i], lam_init=lambda_init(l))
            y_in, w_out = from_attn(o), W['a_w_out']
        elif kind == 1:
            w_in = W['b_w_in']
            n_main = B_Q + 6 * B_KV
            proj = yield (linear, h, None, dict(w=w_in, layer=i, ncols=n_main))
            w_tail = jnp.pad(w_in[i, :, n_main:], ((0, 0), (0, LANES - (w_in.shape[-1] - n_main))))
            gate_logits = yield (linear, h, None, dict(w=w_tail, tn=LANES))
            (q_raw,) = take_cols(proj, 0, B_Q, (BF16,))
            (q_rot,) = take_cols(proj, 0, B_Q, (BF16,), tab128, scale=_logit_scale(B_HD))
            kc_f = proj[:, B_Q:B_Q + B_KV]
            vc_f = proj[:, B_Q + B_KV:B_Q + 2 * B_KV]
            ks_f, ks_b = take_cols(proj, B_Q + 2 * B_KV, B_KV, (F32, BF16), tab128)
            vs_f = proj[:, B_Q + 3 * B_KV:B_Q + 4 * B_KV]
            (vs_b,) = take_cols(proj, B_Q + 3 * B_KV, B_KV, (BF16,))
            kw_f, kw_b = take_cols(proj, B_Q + 4 * B_KV, B_KV, (F32, BF16), tab128)
            vw_f = proj[:, B_Q + 5 * B_KV:B_Q + 6 * B_KV]
            (vw_b,) = take_cols(proj, B_Q + 5 * B_KV, B_KV, (BF16,))
            shp = (nb, tq_real, B_KV_HEADS, B_HD)
            for nm, a in (('b_cmp_k', kc_f), ('b_cmp_v', vc_f), ('b_sel_k', ks_f), ('b_sel_v', vs_f)):
                new.setdefault(nm, []).append(a.reshape(shp))

            cw = W['b_cmp_w'][i]
            wfs = [jnp.concatenate([cw[s, :CMP_STRIDE], cw[s, CMP_STRIDE:]], axis=-1) for s in range(2)]
            if past is None:
                l_all = tq_real
                ks_all, vs_all = ks_b.reshape(nb, tq_real, -1), vs_b.reshape(nb, tq_real, -1)
                kw_all, vw_all = kw_b.reshape(nb, tq_real, -1), vw_b.reshape(nb, tq_real, -1)
                kwpos0 = 0
                keep = min(WINDOW, tq_real)
                new.setdefault('b_win_k', []).append(kw_f.reshape(shp)[:, tq_real - keep:])
                new.setdefault('b_win_v', []).append(vw_f.reshape(shp)[:, tq_real - keep:])
                lp_sel = tq_real
            else:
                l_all = past_len + tq_real
                bkw = past['b_win_k'][i].reshape(nb, -1, B_KV)
                bvw = past['b_win_v'][i].reshape(nb, -1, B_KV)
                wb = bkw.shape[1]
                kb = jnp.concatenate([bkw, kw_f.reshape(nb, tq_real, B_KV)], axis=1)
                vb = jnp.concatenate([bvw, vw_f.reshape(nb, tq_real, B_KV)], axis=1)
                new.setdefault('b_win_k', []).append(kb[:, tq_real:].reshape(nb, wb, B_KV_HEADS, B_HD))
                new.setdefault('b_win_v', []).append(vb[:, tq_real:].reshape(nb, wb, B_KV_HEADS, B_HD))
                lw = -(-(wb + tq_real) // 512) * 512
                kw_all = _pad_rows(kb, lw).astype(BF16)
                vw_all = _pad_rows(vb, lw).astype(BF16)
                kwpos0 = qpos0 - wb
                lp_sel = past_len + math.gcd(PAGES_PER_STEP, pt.shape[1]) * PAGE_SIZE
            nch = (l_all // CMP_STRIDE)
            nblk = nch - 1
            n_sel = -(-l_all // SEL_BLOCK)
            n_rep = nch // (SEL_BLOCK // CMP_STRIDE)
            assert nch % (SEL_BLOCK // CMP_STRIDE) == 0 and n_sel - n_rep in (0, 1)
            n_top_rep = min(SEL_N, n_sel) - (n_sel - n_rep)
            wch = -(-(lp_sel // CMP_STRIDE) // LANES) * LANES

            def partials(rows, wf):
                xg = rows.reshape(nb, nch, CMP_STRIDE, B_KV_HEADS, B_HD)
                xg = xg.transpose(0, 3, 1, 2, 4).reshape(nb * B_KV_HEADS * nch, CMP_STRIDE * B_HD)
                return linear(xg, wf.reshape(CMP_STRIDE * B_HD, 2 * B_HD), tn=2 * B_HD).reshape(
                    nb, B_KV_HEADS, nch, 2 * B_HD)

            if past is None:
                fsk, fsv = partials(kc_f, wfs[0]), partials(vc_f, wfs[1])
            else:
                fsk = paged_compress(past['b_cmp_k'], i, pt, wfs[0].astype(BF16))
                fsv = paged_compress(past['b_cmp_v'], i, pt, wfs[1].astype(BF16))
            fsk, fsv = (jnp.pad(f, ((0, 0), (0, 0), (0, wch - nch), (0, 0))) for f in (fsk, fsv))
            o_cmp, selmask = nsa_compressed(to_attn(q_raw), fsk, fsv, W['b_cmp_b'][i].astype(F32),
                                            nblk=nblk, nch=nch, qpos0=qpos0, n_top_rep=n_top_rep)
            gates = gate_logits[:, :3 * B_HEADS].reshape(nb, tq_real, B_KV_HEADS, 3 * HEADS_PER_GROUP)
            gates = jnp.pad(gates.transpose(0, 2, 1, 3),
                            ((0, 0), (0, 0), (0, tqp - tq_real), (0, LANES - 3 * HEADS_PER_GROUP)))
            if past is None:
                o = nsa_select_window(to_attn(q_rot), ks_all, vs_all, kw_all, vw_all, selmask, o_cmp, gates,
                                      qpos0=qpos0, kwpos0=kwpos0)
            else:
                o_sel = paged_decode_attention(to_attn(q_rot), past['b_sel_k'], past['b_sel_v'], i, pt,
                                               new_block(ks_b), new_block(vs_b), mode='chunks', qpos0=qpos0,
                                               mask=selmask, out_dtype=F32)
                o = nsa_select_window(to_attn(q_rot), None, None, kw_all, vw_all, None, o_cmp, gates,
                                      qpos0=qpos0, kwpos0=kwpos0, o_sel=o_sel)
            y_in, w_out = from_attn(o), W['b_w_out']
        else:
            w_in = W['c_w_in']
            n_main = C_Q + 2 * C_KV + IDX_HEADS * IDX_DIM
            proj = yield (linear, h, None, dict(w=w_in, layer=i, ncols=n_main))
            w_tail = jnp.pad(w_in[i, :, n_main:], ((0, 0), (0, LANES - (w_in.shape[-1] - n_main))))
            tail = yield (linear, h, None, dict(w=w_tail, tn=LANES))
            (q_rot,) = take_cols(proj, 0, C_Q, (BF16,), tab128, scale=_logit_scale(C_HD))
            k_f, k_b = take_cols(proj, C_Q, C_KV, (F32, BF16), tab128)
            v_f = proj[:, C_Q + C_KV:C_Q + 2 * C_KV]
            (v_b,) = take_cols(proj, C_Q + C_KV, C_KV, (BF16,))
            (iq_rot,) = take_cols(proj, C_Q + 2 * C_KV, IDX_HEADS * IDX_DIM, (BF16,), tab64)
            ik_f, ik2 = index_keys(tail, W['c_idx_knorm'][i], tab64)
            shp = (nb, tq_real, C_KV_HEADS, C_HD)
            new.setdefault('c_k', []).append(k_f.reshape(shp))
            new.setdefault('c_v', []).append(v_f.reshape(shp))
            new.setdefault('c_idx_k', []).append(ik_f[:, :IDX_DIM].reshape(nb, tq_real, IDX_DIM))
            l_all = past_len + tq_real
            top = min(IDX_TOPK_MAX, l_all // 4)
            if past is None:
                mask = dsa_select(to_attn(iq_rot), ik2.reshape(nb, tq_real, LANES), to_attn(tail), qpos0=qpos0, top=top)
                o = masked_attention(to_attn(q_rot), k_b.reshape(nb, tq_real, -1), v_b.reshape(nb, tq_real, -1),
                                     mask, qpos0=qpos0)
            else:
                ik_all = gather_index_keys(past['c_idx_k'], i, pt,
                                           _pad_rows(ik2.reshape(nb, tq_real, LANES), PAGE_SIZE))
                mask = dsa_select(to_attn(iq_rot), ik_all, to_attn(tail), qpos0=qpos0, top=top)
                o = paged_decode_attention(to_attn(q_rot), past['c_k'], past['c_v'], i, pt, new_block(k_b),
                                           new_block(v_b), mode='mask', qpos0=qpos0, mask=mask)
            y_in, w_out = from_attn(o), W['c_w_out']
        x = yield (linear, y_in, x, dict(w=w_out, layer=i, scale=1.0))

        h = rmsnorm(x, W['norm_cross'][l], BF16)
        qx = yield (linear, h, None, dict(w=W['x_w_q'], layer=l, out_dtype=BF16))
        ox = cross_attention(to_attn(qx), mem_k[l], mem_v[l])
        x = yield (linear, from_attn(ox), x, dict(w=W['x_w_o'], layer=l, scale=1.0))
        x = yield from _ffn_steps(x, W['norm_ffn2'], W['ffn2_w_in'], W['ffn2_w_out'], l)
    y = rmsnorm(x, W['final_norm'], F32)
    return y, {nm: jnp.stack(v) for nm, v in new.items()}


def kernel(x_prompt, x_sample, cache_a_k, cache_a_v, cache_b_cmp_k, cache_b_cmp_v, cache_b_sel_k, cache_b_sel_v, state_b_win_k, state_b_win_v, cache_c_k, cache_c_v, cache_c_idx_k, cache_mem_k, cache_mem_v, page_table, mem_prompt, norm_ffn1, norm_mix, norm_cross, norm_ffn2, final_norm, ffn1_w_in, ffn1_w_out, ffn2_w_in, ffn2_w_out, x_w_q, x_w_kv, x_w_o, a_w_in, a_w_out, a_lambda, a_subln, b_w_in, b_w_out, b_cmp_w, b_cmp_b, c_w_in, c_w_out, c_idx_knorm):
    W = dict(norm_ffn1=norm_ffn1, norm_mix=norm_mix, norm_cross=norm_cross, norm_ffn2=norm_ffn2,
             final_norm=final_norm, ffn1_w_in=ffn1_w_in, ffn1_w_out=ffn1_w_out, ffn2_w_in=ffn2_w_in,
             ffn2_w_out=ffn2_w_out, x_w_q=x_w_q, x_w_o=x_w_o, a_w_in=a_w_in, a_w_out=a_w_out,
             a_lambda=a_lambda, a_subln=a_subln, b_w_in=b_w_in, b_w_out=b_w_out, b_cmp_w=b_cmp_w,
             b_cmp_b=b_cmp_b, c_w_in=c_w_in, c_w_out=c_w_out, c_idx_knorm=c_idx_knorm)
    nbp, seq, d = x_prompt.shape
    nbs, dseq, _ = x_sample.shape
    ml = mem_prompt.shape[1]

    mem2d = mem_prompt.reshape(nbp * ml, d)
    mkv = [linear(mem2d, x_w_kv, l) for l in range(DEPTH)]
    p_mem_k = jnp.stack([m[:, :X_W].reshape(nbp, ml, X_HEADS, X_HD) for m in mkv])
    p_mem_v = jnp.stack([m[:, X_W:].reshape(nbp, ml, X_HEADS, X_HD) for m in mkv])
    prompt = trunk(x_prompt.reshape(nbp * seq, d), nbp, seq, 0,
                   p_mem_k.reshape(DEPTH, nbp, ml, X_W), p_mem_v.reshape(DEPTH, nbp, ml, X_W), W, None)

    past = dict(a_k=cache_a_k, a_v=cache_a_v, b_cmp_k=cache_b_cmp_k, b_cmp_v=cache_b_cmp_v,
                b_sel_k=cache_b_sel_k, b_sel_v=cache_b_sel_v, b_win_k=state_b_win_k, b_win_v=state_b_win_v,
                c_k=cache_c_k, c_v=cache_c_v, c_idx_k=cache_c_idx_k, page_table=page_table)
    past_len = page_table.shape[1] * PAGE_SIZE
    sml = cache_mem_k.shape[2]
    sample = trunk(x_sample.reshape(nbs * dseq, d), nbs, dseq, past_len,
                   cache_mem_k.reshape(DEPTH, nbs, sml, X_W), cache_mem_v.reshape(DEPTH, nbs, sml, X_W), W, past)
    (y_p, ps), (y_s, ss) = _run_together([prompt, sample])

    return (y_p.reshape(nbp, seq, d), y_s.reshape(nbs, dseq, d),
            ps['a_k'], ps['a_v'], ps['b_cmp_k'], ps['b_cmp_v'], ps['b_sel_k'], ps['b_sel_v'],
            ps['b_win_k'], ps['b_win_v'], ps['c_k'], ps['c_v'], ps['c_idx_k'], p_mem_k, p_mem_v,
            ss['a_k'], ss['a_v'], ss['b_cmp_k'], ss['b_cmp_v'], ss['b_sel_k'], ss['b_sel_v'],
            ss['b_win_k'], ss['b_win_v'], ss['c_k'], ss['c_v'], ss['c_idx_k'])
```

```python
import functools
import math

import jax
import jax.numpy as jnp
from jax import lax
from jax.experimental import pallas as pl
from jax.experimental.pallas import tpu as pltpu

F32 = jnp.float32
BF16 = jnp.bfloat16
I32 = jnp.int32

D_MODEL = 2048
DEPTH = 4
PAGE_SIZE = 128
N_MIXERS = 3
ROPE_THETA = 500000.0
EPS = 1e-6
NEG_INF = -1e30
FORCE = 1e9
D_FF = 256 * math.ceil(8 * D_MODEL / 3 / 256)

A_HEADS = D_MODEL // 128
A_HALF = 64
A_VDIM = 128
A_KV_HEADS = 4
A_Q = A_HEADS * 2 * A_HALF
A_K = A_KV_HEADS * 2 * A_HALF

B_HEADS = D_MODEL // 128
B_HD = 128
B_KV_HEADS = 4
B_KV = B_KV_HEADS * B_HD
B_Q = B_HEADS * B_HD
CMP_STRIDE = 16
CMP_LEN = 32
SEL_BLOCK = 64
SEL_N = 16
WINDOW = 512

C_HEADS = D_MODEL // 128
C_HD = 128
C_KV_HEADS = 4
C_Q = C_HEADS * C_HD
C_KV = C_KV_HEADS * C_HD
IDX_HEADS = 16
IDX_DIM = 64
IDX_TOPK_MAX = 256

MEM_LEN = 256
X_HEADS = 4
X_HD = 128
X_W = X_HEADS * X_HD

GROUPS = 4
HEADS_PER_GROUP = 4
HD = 128
GW = HEADS_PER_GROUP * HD

LANES = 128
SUBLANES = 8
VMEM_LIMIT_BYTES = 56 * 1024 * 1024
M_INIT = -1e29
LOG2E = 1.4426950408889634
INT_MIN = -2147483648


def _cparams(*sem):
    return pltpu.CompilerParams(dimension_semantics=sem, vmem_limit_bytes=VMEM_LIMIT_BYTES)


def _pick(n, pref):
    if n <= pref:
        return n
    t = pref
    while n % t:
        t //= 2
    return t


def _rms_kernel(x_ref, g_ref, o_ref):
    x = x_ref[...]
    ms = jnp.mean(x * x, axis=-1, keepdims=True)
    o_ref[...] = (x * lax.rsqrt(ms + EPS) * g_ref[...]).astype(o_ref.dtype)


def rmsnorm(x, g, out_dtype):
    t, d = x.shape
    tm = _pick(t, 1024)
    return pl.pallas_call(
        _rms_kernel,
        grid=(t // tm,),
        in_specs=[pl.BlockSpec((tm, d), lambda i: (i, 0)), pl.BlockSpec((1, d), lambda i: (0, 0))],
        out_specs=pl.BlockSpec((tm, d), lambda i: (i, 0)),
        out_shape=jax.ShapeDtypeStruct((t, d), out_dtype),
        compiler_params=_cparams("parallel"),
        name="rmsnorm",
    )(x, g.reshape(1, d))


def _dense_kernel(*refs, mode, scale, extra):
    it = iter(refs)
    a_ref = next(it)
    a2_ref = next(it) if extra else None
    w_refs = [next(it) for _ in range(2 if mode == 'swiglu' else 1)]
    r_ref = next(it) if mode == 'res' else None
    r2_ref = next(it) if mode == 'res' and extra else None
    o_ref = next(it)
    o2_ref = next(it) if extra else None
    wb_refs = [next(it) for _ in w_refs]

    def apply(a, r, out):
        a = a[...].astype(BF16)
        if mode == 'swiglu':
            g = jnp.dot(a, wb_refs[0][...], preferred_element_type=F32)
            u = jnp.dot(a, wb_refs[1][...], preferred_element_type=F32)
            y = g * jax.nn.sigmoid(g) * u
        else:
            y = jnp.dot(a, wb_refs[0][...], preferred_element_type=F32)
            if mode == 'res':
                y = r[...] + scale * y
        out[...] = y.astype(out.dtype)

    @pl.when(pl.program_id(1) == 0)
    def _():
        for w_ref, wb_ref in zip(w_refs, wb_refs):
            wb_ref[...] = w_ref[...].astype(BF16)
        if extra:
            apply(a2_ref, r2_ref, o2_ref)

    apply(a_ref, r_ref, o_ref)


def _w_spec(w, layer, k, tn, c0):
    if w.ndim == 3:
        return pl.BlockSpec((None, k, tn), lambda j, i: (layer, 0, j + c0))
    return pl.BlockSpec((k, tn), lambda j, i: (0, j + c0))


def _dense(a, w_specs, ws, *, mode, k, tn, ncols, out_dtype, res=None, scale=1.0, tm=None, name):
    pair = isinstance(a, (list, tuple))
    a_main, a_x = (a[0], a[1]) if pair else (a, None)
    r_main, r_x = (res[0], res[1]) if (pair and res is not None) else (res, None)
    m = a_main.shape[0]
    tm = _pick(m, (2048 if k <= D_MODEL else 512) if tm is None else tm)
    assert m % tm == 0 and ncols % tn == 0
    row_spec = pl.BlockSpec((tm, k), lambda j, i: (i, 0))
    o_spec = pl.BlockSpec((tm, tn), lambda j, i: (i, j))
    in_specs, args = [row_spec], [a_main]
    out_specs, out_shape = [o_spec], [jax.ShapeDtypeStruct((m, ncols), out_dtype)]
    if pair:
        mx = a_x.shape[0]
        x_out = pl.BlockSpec((mx, tn), lambda j, i: (0, j))
        in_specs.append(pl.BlockSpec((mx, k), lambda j, i: (0, 0)))
        args.append(a_x)
        out_specs.append(x_out)
        out_shape.append(jax.ShapeDtypeStruct((mx, ncols), out_dtype))
    in_specs += w_specs
    args += ws
    if mode == 'res':
        in_specs.append(o_spec)
        args.append(r_main)
        if pair:
            in_specs.append(x_out)
            args.append(r_x)
    outs = pl.pallas_call(
        functools.partial(_dense_kernel, mode=mode, scale=scale, extra=pair),
        grid=(ncols // tn, m // tm), in_specs=in_specs, out_specs=out_specs, out_shape=out_shape,
        scratch_shapes=[pltpu.VMEM((k, tn), BF16) for _ in w_specs],
        compiler_params=_cparams("parallel", "arbitrary"), name=name,
    )(*args)
    return list(outs) if pair else outs[0]


def linear(a, w, layer=0, *, col0=0, ncols=None, out_dtype=F32, tm=None, tn=512, res=None, scale=1.0):
    k = (a[0] if isinstance(a, (list, tuple)) else a).shape[1]
    ncols = w.shape[-1] - col0 if ncols is None else ncols
    tn = _pick(ncols, tn)
    assert col0 % tn == 0
    return _dense(a, [_w_spec(w, layer, k, tn, col0 // tn)], [w], mode='plain' if res is None else 'res', k=k, tn=tn,
                  ncols=ncols, out_dtype=F32 if res is not None else out_dtype, res=res, scale=scale, tm=tm,
                  name='linear' if res is None else 'linear_res')


def swiglu_in(a, w_in, layer):
    k = (a[0] if isinstance(a, (list, tuple)) else a).shape[1]
    tn = 512
    nj = D_FF // tn
    return _dense(a, [_w_spec(w_in, layer, k, tn, 0), _w_spec(w_in, layer, k, tn, nj)], [w_in, w_in], mode='swiglu',
                  k=k, tn=tn, ncols=D_FF, out_dtype=BF16, tm=1024, name='swiglu_in')


def _ffn_steps(x, g, w_in, w_out, layer):
    h = rmsnorm(x, g[layer], BF16)
    hid = yield (swiglu_in, h, None, dict(w_in=w_in, layer=layer))
    return (yield (linear, hid, x, dict(w=w_out, layer=layer, scale=0.5)))


def _run_together(gens):
    reqs = [next(g) for g in gens]
    results = [None] * len(gens)
    while True:
        fn, _, res0, kw = reqs[0]
        if res0 is not None:
            kw = dict(kw, res=[r[2] for r in reqs])
        outs = fn([r[1] for r in reqs], **kw)
        nxt = []
        for gi, (g, o) in enumerate(zip(gens, outs)):
            try:
                nxt.append(g.send(o))
            except StopIteration as stop:
                results[gi] = stop.value
        if not nxt:
            return results
        assert len(nxt) == len(gens)
        reqs = nxt


def rope_tables(pos, dh):
    rot = dh // 4
    half = rot // 2
    inv = ROPE_THETA ** (-jnp.arange(half, dtype=F32) / half)
    ang = pos.astype(F32)[:, None] * inv[None, :]
    cos, sin = jnp.cos(ang), jnp.sin(ang)
    p = pos.shape[0]
    rest1 = jnp.ones((p, dh - rot), F32)
    rest0 = jnp.zeros((p, dh - rot), F32)
    z = jnp.zeros((p, half), F32)
    c = jnp.concatenate([cos, cos, rest1], axis=1)
    up = jnp.concatenate([-sin, z, rest0], axis=1)
    dn = jnp.concatenate([z, sin, rest0], axis=1)
    rep = LANES // dh
    return tuple(jnp.tile(t, (1, rep)) for t in (c, up, dn)), half


def _rope_chunk(xc, c, up, dn, half):
    return xc * c + pltpu.roll(xc, LANES - half, 1) * up + pltpu.roll(xc, half, 1) * dn


def _cols_kernel(*refs, half, width, scale):
    if half:
        x_ref, c_ref, u_ref, d_ref = refs[:4]
        outs = refs[4:]
        c, up, dn = c_ref[...], u_ref[...], d_ref[...]
        x = x_ref[...]
        y = jnp.concatenate(
            [_rope_chunk(x[:, k * LANES:(k + 1) * LANES], c, up, dn, half) for k in range(width // LANES)], axis=1)
    else:
        x_ref = refs[0]
        outs = refs[1:]
        y = x_ref[...]
    if scale is not None:
        y = y * scale
    for o in outs:
        o[...] = y.astype(o.dtype)


def take_cols(x, col0, width, out_dtypes, tabs=None, scale=None):
    t = x.shape[0]
    bw = 512 if width % 512 == 0 else width
    assert col0 % bw == 0
    tables, half = tabs if tabs is not None else ((), 0)
    tm = _pick(t, 1024) if tabs is None else _pick(tables[0].shape[0], 1024)
    assert t % tm == 0
    npb = (tables[0].shape[0] // tm) if tabs is not None else 1
    c0 = col0 // bw
    in_specs = [pl.BlockSpec((tm, bw), lambda i, j: (i, j + c0))]
    in_specs += [pl.BlockSpec((tm, LANES), lambda i, j: (i % npb, 0)) for _ in tables]
    outs = pl.pallas_call(
        functools.partial(_cols_kernel, half=half, width=bw, scale=scale),
        grid=(t // tm, width // bw),
        in_specs=in_specs,
        out_specs=[pl.BlockSpec((tm, bw), lambda i, j: (i, j)) for _ in out_dtypes],
        out_shape=[jax.ShapeDtypeStruct((t, width), dt) for dt in out_dtypes],
        compiler_params=_cparams("parallel", "parallel"), name="take_cols",
    )(x, *tables)
    return outs


def _stack_heads(q, sel=None):
    parts = []
    for r in range(HEADS_PER_GROUP):
        qr = q[:, r * HD:(r + 1) * HD]
        if sel is not None:
            qr = jnp.where(sel, qr, jnp.zeros((), q.dtype))
        parts.append(qr)
    return _concat_rows(parts)


def _concat_rows(parts):
    if parts[0].shape[0] % (2 * SUBLANES):
        return jnp.concatenate([p.astype(F32) for p in parts], axis=0).astype(parts[0].dtype)
    return jnp.concatenate(parts, axis=0)


def _unstack_heads(o, tq):
    return jnp.concatenate([o[r * tq:(r + 1) * tq] for r in range(HEADS_PER_GROUP)], axis=1)


def _qk(q, k):
    return lax.dot_general(q, k, (((1,), (1,)), ((), ())), preferred_element_type=F32)


def _mask_rows(mask, s, tq):
    tk = s.shape[-1]
    return jnp.where(mask[None], s.reshape(-1, tq, tk), NEG_INF).reshape(s.shape)


def _logit_scale(dim):
    return (dim ** -0.5) * LOG2E


def _online_update(slot, s, vt, m_sc, l_sc, acc_sc):
    m_prev = m_sc[slot]
    m_new = jnp.maximum(m_prev, jnp.max(s, axis=-1, keepdims=True))
    alpha = jnp.exp2(m_prev - m_new)
    p = jnp.exp2(s - jnp.concatenate([m_new] * (s.shape[-1] // LANES), axis=1))
    l_sc[slot] = alpha * l_sc[slot] + jnp.sum(p, axis=-1, keepdims=True)
    acc_sc[slot] = alpha * acc_sc[slot] + jnp.dot(p.astype(BF16), vt, preferred_element_type=F32)
    m_sc[slot] = m_new


def _init_state(m_sc, l_sc, acc_sc):
    m_sc[...] = jnp.full(m_sc.shape, M_INIT, F32)
    l_sc[...] = jnp.zeros(l_sc.shape, F32)
    acc_sc[...] = jnp.zeros(acc_sc.shape, F32)


def _order_key(x):
    b = lax.bitcast_convert_type(x, I32)
    return jnp.where(b < 0, b ^ jnp.int32(0x7FFFFFFF), b)


def _select_top(key, thr, k):
    r, w = key.shape
    gt = jnp.where(key > thr, 1.0, 0.0)
    eq = jnp.where(key == thr, 1.0, 0.0)
    need = jnp.float32(k) - jnp.sum(gt, axis=-1, keepdims=True)
    before = (lax.broadcasted_iota(I32, (LANES, LANES), 0) < lax.broadcasted_iota(I32, (LANES, LANES), 1))
    before = jnp.where(before, 1.0, 0.0).astype(BF16)
    ones = jnp.ones((LANES, LANES), BF16)
    taken = jnp.zeros((r, LANES), F32)
    cols = []
    for c in range(w // LANES):
        eq_c = eq[:, c * LANES:(c + 1) * LANES]
        eq_b = eq_c.astype(BF16)
        rank = taken + jnp.dot(eq_b, before, preferred_element_type=F32)
        cols.append(jnp.maximum(gt[:, c * LANES:(c + 1) * LANES], jnp.where(rank < need, eq_c, 0.0)))
        taken = taken + jnp.dot(eq_b, ones, preferred_element_type=F32)
    return jnp.concatenate(cols, axis=1) > 0.5


def _kth_largest_key(key, k):
    kf = jnp.float32(k)

    def count_ge(t):
        return jnp.sum(jnp.where(key >= t, 1.0, 0.0), axis=-1, keepdims=True)

    t0 = jnp.where(count_ge(jnp.int32(0)) >= kf, jnp.int32(0), jnp.int32(INT_MIN))

    def one_bit(bit, t):
        cand = t | (jnp.int32(1) << bit)
        return jnp.where(count_ge(cand) >= kf, cand, t)

    if key.shape[0] * key.shape[1] > 128 * 1024:
        return lax.fori_loop(0, 31, lambda it, t: one_bit(jnp.int32(30) - it, t), t0)

    def two_bits(it, t):
        hi = jnp.int32(1) << (jnp.int32(30) - 2 * it)
        lo = jnp.int32(1) << (jnp.int32(29) - 2 * it)
        c1, c2, c3 = t | lo, t | hi, t | hi | lo
        return jnp.where(count_ge(c3) >= kf, c3,
                         jnp.where(count_ge(c2) >= kf, c2, jnp.where(count_ge(c1) >= kf, c1, t)))

    return one_bit(jnp.int32(0), lax.fori_loop(0, 15, two_bits, t0))


def _n_kv_tiles(qpos0, i, tq, tk, lp):
    hi = qpos0 + (i + 1) * tq
    return jnp.minimum((hi + tk - 1) // tk, lp // tk)


def _kv_tile(lp, tq):
    return _pick(lp, 512 if tq >= 64 else 2048)


def _kv_len(k):
    return k.shape[1] if k.ndim == 3 else k.shape[2]


def _kv_spec(k, bg):
    lp = _kv_len(k)
    if k.ndim == 3:
        return pl.BlockSpec((None, lp, HD), lambda *idx: (bg(*idx)[0], 0, bg(*idx)[1]))
    return pl.BlockSpec((None, None, lp, HD), lambda *idx: (bg(*idx)[0], bg(*idx)[1], 0, 0))


def _softmax_scratch(slots, rows):
    return [pltpu.VMEM((slots, rows, LANES), F32), pltpu.VMEM((slots, rows, LANES), F32),
            pltpu.VMEM((slots, rows, HD), F32)]


def _diff_attn_kernel(lam_ref, q_ref, k_ref, v_ref, g_ref, o_ref, m_sc, l_sc, acc_sc, *, tq, tk, qpos0, lp, post):
    i = pl.program_id(2)
    q = q_ref[...]
    lane = lax.broadcasted_iota(I32, (1, HD), 1)
    lo = lane < A_HALF
    qs = _concat_rows([_stack_heads(q, lo), _stack_heads(q, jnp.logical_not(lo))])
    qpos = qpos0 + i * tq + lax.broadcasted_iota(I32, (tq, 1), 0)
    _init_state(m_sc, l_sc, acc_sc)
    half = HEADS_PER_GROUP * tq

    def step(j, masked):
        off = pl.multiple_of(j * tk, tk)
        s = _qk(qs, k_ref[pl.ds(off, tk), :])
        if masked:
            s = _mask_rows(off + lax.broadcasted_iota(I32, (1, tk), 1) <= qpos, s, tq)
        _online_update(0, s, v_ref[pl.ds(off, tk), :], m_sc, l_sc, acc_sc)

    n_full = jnp.minimum((qpos0 + i * tq + 1) // tk, lp // tk)
    lax.fori_loop(0, n_full, lambda j, c: (step(j, False), c)[1], 0)
    lax.fori_loop(n_full, _n_kv_tiles(qpos0, i, tq, tk, lp), lambda j, c: (step(j, True), c)[1], 0)
    on = acc_sc[0] / l_sc[0]
    o = on[:half] - lam_ref[0] * on[half:]
    ms = jnp.mean(o * o, axis=-1, keepdims=True)
    o = o * lax.rsqrt(ms + EPS) * g_ref[...] * post
    o_ref[...] = _unstack_heads(o, tq).astype(o_ref.dtype)


def diff_attention(q, k, v, lam_f, subln, lam_init, qpos0):
    b, tqn, _ = q.shape
    lp = _kv_len(k)
    tq = _pick(tqn, 256)
    tk = _kv_tile(lp, tq)
    kern = functools.partial(_diff_attn_kernel, tq=tq, tk=tk, qpos0=qpos0, lp=lp, post=1.0 - lam_init)
    return pl.pallas_call(
        kern, grid=(b, GROUPS, tqn // tq),
        in_specs=[pl.BlockSpec(memory_space=pltpu.SMEM),
                  pl.BlockSpec((None, tq, GW), lambda bi, g, i: (bi, i, g)),
                  _kv_spec(k, lambda bi, g, i: (bi, g)),
                  _kv_spec(v, lambda bi, g, i: (bi, g)),
                  pl.BlockSpec((1, HD), lambda bi, g, i: (0, 0))],
        out_specs=pl.BlockSpec((None, tq, GW), lambda bi, g, i: (bi, i, g)),
        out_shape=jax.ShapeDtypeStruct(q.shape, BF16),
        scratch_shapes=_softmax_scratch(1, 2 * HEADS_PER_GROUP * tq),
        compiler_params=_cparams("parallel", "parallel", "parallel"), name="diff_attention",
    )(lam_f.reshape(1).astype(F32), q, k, v, subln.reshape(1, HD))


def _nsa_cmp_kernel(q_ref, fsk_ref, fsv_ref, b_ref, o_ref, sel_ref, *, tq, w, nblk, nch, qpos0, klanes):
    i = pl.program_id(2)
    qst = _stack_heads(q_ref[...])
    fsk = fsk_ref[...]
    fsv = fsv_ref[...]
    kcb = fsk[:, :HD] + pltpu.roll(fsk[:, HD:], w - 1, 0) + b_ref[0:1, :]
    vcb = fsv[:, :HD] + pltpu.roll(fsv[:, HD:], w - 1, 0) + b_ref[1:2, :]
    qpos = qpos0 + i * tq + lax.broadcasted_iota(I32, (tq, 1), 0)
    lanei = lax.broadcasted_iota(I32, (1, w), 1)
    s = _qk(qst, kcb.astype(BF16)) * (B_HD ** -0.5)
    valid = (lanei * CMP_STRIDE + (CMP_LEN - 1) <= qpos) & (lanei < nblk)
    s3 = jnp.where(valid[None], s.reshape(HEADS_PER_GROUP, tq, w), NEG_INF)
    e = jnp.exp(s3 - jnp.max(s3, axis=-1, keepdims=True))
    p = e / jnp.sum(e, axis=-1, keepdims=True)
    any_valid = (qpos >= CMP_LEN - 1) & (nblk > 0)
    p = jnp.where(any_valid[None], p, 0.0)
    o = jnp.dot(p.reshape(HEADS_PER_GROUP * tq, w).astype(BF16), vcb.astype(BF16), preferred_element_type=F32)
    o_ref[...] = _unstack_heads(o, tq).astype(o_ref.dtype)

    grp = p[0] + p[1] + p[2] + p[3]
    chunk = grp + jnp.where(lanei == 0, 0.0, pltpu.roll(grp, 1, 1))
    pair = jnp.where((lanei & 1) == 0, chunk + pltpu.roll(chunk, w - 1, 1), chunk + pltpu.roll(chunk, 1, 1))
    quad = jnp.where((lanei & 2) == 0, pair + pltpu.roll(pair, w - 2, 1), pair + pltpu.roll(pair, 2, 1))
    jb = lanei >> 2
    cur = qpos >> 6
    forced = (jb == 0) | (jb == cur) | (jb == cur - 1)
    score = jnp.where(forced, FORCE, jnp.where(jb * SEL_BLOCK <= qpos, quad, NEG_INF))
    score = jnp.where(lanei < nch, score, NEG_INF)
    key = _order_key(score)
    thr = _kth_largest_key(key, klanes)
    sel_ref[...] = jnp.where(_select_top(key, thr, klanes) | (lanei >= nch), 1.0, 0.0).astype(sel_ref.dtype)


def nsa_compressed(q_raw, fsk, fsv, bias, *, nblk, nch, qpos0, n_top_rep):
    b, tqn, _ = q_raw.shape
    w = fsk.shape[2]
    tq = _pick(tqn, 256)
    kern = functools.partial(_nsa_cmp_kernel, tq=tq, w=w, nblk=nblk, nch=nch, qpos0=qpos0,
                             klanes=(SEL_BLOCK // CMP_STRIDE) * n_top_rep)
    return pl.pallas_call(
        kern, grid=(b, GROUPS, tqn // tq),
        in_specs=[pl.BlockSpec((None, tq, GW), lambda bi, g, i: (bi, i, g)),
                  pl.BlockSpec((None, None, w, 2 * HD), lambda bi, g, i: (bi, g, 0, 0)),
                  pl.BlockSpec((None, None, w, 2 * HD), lambda bi, g, i: (bi, g, 0, 0)),
                  pl.BlockSpec((2, HD), lambda bi, g, i: (0, 0))],
        out_specs=[pl.BlockSpec((None, tq, GW), lambda bi, g, i: (bi, i, g)),
                   pl.BlockSpec((None, None, tq, w), lambda bi, g, i: (bi, g, i, 0))],
        out_shape=[jax.ShapeDtypeStruct(q_raw.shape, F32), jax.ShapeDtypeStruct((b, GROUPS, tqn, w), BF16)],
        compiler_params=_cparams("parallel", "parallel", "parallel"), name="nsa_compressed",
    )(q_raw, fsk, fsv, bias)


def _nsa_sw_kernel(*refs, tq, tk, tkw, qpos0, lp, lw, kwpos0, w, sel_given):
    if sel_given:
        q_ref, kw_ref, vw_ref, osel_ref, oc_ref, gate_ref, o_ref, m_sc, l_sc, acc_sc = refs
    else:
        q_ref, ks_ref, vs_ref, kw_ref, vw_ref, sel_ref, oc_ref, gate_ref, o_ref, m_sc, l_sc, acc_sc = refs
    i = pl.program_id(2)
    qst = _stack_heads(q_ref[...])
    qpos = qpos0 + i * tq + lax.broadcasted_iota(I32, (tq, 1), 0)
    _init_state(m_sc, l_sc, acc_sc)
    chunks_per_tile = tk // CMP_STRIDE
    assert LANES % chunks_per_tile == 0 and w % LANES == 0

    def sel_body(j, carry):
        off = pl.multiple_of(j * tk, tk)
        kt = ks_ref[pl.ds(off, tk), :]
        vt = vs_ref[pl.ds(off, tk), :]
        kpos = off + lax.broadcasted_iota(I32, (1, tk), 1)
        cbase = pl.multiple_of((j * chunks_per_tile) // LANES * LANES, LANES)
        chunk_row = cbase + lax.broadcasted_iota(I32, (LANES, 1), 0)
        expand = jnp.where(chunk_row == (kpos >> 4), 1.0, 0.0).astype(BF16)
        chosen = jnp.dot(sel_ref[:, pl.ds(cbase, LANES)], expand, preferred_element_type=F32) > 0.5
        mask = chosen & (kpos <= qpos)
        _online_update(0, _mask_rows(mask, _qk(qst, kt), tq), vt, m_sc, l_sc, acc_sc)
        return carry

    if not sel_given:
        lax.fori_loop(0, _n_kv_tiles(qpos0, i, tq, tk, lp), sel_body, 0)

    def win_body(j, carry):
        off = pl.multiple_of(j * tkw, tkw)
        kt = kw_ref[pl.ds(off, tkw), :]
        vt = vw_ref[pl.ds(off, tkw), :]
        kpos = kwpos0 + off + lax.broadcasted_iota(I32, (1, tkw), 1)
        dt = qpos - kpos
        mask = (dt >= 0) & (dt <= WINDOW)
        _online_update(1, _mask_rows(mask, _qk(qst, kt), tq), vt, m_sc, l_sc, acc_sc)
        return carry

    first = jnp.maximum(qpos0 + i * tq - WINDOW - kwpos0, 0) // tkw
    last = jnp.minimum((qpos0 + (i + 1) * tq - kwpos0 + tkw - 1) // tkw, lw // tkw)
    lax.fori_loop(first, last, win_body, 0)

    o_sel = osel_ref[...] if sel_given else _unstack_heads(acc_sc[0] / l_sc[0], tq)
    o_win = _unstack_heads(acc_sc[1] / l_sc[1], tq)
    gsig = jax.nn.sigmoid(gate_ref[...])
    oc = oc_ref[...]
    outs = []
    for r in range(HEADS_PER_GROUP):
        cs = slice(r * HD, (r + 1) * HD)
        outs.append(oc[:, cs] * gsig[:, 3 * r:3 * r + 1] + o_sel[:, cs] * gsig[:, 3 * r + 1:3 * r + 2]
                    + o_win[:, cs] * gsig[:, 3 * r + 2:3 * r + 3])
    o_ref[...] = jnp.concatenate(outs, axis=1).astype(o_ref.dtype)


def nsa_select_window(q_rot, ks, vs, kw, vw, selmask, o_cmp, gates, *, qpos0, kwpos0, o_sel=None):
    b, tqn, _ = q_rot.shape
    sel_given = o_sel is not None
    lw = _kv_len(kw)
    lp, w = (lw, LANES) if sel_given else (_kv_len(ks), selmask.shape[3])
    tq = _pick(tqn, 256)
    tk = _kv_tile(lp, tq)
    tkw = _kv_tile(lw, tq)
    kern = functools.partial(_nsa_sw_kernel, tq=tq, tk=tk, tkw=tkw, qpos0=qpos0, lp=lp, lw=lw, kwpos0=kwpos0, w=w,
                             sel_given=sel_given)
    qspec = pl.BlockSpec((None, tq, GW), lambda bi, g, i: (bi, i, g))
    gspec = pl.BlockSpec((None, None, tq, LANES), lambda bi, g, i: (bi, g, i, 0))
    bg = lambda bi, g, i: (bi, g)
    if sel_given:
        in_specs = [qspec, _kv_spec(kw, bg), _kv_spec(vw, bg), qspec, qspec, gspec]
        args = (q_rot, kw, vw, o_sel, o_cmp, gates)
    else:
        in_specs = [qspec, _kv_spec(ks, bg), _kv_spec(vs, bg), _kv_spec(kw, bg), _kv_spec(vw, bg),
                    pl.BlockSpec((None, None, tq, w), lambda bi, g, i: (bi, g, i, 0)), qspec, gspec]
        args = (q_rot, ks, vs, kw, vw, selmask, o_cmp, gates)
    return pl.pallas_call(
        kern, grid=(b, GROUPS, tqn // tq), in_specs=in_specs, out_specs=qspec,
        out_shape=jax.ShapeDtypeStruct(q_rot.shape, BF16),
        scratch_shapes=_softmax_scratch(2, HEADS_PER_GROUP * tq),
        compiler_params=_cparams("parallel", "parallel", "parallel"), name="nsa_select_window",
    )(*args)


def _ik_kernel(t_ref, g_ref, c_ref, u_ref, d_ref, ikf_ref, ik2_ref, *, half):
    x = t_ref[...]
    lane = lax.broadcasted_iota(I32, (1, LANES), 1)
    xm = jnp.where(lane < IDX_DIM, x, 0.0)
    ms = jnp.sum(xm * xm, axis=-1, keepdims=True) * (1.0 / IDX_DIM)
    y = xm * lax.rsqrt(ms + EPS) * g_ref[...]
    y = _rope_chunk(y, c_ref[...], u_ref[...], d_ref[...], half)
    ikf_ref[...] = y
    ik2_ref[...] = (y + pltpu.roll(y, IDX_DIM, 1)).astype(ik2_ref.dtype)


def index_keys(tail, knorm, tabs):
    t = tail.shape[0]
    tables, half = tabs
    tm = _pick(tables[0].shape[0], 512)
    npb = tables[0].shape[0] // tm
    g = jnp.concatenate([knorm.astype(F32), jnp.zeros((LANES - IDX_DIM,), F32)]).reshape(1, LANES)
    rspec = pl.BlockSpec((tm, LANES), lambda i: (i, 0))
    tspec = pl.BlockSpec((tm, LANES), lambda i: (i % npb, 0))
    return pl.pallas_call(
        functools.partial(_ik_kernel, half=half), grid=(t // tm,),
        in_specs=[rspec, pl.BlockSpec((1, LANES), lambda i: (0, 0)), tspec, tspec, tspec],
        out_specs=[rspec, rspec],
        out_shape=[jax.ShapeDtypeStruct((t, LANES), F32), jax.ShapeDtypeStruct((t, LANES), BF16)],
        compiler_params=_cparams("parallel"), name="index_keys",
    )(tail, g, *tables)


def _dsa_index_kernel(iq_ref, ik_ref, tail_ref, mask_ref, sc_sc, *, tq, tk, qpos0, lp, top, n_widths):
    i = pl.program_id(1)
    iq = iq_ref[...]
    lane = lax.broadcasted_iota(I32, (1, LANES), 1)
    lo = lane < IDX_DIM
    hi = jnp.logical_not(lo)
    zero = jnp.zeros((), iq.dtype)
    parts = []
    for h in range(IDX_HEADS):
        pair = iq[:, (h // 2) * LANES:(h // 2 + 1) * LANES]
        parts.append(jnp.where(lo if h % 2 == 0 else hi, pair, zero))
    iqst = _concat_rows(parts)
    iw = tail_ref[:, IDX_DIM:IDX_DIM + IDX_HEADS] * ((IDX_DIM ** -0.5) * (IDX_HEADS ** -0.5))
    qpos = qpos0 + i * tq + lax.broadcasted_iota(I32, (tq, 1), 0)
    sc_sc[...] = jnp.full(sc_sc.shape, NEG_INF, F32)

    def body(j, carry):
        off = pl.multiple_of(j * tk, tk)
        sc = _qk(iqst, ik_ref[pl.ds(off, tk), :])
        acc = jnp.zeros((tq, tk), F32)
        for h in range(IDX_HEADS):
            acc = acc + jnp.maximum(sc[h * tq:(h + 1) * tq], 0.0) * iw[:, h:h + 1]
        kpos = off + lax.broadcasted_iota(I32, (1, tk), 1)
        sc_sc[:, pl.ds(off, tk)] = jnp.where(kpos <= qpos, acc, NEG_INF)
        return carry

    lax.fori_loop(0, _n_kv_tiles(qpos0, i, tq, tk, lp), body, 0)
    key = _order_key(sc_sc[...])
    quarter = lp // 4
    if n_widths > 1 and quarter % LANES == 0 and quarter > top:
        reach = jnp.clip((qpos0 + (i + 1) * tq + quarter - 1) // quarter, 1, 4)
        thr = lax.switch(reach - 1, [functools.partial(_kth_largest_key, key[:, :quarter * n], top)
                                     for n in range(1, 5)])
    else:
        thr = _kth_largest_key(key, top)
    kpos_all = lax.broadcasted_iota(I32, (1, lp), 1)
    mask_ref[...] = jnp.where(_select_top(key, thr, top) & (kpos_all <= qpos), 1.0, 0.0).astype(mask_ref.dtype)


def dsa_select(iq, ik2, tail, *, qpos0, top):
    b, tqn, _ = iq.shape
    lp = ik2.shape[1]
    tq = _pick(tqn, 128)
    tk = _kv_tile(lp, tq)
    kern = functools.partial(_dsa_index_kernel, tq=tq, tk=tk, qpos0=qpos0, lp=lp, top=top,
                             n_widths=4 if tqn > tq else 1)
    return pl.pallas_call(
        kern, grid=(b, tqn // tq),
        in_specs=[pl.BlockSpec((None, tq, IDX_HEADS * IDX_DIM), lambda bi, i: (bi, i, 0)),
                  pl.BlockSpec((None, lp, LANES), lambda bi, i: (bi, 0, 0)),
                  pl.BlockSpec((None, tq, LANES), lambda bi, i: (bi, i, 0))],
        out_specs=pl.BlockSpec((None, tq, lp), lambda bi, i: (bi, i, 0)),
        out_shape=jax.ShapeDtypeStruct((b, tqn, lp), BF16),
        scratch_shapes=[pltpu.VMEM((tq, lp), F32)],
        compiler_params=_cparams("parallel", "parallel"), name="dsa_select",
    )(iq, ik2, tail)


def _masked_attn_kernel(q_ref, k_ref, v_ref, mask_ref, o_ref, m_sc, l_sc, acc_sc, *, tq, tk, qpos0, lp):
    i = pl.program_id(1)
    qst = _stack_heads(q_ref[...])
    _init_state(m_sc, l_sc, acc_sc)

    def body(j, carry):
        off = pl.multiple_of(j * tk, tk)
        kt = k_ref[pl.ds(off, tk), :]
        vt = v_ref[pl.ds(off, tk), :]
        mask = mask_ref[:, pl.ds(off, tk)] > 0.5
        _online_update(0, _mask_rows(mask, _qk(qst, kt), tq), vt, m_sc, l_sc, acc_sc)
        return carry

    lax.fori_loop(0, _n_kv_tiles(qpos0, i, tq, tk, lp), body, 0)
    o_ref[...] = _unstack_heads(acc_sc[0] / l_sc[0], tq).astype(o_ref.dtype)


def masked_attention(q, k, v, mask, *, qpos0):
    b, tqn, _ = q.shape
    lp = _kv_len(k)
    tq = _pick(tqn, 256)
    tk = _kv_tile(lp, tq)
    kern = functools.partial(_masked_attn_kernel, tq=tq, tk=tk, qpos0=qpos0, lp=lp)
    qspec = pl.BlockSpec((None, tq, GW), lambda bi, i, g: (bi, i, g))
    bg = lambda bi, i, g: (bi, g)
    return pl.pallas_call(
        kern, grid=(b, tqn // tq, GROUPS),
        in_specs=[qspec, _kv_spec(k, bg), _kv_spec(v, bg),
                  pl.BlockSpec((None, tq, lp), lambda bi, i, g: (bi, i, 0))],
        out_specs=qspec,
        out_shape=jax.ShapeDtypeStruct(q.shape, BF16),
        scratch_shapes=_softmax_scratch(1, HEADS_PER_GROUP * tq),
        compiler_params=_cparams("parallel", "parallel", "parallel"), name="masked_attention",
    )(q, k, v, mask)


def _cross_kernel(q_ref, k_ref, v_ref, o_ref):
    q = q_ref[...]
    k = k_ref[...].astype(BF16)
    v = v_ref[...].astype(BF16)
    outs = []
    for h in range(X_HEADS):
        cs = slice(h * X_HD, (h + 1) * X_HD)
        s = _qk(q[:, cs], k[:, cs]) * (X_HD ** -0.5)
        e = jnp.exp(s - jnp.max(s, axis=-1, keepdims=True))
        p = e / jnp.sum(e, axis=-1, keepdims=True)
        outs.append(jnp.dot(p.astype(BF16), v[:, cs], preferred_element_type=F32))
    o_ref[...] = jnp.concatenate(outs, axis=1).astype(o_ref.dtype)


def cross_attention(q, mk, mv):
    b, tqn, _ = q.shape
    tq = _pick(tqn, 512)
    ml = mk.shape[1]
    qspec = pl.BlockSpec((None, tq, X_W), lambda bi, i: (bi, i, 0))
    mspec = pl.BlockSpec((None, ml, X_W), lambda bi, i: (bi, 0, 0))
    return pl.pallas_call(
        _cross_kernel, grid=(b, tqn // tq), in_specs=[qspec, mspec, mspec], out_specs=qspec,
        out_shape=jax.ShapeDtypeStruct(q.shape, BF16),
        compiler_params=_cparams("parallel", "parallel"), name="cross_attention",
    )(q, mk, mv)


PAGES_PER_STEP = 16


def _gather_kernel(pt_ref, *refs, pps):
    pools, new_ref, o_ref = refs[:pps], refs[pps], refs[pps + 1]
    s = pl.program_id(1)
    n_steps = pl.num_programs(1) - 1

    @pl.when(s < n_steps)
    def _():
        for k in range(pps):
            x = pools[k][...].astype(o_ref.dtype)
            o_ref[k * PAGE_SIZE:(k + 1) * PAGE_SIZE, :] = jnp.concatenate([x, x], axis=1)

    @pl.when(s == n_steps)
    def _():
        o_ref[...] = jnp.zeros(o_ref.shape, o_ref.dtype)
        o_ref[0:PAGE_SIZE, :] = new_ref[...]


def gather_index_keys(pool, layer, page_table, new_rows):
    b, n_pages = page_table.shape
    pps = math.gcd(PAGES_PER_STEP, n_pages)
    n_steps = n_pages // pps
    rows = pps * PAGE_SIZE
    grid_spec = pltpu.PrefetchScalarGridSpec(
        num_scalar_prefetch=1, grid=(b, n_steps + 1),
        in_specs=_page_specs(pool, layer, pps, n_pages)
        + [pl.BlockSpec((None, PAGE_SIZE, LANES), lambda bi, s, pt: (bi, 0, 0))],
        out_specs=pl.BlockSpec((None, rows, LANES), lambda bi, s, pt: (bi, s, 0)))
    return pl.pallas_call(
        functools.partial(_gather_kernel, pps=pps), grid_spec=grid_spec,
        out_shape=jax.ShapeDtypeStruct((b, (n_steps + 1) * rows, LANES), BF16),
        compiler_params=_cparams("parallel", "arbitrary"), name="gather_index_keys",
    )(page_table, *([pool] * pps), new_rows)


def _paged_steps(pool, page_table):
    n_pages = page_table.shape[1]
    view = pool.reshape(pool.shape[0], pool.shape[1], PAGE_SIZE * GROUPS, HD)
    pps = math.gcd(PAGES_PER_STEP, n_pages)
    return view, pps, n_pages // pps


def _page_specs(view, layer, pps, n_pages):
    def spec(k):
        return pl.BlockSpec((None, None) + view.shape[2:],
                            lambda bi, s, pt: (layer, pt[bi, jnp.minimum(s * pps + k, n_pages - 1)], 0, 0))
    return [spec(k) for k in range(pps)]


def _group_rows(page_refs, g):
    return jnp.concatenate([p[pl.ds(g, PAGE_SIZE, stride=GROUPS), :].astype(BF16) for p in page_refs], axis=0)


def _paged_decode_kernel(pt_ref, *refs, pps, mode, qpos0, past_len, post):
    it = iter(refs)
    lam_ref = next(it) if mode == 'diff' else None
    q_ref = next(it)
    kpools = [next(it) for _ in range(pps)]
    vpools = [next(it) for _ in range(pps)]
    knew_ref, vnew_ref = next(it), next(it)
    mask_ref = next(it) if mode != 'diff' else None
    g_ref = next(it) if mode == 'diff' else None
    o_ref, m_sc, l_sc, acc_sc = next(it), next(it), next(it), next(it)

    step = pl.program_id(1)
    n_steps = pl.num_programs(1) - 1
    tq = q_ref.shape[0]
    rows = pps * PAGE_SIZE
    qpos = qpos0 + lax.broadcasted_iota(I32, (tq, 1), 0)

    @pl.when(step == 0)
    def _():
        _init_state(m_sc, l_sc, acc_sc)

    def queries(g):
        q = q_ref[:, g * GW:(g + 1) * GW]
        if mode != 'diff':
            return _stack_heads(q)
        lo = lax.broadcasted_iota(I32, (1, HD), 1) < A_HALF
        return _concat_rows([_stack_heads(q, lo), _stack_heads(q, jnp.logical_not(lo))])

    def attend(g, kt, vt, mask):
        s = _qk(queries(g), kt)
        if mask is not None:
            s = _mask_rows(mask, s, tq)
        _online_update(g, s, vt, m_sc, l_sc, acc_sc)

    @pl.when(step < n_steps)
    def _():
        mask = mask_ref[...] > 0.5 if mode == 'mask' else None
        if mode == 'chunks':
            assert rows // CMP_STRIDE == LANES
            kpos = step * rows + lax.broadcasted_iota(I32, (1, rows), 1)
            chunk_row = step * LANES + lax.broadcasted_iota(I32, (LANES, 1), 0)
            expand = jnp.where(chunk_row == (kpos >> 4), 1.0, 0.0).astype(BF16)
        for g in range(GROUPS):
            if mode == 'chunks':
                mask = jnp.dot(mask_ref[g], expand, preferred_element_type=F32) > 0.5
            attend(g, _group_rows(kpools, g), _group_rows(vpools, g), mask)

    @pl.when(step == n_steps)
    def _():
        kpos = past_len + lax.broadcasted_iota(I32, (1, PAGE_SIZE), 1)
        mask = kpos <= qpos
        if mode == 'mask':
            mask = mask & (mask_ref[:, 0:PAGE_SIZE] > 0.5)
        for g in range(GROUPS):
            attend(g, knew_ref[g], vnew_ref[g], mask)
        for g in range(GROUPS):
            if mode == 'diff':
                on = acc_sc[g] / l_sc[g]
                o = on[:HEADS_PER_GROUP * tq] - lam_ref[0] * on[HEADS_PER_GROUP * tq:]
                ms = jnp.mean(o * o, axis=-1, keepdims=True)
                o = o * lax.rsqrt(ms + EPS) * g_ref[...] * post
            else:
                o = acc_sc[g] / l_sc[g]
            o_ref[:, g * GW:(g + 1) * GW] = _unstack_heads(o, tq).astype(o_ref.dtype)


def paged_decode_attention(q, kpool, vpool, layer, page_table, k_new, v_new, *, mode, qpos0, out_dtype=BF16,
                           mask=None, lam_f=None, subln=None, lam_init=0.0):
    b, tq, _ = q.shape
    n_pages = page_table.shape[1]
    kview, pps, n_steps = _paged_steps(kpool, page_table)
    vview, _, _ = _paged_steps(vpool, page_table)
    rows = pps * PAGE_SIZE
    n_c = 2 if mode == 'diff' else 1
    in_specs, args = [], []
    if mode == 'diff':
        in_specs.append(pl.BlockSpec(memory_space=pltpu.SMEM))
        args.append(lam_f.reshape(1).astype(F32))
    in_specs.append(pl.BlockSpec((None, tq, D_MODEL), lambda bi, s, pt: (bi, 0, 0)))
    args.append(q)
    in_specs += _page_specs(kview, layer, pps, n_pages) + _page_specs(vview, layer, pps, n_pages)
    args += [kview] * pps + [vview] * pps
    new_spec = pl.BlockSpec((None, GROUPS, PAGE_SIZE, HD), lambda bi, s, pt: (bi, 0, 0, 0))
    in_specs += [new_spec, new_spec]
    args += [k_new, v_new]
    if mode == 'mask':
        in_specs.append(pl.BlockSpec((None, tq, rows), lambda bi, s, pt: (bi, 0, s)))
        args.append(mask)
    elif mode == 'chunks':
        in_specs.append(pl.BlockSpec((None, GROUPS, tq, LANES), lambda bi, s, pt: (bi, 0, 0, s)))
        args.append(mask)
    else:
        in_specs.append(pl.BlockSpec((1, HD), lambda bi, s, pt: (0, 0)))
        args.append(subln.reshape(1, HD))
    kern = functools.partial(_paged_decode_kernel, pps=pps, mode=mode, qpos0=qpos0,
                             past_len=n_pages * PAGE_SIZE, post=1.0 - lam_init)
    grid_spec = pltpu.PrefetchScalarGridSpec(
        num_scalar_prefetch=1, grid=(b, n_steps + 1), in_specs=in_specs,
        out_specs=pl.BlockSpec((None, tq, D_MODEL), lambda bi, s, pt: (bi, 0, 0)),
        scratch_shapes=_softmax_scratch(GROUPS, n_c * HEADS_PER_GROUP * tq))
    return pl.pallas_call(
        kern, grid_spec=grid_spec, out_shape=jax.ShapeDtypeStruct(q.shape, out_dtype),
        compiler_params=_cparams("parallel", "arbitrary"), name="paged_decode_attention",
    )(page_table, *args)


def _paged_compress_kernel(pt_ref, *refs, pps):
    pools, w_ref, o_ref = refs[:pps], refs[pps], refs[pps + 1]
    cpp = PAGE_SIZE // CMP_STRIDE
    acc = jnp.zeros((GROUPS * pps * cpp, 2 * HD), F32)
    for j in range(CMP_STRIDE):
        x = jnp.concatenate([p[pl.ds(GROUPS * j + g, cpp, stride=GROUPS * CMP_STRIDE), :]
                             for g in range(GROUPS) for p in pools], axis=0)
        acc = acc + jnp.dot(x.astype(BF16), w_ref[j], preferred_element_type=F32)
    for g in range(GROUPS):
        o_ref[g] = acc[g * pps * cpp:(g + 1) * pps * cpp]


def paged_compress(pool, layer, page_table, w):
    b, n_pages = page_table.shape
    view, pps, n_steps = _paged_steps(pool, page_table)
    cps = pps * (PAGE_SIZE // CMP_STRIDE)
    grid_spec = pltpu.PrefetchScalarGridSpec(
        num_scalar_prefetch=1, grid=(b, n_steps),
        in_specs=_page_specs(view, layer, pps, n_pages) + [pl.BlockSpec(w.shape, lambda bi, s, pt: (0, 0, 0))],
        out_specs=pl.BlockSpec((None, GROUPS, cps, 2 * HD), lambda bi, s, pt: (bi, 0, s, 0)))
    return pl.pallas_call(
        functools.partial(_paged_compress_kernel, pps=pps), grid_spec=grid_spec,
        out_shape=jax.ShapeDtypeStruct((b, GROUPS, n_steps * cps, 2 * HD), F32),
        compiler_params=_cparams("parallel", "parallel"), name="paged_compress",
    )(page_table, *([view] * pps), w)


def lambda_init(layer):
    return 0.8 - 0.6 * math.exp(-0.3 * layer)


def _pad_rows(x, n):
    return jnp.pad(x, ((0, 0), (0, n - x.shape[1]), (0, 0)))


def trunk(x, nb, tq_real, qpos0, mem_k, mem_v, W, past):
    t = nb * tq_real
    tqp = max(tq_real, SUBLANES)
    pos = qpos0 + jnp.arange(tq_real, dtype=I32)
    pos_rows = jnp.tile(pos, nb) if tq_real < SUBLANES else pos
    tab64 = rope_tables(pos_rows, 64)
    tab128 = rope_tables(pos_rows, 128)
    new = {}

    def to_attn(a):
        return _pad_rows(a.reshape(nb, tq_real, a.shape[-1]), tqp)

    def from_attn(a):
        return a[:, :tq_real].reshape(t, a.shape[-1])

    if past is not None:
        pt = past['page_table']
        past_len = pt.shape[1] * PAGE_SIZE

        def new_block(new_bf):
            nr = new_bf.reshape(nb, tq_real, GROUPS, HD).transpose(0, 2, 1, 3)
            return jnp.pad(nr, ((0, 0), (0, 0), (0, PAGE_SIZE - tq_real), (0, 0)))
    else:
        past_len = 0

    for l in range(DEPTH):
        x = yield from _ffn_steps(x, W['norm_ffn1'], W['ffn1_w_in'], W['ffn1_w_out'], l)
        h = rmsnorm(x, W['norm_mix'][l], BF16)
        i = l // N_MIXERS
        kind = l % N_MIXERS
        if kind == 0:
            proj = yield (linear, h, None, dict(w=W['a_w_in'], layer=i))
            (q_rot,) = take_cols(proj, 0, A_Q, (BF16,), tab64, scale=_logit_scale(A_HALF))
            k_f, k_b = take_cols(proj, A_Q, A_K, (F32, BF16), tab64)
            (v_b,) = take_cols(proj, A_Q + A_K, A_KV_HEADS * A_VDIM, (BF16,))
            v_f = proj[:, A_Q + A_K:]
            new.setdefault('a_k', []).append(k_f.reshape(nb, tq_real, A_KV_HEADS, 2 * A_HALF))
            new.setdefault('a_v', []).append(v_f.reshape(nb, tq_real, A_KV_HEADS, A_VDIM))
            lam = W['a_lambda'][i]
            lam_f = (jnp.exp(jnp.sum(lam[0] * lam[1])) - jnp.exp(jnp.sum(lam[2] * lam[3]))).astype(F32) + lambda_init(l)
            if past is None:
                o = diff_attention(to_attn(q_rot), k_b.reshape(nb, tq_real, -1), v_b.reshape(nb, tq_real, -1),
                                   lam_f, W['a_subln'][i], lambda_init(l), qpos0)
            else:
                o = paged_decode_attention(to_attn(q_rot), past['a_k'], past['a_v'], i, pt, new_block(k_b),
                                           new_block(v_b), mode='diff', qpos0=qpos0, lam_f=lam_f,
                                           subln=W['a_subln'][i], lam_init=lambda_init(l))
            y_in, w_out = from_attn(o), W['a_w_out']
        elif kind == 1:
            w_in = W['b_w_in']
            n_main = B_Q + 6 * B_KV
            proj = yield (linear, h, None, dict(w=w_in, layer=i, ncols=n_main))
            w_tail = jnp.pad(w_in[i, :, n_main:], ((0, 0), (0, LANES - (w_in.shape[-1] - n_main))))
            gate_logits = yield (linear, h, None, dict(w=w_tail, tn=LANES))
            (q_raw,) = take_cols(proj, 0, B_Q, (BF16,))
            (q_rot,) = take_cols(proj, 0, B_Q, (BF16,), tab128, scale=_logit_scale(B_HD))
            kc_f = proj[:, B_Q:B_Q + B_KV]
            vc_f = proj[:, B_Q + B_KV:B_Q + 2 * B_KV]
            ks_f, ks_b = take_cols(proj, B_Q + 2 * B_KV, B_KV, (F32, BF16), tab128)
            vs_f = proj[:, B_Q + 3 * B_KV:B_Q + 4 * B_KV]
            (vs_b,) = take_cols(proj, B_Q + 3 * B_KV, B_KV, (BF16,))
            kw_f, kw_b = take_cols(proj, B_Q + 4 * B_KV, B_KV, (F32, BF16), tab128)
            vw_f = proj[:, B_Q + 5 * B_KV:B_Q + 6 * B_KV]
            (vw_b,) = take_cols(proj, B_Q + 5 * B_KV, B_KV, (BF16,))
            shp = (nb, tq_real, B_KV_HEADS, B_HD)
            for nm, a in (('b_cmp_k', kc_f), ('b_cmp_v', vc_f), ('b_sel_k', ks_f), ('b_sel_v', vs_f)):
                new.setdefault(nm, []).append(a.reshape(shp))

            cw = W['b_cmp_w'][i]
            wfs = [jnp.concatenate([cw[s, :CMP_STRIDE], cw[s, CMP_STRIDE:]], axis=-1) for s in range(2)]
            if past is None:
                l_all = tq_real
                ks_all, vs_all = ks_b.reshape(nb, tq_real, -1), vs_b.reshape(nb, tq_real, -1)
                kw_all, vw_all = kw_b.reshape(nb, tq_real, -1), vw_b.reshape(nb, tq_real, -1)
                kwpos0 = 0
                keep = min(WINDOW, tq_real)
                new.setdefault('b_win_k', []).append(kw_f.reshape(shp)[:, tq_real - keep:])
                new.setdefault('b_win_v', []).append(vw_f.reshape(shp)[:, tq_real - keep:])
                lp_sel = tq_real
            else:
                l_all = past_len + tq_real
                bkw = past['b_win_k'][i].reshape(nb, -1, B_KV)
                bvw = past['b_win_v'][i].reshape(nb, -1, B_KV)
                wb = bkw.shape[1]
                kb = jnp.concatenate([bkw, kw_f.reshape(nb, tq_real, B_KV)], axis=1)
                vb = jnp.concatenate([bvw, vw_f.reshape(nb, tq_real, B_KV)], axis=1)
                new.setdefault('b_win_k', []).append(kb[:, tq_real:].reshape(nb, wb, B_KV_HEADS, B_HD))
                new.setdefault('b_win_v', []).append(vb[:, tq_real:].reshape(nb, wb, B_KV_HEADS, B_HD))
                lw = -(-(wb + tq_real) // 512) * 512
                kw_all = _pad_rows(kb, lw).astype(BF16)
                vw_all = _pad_rows(vb, lw).astype(BF16)
                kwpos0 = qpos0 - wb
                lp_sel = past_len + math.gcd(PAGES_PER_STEP, pt.shape[1]) * PAGE_SIZE
            nch = (l_all // CMP_STRIDE)
            nblk = nch - 1
            n_sel = -(-l_all // SEL_BLOCK)
            n_rep = nch // (SEL_BLOCK // CMP_STRIDE)
            assert nch % (SEL_BLOCK // CMP_STRIDE) == 0 and n_sel - n_rep in (0, 1)
            n_top_rep = min(SEL_N, n_sel) - (n_sel - n_rep)
            wch = -(-(lp_sel // CMP_STRIDE) // LANES) * LANES

            def partials(rows, wf):
                xg = rows.reshape(nb, nch, CMP_STRIDE, B_KV_HEADS, B_HD)
                xg = xg.transpose(0, 3, 1, 2, 4).reshape(nb * B_KV_HEADS * nch, CMP_STRIDE * B_HD)
                return linear(xg, wf.reshape(CMP_STRIDE * B_HD, 2 * B_HD), tn=2 * B_HD).reshape(
                    nb, B_KV_HEADS, nch, 2 * B_HD)

            if past is None:
                fsk, fsv = partials(kc_f, wfs[0]), partials(vc_f, wfs[1])
            else:
                fsk = paged_compress(past['b_cmp_k'], i, pt, wfs[0].astype(BF16))
                fsv = paged_compress(past['b_cmp_v'], i, pt, wfs[1].astype(BF16))
            fsk, fsv = (jnp.pad(f, ((0, 0), (0, 0), (0, wch - nch), (0, 0))) for f in (fsk, fsv))
            o_cmp, selmask = nsa_compressed(to_attn(q_raw), fsk, fsv, W['b_cmp_b'][i].astype(F32),
                                            nblk=nblk, nch=nch, qpos0=qpos0, n_top_rep=n_top_rep)
            gates = gate_logits[:, :3 * B_HEADS].reshape(nb, tq_real, B_KV_HEADS, 3 * HEADS_PER_GROUP)
            gates = jnp.pad(gates.transpose(0, 2, 1, 3),
                            ((0, 0), (0, 0), (0, tqp - tq_real), (0, LANES - 3 * HEADS_PER_GROUP)))
            if past is None:
                o = nsa_select_window(to_attn(q_rot), ks_all, vs_all, kw_all, vw_all, selmask, o_cmp, gates,
                                      qpos0=qpos0, kwpos0=kwpos0)
            else:
                o_sel = paged_decode_attention(to_attn(q_rot), past['b_sel_k'], past['b_sel_v'], i, pt,
                                               new_block(ks_b), new_block(vs_b), mode='chunks', qpos0=qpos0,
                                               mask=selmask, out_dtype=F32)
                o = nsa_select_window(to_attn(q_rot), None, None, kw_all, vw_all, None, o_cmp, gates,
                                      qpos0=qpos0, kwpos0=kwpos0, o_sel=o_sel)
            y_in, w_out = from_attn(o), W['b_w_out']
        else:
            w_in = W['c_w_in']
            n_main = C_Q + 2 * C_KV + IDX_HEADS * IDX_DIM
            proj = yield (linear, h, None, dict(w=w_in, layer=i, ncols=n_main))
            w_tail = jnp.pad(w_in[i, :, n_main:], ((0, 0), (0, LANES - (w_in.shape[-1] - n_main))))
            tail = yield (linear, h, None, dict(w=w_tail, tn=LANES))
            (q_rot,) = take_cols(proj, 0, C_Q, (BF16,), tab128, scale=_logit_scale(C_HD))
            k_f, k_b = take_cols(proj, C_Q, C_KV, (F32, BF16), tab128)
            v_f = proj[:, C_Q + C_KV:C_Q + 2 * C_KV]
            (v_b,) = take_cols(proj, C_Q + C_KV, C_KV, (BF16,))
            (iq_rot,) = take_cols(proj, C_Q + 2 * C_KV, IDX_HEADS * IDX_DIM, (BF16,), tab64)
            ik_f, ik2 = index_keys(tail, W['c_idx_knorm'][i], tab64)
            shp = (nb, tq_real, C_KV_HEADS, C_HD)
            new.setdefault('c_k', []).append(k_f.reshape(shp))
            new.setdefault('c_v', []).append(v_f.reshape(shp))
            new.setdefault('c_idx_k', []).append(ik_f[:, :IDX_DIM].reshape(nb, tq_real, IDX_DIM))
            l_all = past_len + tq_real
            top = min(IDX_TOPK_MAX, l_all // 4)
            if past is None:
                mask = dsa_select(to_attn(iq_rot), ik2.reshape(nb, tq_real, LANES), to_attn(tail), qpos0=qpos0, top=top)
                o = masked_attention(to_attn(q_rot), k_b.reshape(nb, tq_real, -1), v_b.reshape(nb, tq_real, -1),
                                     mask, qpos0=qpos0)
            else:
                ik_all = gather_index_keys(past['c_idx_k'], i, pt,
                                           _pad_rows(ik2.reshape(nb, tq_real, LANES), PAGE_SIZE))
                mask = dsa_select(to_attn(iq_rot), ik_all, to_attn(tail), qpos0=qpos0, top=top)
                o = paged_decode_attention(to_attn(q_rot), past['c_k'], past['c_v'], i, pt, new_block(k_b),
                                           new_block(v_b), mode='mask', qpos0=qpos0, mask=mask)
            y_in, w_out = from_attn(o), W['c_w_out']
        x = yield (linear, y_in, x, dict(w=w_out, layer=i, scale=1.0))

        h = rmsnorm(x, W['norm_cross'][l], BF16)
        qx = yield (linear, h, None, dict(w=W['x_w_q'], layer=l, out_dtype=BF16))
        ox = cross_attention(to_attn(qx), mem_k[l], mem_v[l])
        x = yield (linear, from_attn(ox), x, dict(w=W['x_w_o'], layer=l, scale=1.0))
        x = yield from _ffn_steps(x, W['norm_ffn2'], W['ffn2_w_in'], W['ffn2_w_out'], l)
    y = rmsnorm(x, W['final_norm'], F32)
    return y, {nm: jnp.stack(v) for nm, v in new.items()}


def kernel(x_prompt, x_sample, cache_a_k, cache_a_v, cache_b_cmp_k, cache_b_cmp_v, cache_b_sel_k, cache_b_sel_v, state_b_win_k, state_b_win_v, cache_c_k, cache_c_v, cache_c_idx_k, cache_mem_k, cache_mem_v, page_table, mem_prompt, norm_ffn1, norm_mix, norm_cross, norm_ffn2, final_norm, ffn1_w_in, ffn1_w_out, ffn2_w_in, ffn2_w_out, x_w_q, x_w_kv, x_w_o, a_w_in, a_w_out, a_lambda, a_subln, b_w_in, b_w_out, b_cmp_w, b_cmp_b, c_w_in, c_w_out, c_idx_knorm):
    W = dict(norm_ffn1=norm_ffn1, norm_mix=norm_mix, norm_cross=norm_cross, norm_ffn2=norm_ffn2,
             final_norm=final_norm, ffn1_w_in=ffn1_w_in, ffn1_w_out=ffn1_w_out, ffn2_w_in=ffn2_w_in,
             ffn2_w_out=ffn2_w_out, x_w_q=x_w_q, x_w_o=x_w_o, a_w_in=a_w_in, a_w_out=a_w_out,
             a_lambda=a_lambda, a_subln=a_subln, b_w_in=b_w_in, b_w_out=b_w_out, b_cmp_w=b_cmp_w,
             b_cmp_b=b_cmp_b, c_w_in=c_w_in, c_w_out=c_w_out, c_idx_knorm=c_idx_knorm)
    nbp, seq, d = x_prompt.shape
    nbs, dseq, _ = x_sample.shape
    ml = mem_prompt.shape[1]

    mem2d = mem_prompt.reshape(nbp * ml, d)
    mkv = [linear(mem2d, x_w_kv, l) for l in range(DEPTH)]
    p_mem_k = jnp.stack([m[:, :X_W].reshape(nbp, ml, X_HEADS, X_HD) for m in mkv])
    p_mem_v = jnp.stack([m[:, X_W:].reshape(nbp, ml, X_HEADS, X_HD) for m in mkv])
    prompt = trunk(x_prompt.reshape(nbp * seq, d), nbp, seq, 0,
                   p_mem_k.reshape(DEPTH, nbp, ml, X_W), p_mem_v.reshape(DEPTH, nbp, ml, X_W), W, None)

    past = dict(a_k=cache_a_k, a_v=cache_a_v, b_cmp_k=cache_b_cmp_k, b_cmp_v=cache_b_cmp_v,
                b_sel_k=cache_b_sel_k, b_sel_v=cache_b_sel_v, b_win_k=state_b_win_k, b_win_v=state_b_win_v,
                c_k=cache_c_k, c_v=cache_c_v, c_idx_k=cache_c_idx_k, page_table=page_table)
    past_len = page_table.shape[1] * PAGE_SIZE
    sml = cache_mem_k.shape[2]
    sample = trunk(x_sample.reshape(nbs * dseq, d), nbs, dseq, past_len,
                   cache_mem_k.reshape(DEPTH, nbs, sml, X_W), cache_mem_v.reshape(DEPTH, nbs, sml, X_W), W, past)
    (y_p, ps), (y_s, ss) = _run_together([prompt, sample])

    return (y_p.reshape(nbp, seq, d), y_s.reshape(nbs, dseq, d),
            ps['a_k'], ps['a_v'], ps['b_cmp_k'], ps['b_cmp_v'], ps['b_sel_k'], ps['b_sel_v'],
            ps['b_win_k'], ps['b_win_v'], ps['c_k'], ps['c_v'], ps['c_idx_k'], p_mem_k, p_mem_v,
            ss['a_k'], ss['a_v'], ss['b_cmp_k'], ss['b_cmp_v'], ss['b_sel_k'], ss['b_sel_v'],
            ss['b_win_k'], ss['b_win_v'], ss['c_k'], ss['c_v'], ss['c_idx_k'])
```
